```python
import jax, jax.numpy as jnp
from jax import lax
import numpy as np

D_MODEL = 1024
BATCH = 16
SEQ = 256
DEPTH = 2
DEC_BATCH = 8
DEC_SEQ = 4096
PAST_LEN = 256

GRID_W = 64
POOL_WINDOWS = (2, 4, 8, 16)
POOL_GROUPS = len(POOL_WINDOWS)
POOL_DIM = D_MODEL
POOL_GROUP_DIM = POOL_DIM // POOL_GROUPS
SSD_EXPAND = 2
D_INNER = SSD_EXPAND * D_MODEL
SSD_HEAD_DIM = 64
SSD_HEADS = D_INNER // SSD_HEAD_DIM
SSD_GROUPS = 4
SSD_STATE = 128
SSD_CONV = 4
SSD_CHUNK = 128
CONV_DIM = D_INNER + 2 * SSD_GROUPS * SSD_STATE
N_BRANCH = 2
IN_COLS = POOL_DIM + D_INNER + CONV_DIM + 2 * SSD_HEADS + N_BRANCH * D_MODEL
D_FF = 2816
N_MOD = 9
EPS = 1e-6

kernel_name = "hybrid_pool_ssd_diffusion_step"


def _rmsnorm(x, g):
    xf = x.astype(jnp.float32)
    y = xf * lax.rsqrt(jnp.mean(xf * xf, axis=-1, keepdims=True) + EPS)
    return (y * g.astype(jnp.float32)).astype(x.dtype)


def _modulate(x, g, shift, scale):
    return _rmsnorm(x, g) * (1 + scale) + shift


def _adaln(cond, w, b):
    mod = jax.nn.silu(cond) @ w + b
    return mod.reshape(cond.shape[0], N_MOD, 1, D_MODEL)


def _swiglu(h, w13, w2):
    g, u = jnp.split(h @ w13, 2, axis=-1)
    return (jax.nn.silu(g) * u) @ w2


def _dwconv_centred(u, w, b):
    K = w.shape[0]
    left = K // 2
    T = u.shape[1]
    up = jnp.pad(u, ((0, 0), (left, K - 1 - left), (0, 0)))
    out = b + up[:, 0:T] * w[0]
    for k in range(1, K):
        out = out + up[:, k:k + T] * w[k]
    return out


def _bounds(n, w):
    idx = jnp.arange(n)
    lo = jnp.clip(idx - w // 2, 0, n)
    hi = jnp.clip(idx - w // 2 + w, 0, n)
    return lo, hi


def _pool_seq(u, w):
    T = u.shape[1]
    s = jnp.pad(jnp.cumsum(u.astype(jnp.float32), axis=1), ((0, 0), (1, 0), (0, 0)))
    lo, hi = _bounds(T, w)
    cnt = (hi - lo).astype(jnp.float32)[None, :, None]
    return ((s[:, hi] - s[:, lo]) / cnt).astype(u.dtype)


def _pool_grid(u, w):
    R, W = u.shape[1], u.shape[2]
    s = jnp.cumsum(jnp.cumsum(u.astype(jnp.float32), axis=1), axis=2)
    s = jnp.pad(s, ((0, 0), (1, 0), (1, 0), (0, 0)))
    rlo, rhi = _bounds(R, w)
    clo, chi = _bounds(W, w)
    s_hi = s[:, rhi]
    s_lo = s[:, rlo]
    tot = (s_hi[:, :, chi] - s_hi[:, :, clo]) - (s_lo[:, :, chi] - s_lo[:, :, clo])
    cnt = ((rhi - rlo)[:, None] * (chi - clo)[None, :]).astype(jnp.float32)[None, :, :, None]
    return (tot / cnt).astype(u.dtype)


def _pool_mixer(u, pool_w, pool_scale, rows):
    b, T, _ = u.shape
    outs = []
    for g, w in enumerate(POOL_WINDOWS):
        ug = u[..., g * POOL_GROUP_DIM:(g + 1) * POOL_GROUP_DIM]
        if rows is None:
            pooled = _pool_seq(ug, w)
        else:
            pooled = _pool_grid(ug.reshape(b, rows, GRID_W, POOL_GROUP_DIM), w).reshape(b, T, POOL_GROUP_DIM)
        outs.append(pooled - ug)
    d = jnp.stack(outs, axis=2)
    y = jnp.einsum('btgc,gcd->btgd', d, pool_w).reshape(b, T, POOL_DIM)
    return y * pool_scale


def _ssd_scan(x, dt, a, bmat, cmat, h0):
    b, T, H, P = x.shape
    G, N = bmat.shape[2], bmat.shape[3]
    Hg = H // G
    L = SSD_CHUNK
    nc = T // L
    xf = (x.astype(jnp.float32) * dt[..., None]).reshape(b, nc, L, G, Hg, P)
    da = (dt * a).reshape(b, nc, L, G, Hg)
    bc = bmat.astype(jnp.float32).reshape(b, nc, L, G, N)
    cc = cmat.astype(jnp.float32).reshape(b, nc, L, G, N)
    cs = jnp.cumsum(da, axis=2)
    seg = cs[:, :, :, None] - cs[:, :, None]
    causal = jnp.tril(jnp.ones((L, L), dtype=bool))[:, :, None, None]
    decay = jnp.where(causal, jnp.exp(jnp.where(causal, seg, 0.0)), 0.0)
    scores = jnp.einsum('bclgn,bcsgn->bclsg', cc, bc)
    m = scores[..., None] * decay
    y_diag = jnp.einsum('bclsgh,bcsghp->bclghp', m, xf)
    xw = xf * jnp.exp(cs[:, :, -1:] - cs)[..., None]
    states = jnp.einsum('bclgn,bclghp->bcghpn', bc, xw)
    chunk_decay = jnp.exp(cs[:, :, -1])

    def step(h, inp):
        st, dec = inp
        return h * dec[..., None, None] + st, h

    h0g = h0.astype(jnp.float32).reshape(b, G, Hg, P, N)
    h_last, h_prev = lax.scan(step, h0g, (jnp.moveaxis(states, 1, 0), jnp.moveaxis(chunk_decay, 1, 0)))
    h_prev = jnp.moveaxis(h_prev, 0, 1)
    y_off = jnp.einsum('bclgn,bcghpn->bclghp', cc, h_prev) * jnp.exp(cs)[..., None]
    y = (y_diag + y_off).reshape(b, T, H, P)
    return y, h_last.reshape(b, H, P, N)


def _token_mixer(h, rows, h0_f, h0_b, lp):
    b, T, _ = h.shape
    proj = h @ lp['w_in']
    s1 = POOL_DIM
    s2 = s1 + D_INNER
    s3 = s2 + CONV_DIM
    s4 = s3 + 2 * SSD_HEADS
    u_pool, z, xbc, dt_raw, gates = jnp.split(proj, [s1, s2, s3, s4], axis=-1)
    y_pool = _pool_mixer(u_pool, lp['pool_w'], lp['pool_scale'], rows)
    xbc = jax.nn.silu(_dwconv_centred(xbc, lp['conv_w'], lp['conv_b']))
    xs, bm, cm = jnp.split(xbc, [D_INNER, D_INNER + SSD_GROUPS * SSD_STATE], axis=-1)
    xs = xs.reshape(b, T, SSD_HEADS, SSD_HEAD_DIM)
    bm = bm.reshape(b, T, SSD_GROUPS, SSD_STATE)
    cm = cm.reshape(b, T, SSD_GROUPS, SSD_STATE)
    dt = jax.nn.softplus(dt_raw.astype(jnp.float32).reshape(b, T, 2, SSD_HEADS) + lp['dt_bias'].astype(jnp.float32))
    a = -jnp.exp(lp['a_log'].astype(jnp.float32))
    y_f, hf = _ssd_scan(xs, dt[:, :, 0], a[0], bm, cm, h0_f)
    flip = lambda t: jnp.flip(t, axis=1)
    y_b, hb = _ssd_scan(flip(xs), flip(dt[:, :, 1]), a[1], flip(bm), flip(cm), h0_b)
    y = y_f + flip(y_b) + lp['d_skip'].astype(jnp.float32)[:, None] * xs.astype(jnp.float32)
    y = y.reshape(b, T, D_INNER).astype(h.dtype)
    y = _rmsnorm(y * jax.nn.silu(z), lp['ssd_norm'])
    g_a, g_b = jnp.split(jax.nn.sigmoid(gates), 2, axis=-1)
    merged = g_a * (y_pool @ lp['w_branch_pool']) + g_b * (y @ lp['w_branch_ssd'])
    return merged @ lp['w_out'], hf, hb


def _layer(x, mod, rows, h0_f, h0_b, lp):
    ng = lp['norm_g']
    x = x + 0.5 * mod[:, 2] * _swiglu(_modulate(x, ng[0], mod[:, 0], mod[:, 1]), lp['ffn1_w13'], lp['ffn1_w2'])
    m, hf, hb = _token_mixer(_modulate(x, ng[1], mod[:, 3], mod[:, 4]), rows, h0_f, h0_b, lp)
    x = x + mod[:, 5] * m
    x = x + 0.5 * mod[:, 8] * _swiglu(_modulate(x, ng[2], mod[:, 6], mod[:, 7]), lp['ffn2_w13'], lp['ffn2_w2'])
    return x, hf, hb


def setup_inputs(seed: int = 0) -> dict:
    key = jax.random.key(seed)
    ks = jax.random.split(key, 32)
    f32 = jnp.float32
    nrm = lambda k, shape, s: jax.random.normal(k, shape, f32) * s
    sd = (DEC_BATCH, DEPTH, SSD_HEADS, SSD_HEAD_DIM, SSD_STATE)
    dt0 = jnp.exp(jax.random.uniform(ks[17], (DEPTH, 2, SSD_HEADS), f32, np.log(1e-3), np.log(1e-1)))
    return {
        'x_prompt': nrm(ks[0], (BATCH, SEQ, D_MODEL), 1.0),
        'x_sample': nrm(ks[1], (DEC_BATCH, DEC_SEQ, D_MODEL), 1.0),
        'state_ssd_fwd': nrm(ks[2], sd, 0.5),
        'state_ssd_bwd': nrm(ks[3], sd, 0.5),
        'c': nrm(ks[4], (DEC_BATCH, D_MODEL), 1.0),
        'c_ctx': nrm(ks[5], (D_MODEL,), 1.0),
        'ada_w': nrm(ks[6], (DEPTH, D_MODEL, N_MOD * D_MODEL), D_MODEL ** -0.5),
        'ada_b': nrm(ks[7], (DEPTH, N_MOD * D_MODEL), 0.02),
        'norm_g': 1.0 + nrm(ks[8], (DEPTH, 3, D_MODEL), 0.02),
        'ffn1_w13': nrm(ks[9], (DEPTH, D_MODEL, 2 * D_FF), D_MODEL ** -0.5),
        'ffn1_w2': nrm(ks[10], (DEPTH, D_FF, D_MODEL), D_FF ** -0.5),
        'ffn2_w13': nrm(ks[11], (DEPTH, D_MODEL, 2 * D_FF), D_MODEL ** -0.5),
        'ffn2_w2': nrm(ks[12], (DEPTH, D_FF, D_MODEL), D_FF ** -0.5),
        'w_in': nrm(ks[13], (DEPTH, D_MODEL, IN_COLS), D_MODEL ** -0.5),
        'pool_w': nrm(ks[14], (DEPTH, POOL_GROUPS, POOL_GROUP_DIM, POOL_GROUP_DIM), POOL_GROUP_DIM ** -0.5),
        'pool_scale': 1.0 + nrm(ks[15], (DEPTH, POOL_DIM), 0.1),
        'conv_w': nrm(ks[16], (DEPTH, SSD_CONV, CONV_DIM), SSD_CONV ** -0.5),
        'conv_b': nrm(ks[18], (DEPTH, CONV_DIM), 0.02),
        'a_log': jnp.log(jax.random.uniform(ks[19], (DEPTH, 2, SSD_HEADS), f32, 1.0, 16.0)),
        'dt_bias': dt0 + jnp.log(-jnp.expm1(-dt0)),
        'd_skip': 1.0 + nrm(ks[20], (DEPTH, SSD_HEADS), 0.1),
        'ssd_norm': 1.0 + nrm(ks[21], (DEPTH, D_INNER), 0.02),
        'w_branch_pool': nrm(ks[22], (DEPTH, POOL_DIM, D_MODEL), POOL_DIM ** -0.5),
        'w_branch_ssd': nrm(ks[23], (DEPTH, D_INNER, D_MODEL), D_INNER ** -0.5),
        'w_out': nrm(ks[24], (DEPTH, D_MODEL, D_MODEL), D_MODEL ** -0.5),
        'final_norm': 1.0 + nrm(ks[25], (D_MODEL,), 0.02),
    }


def reference(x_prompt, x_sample, state_ssd_fwd, state_ssd_bwd, c, c_ctx, ada_w, ada_b, norm_g,
              ffn1_w13, ffn1_w2, ffn2_w13, ffn2_w2, w_in, pool_w, pool_scale, conv_w, conv_b,
              a_log, dt_bias, d_skip, ssd_norm, w_branch_pool, w_branch_ssd, w_out, final_norm):
    bp = x_prompt.shape[0]
    rows = x_sample.shape[1] // GRID_W
    zeros = jnp.zeros((bp, SSD_HEADS, SSD_HEAD_DIM, SSD_STATE), x_prompt.dtype)
    xc = x_prompt
    xl = x_sample
    new_f = []
    new_b = []
    for l in range(DEPTH):
        lp = {
            'norm_g': norm_g[l], 'ffn1_w13': ffn1_w13[l], 'ffn1_w2': ffn1_w2[l],
            'ffn2_w13': ffn2_w13[l], 'ffn2_w2': ffn2_w2[l], 'w_in': w_in[l],
            'pool_w': pool_w[l], 'pool_scale': pool_scale[l], 'conv_w': conv_w[l], 'conv_b': conv_b[l],
            'a_log': a_log[l], 'dt_bias': dt_bias[l], 'd_skip': d_skip[l], 'ssd_norm': ssd_norm[l],
            'w_branch_pool': w_branch_pool[l], 'w_branch_ssd': w_branch_ssd[l], 'w_out': w_out[l],
        }
        mod_ctx = _adaln(c_ctx[None, :], ada_w[l], ada_b[l])
        mod_lat = _adaln(c, ada_w[l], ada_b[l])
        xc, hf, hb = _layer(xc, mod_ctx, None, zeros, zeros, lp)
        new_f.append(hf.astype(x_prompt.dtype))
        new_b.append(hb.astype(x_prompt.dtype))
        xl, _, _ = _layer(xl, mod_lat, rows, state_ssd_fwd[:, l], state_ssd_bwd[:, l], lp)
    y_prompt = _rmsnorm(xc, final_norm)
    y_sample = _rmsnorm(xl, final_norm)
    new_state_fwd = jnp.stack(new_f, axis=1)
    new_state_bwd = jnp.stack(new_b, axis=1)
    return (y_prompt, y_sample, new_state_fwd, new_state_bwd)
```

```python
import functools

import jax
import jax.numpy as jnp
from jax import lax
from jax.experimental import pallas as pl
from jax.experimental.pallas import tpu as pltpu

F32 = jnp.float32
BF16 = jnp.bfloat16

D_MODEL = 1024
DEPTH = 2
GRID_W = 64
POOL_WINDOWS = (2, 4, 8, 16)
POOL_GROUPS = len(POOL_WINDOWS)
POOL_DIM = D_MODEL
POOL_GROUP_DIM = POOL_DIM // POOL_GROUPS
D_INNER = 2 * D_MODEL
SSD_HEAD_DIM = 64
SSD_HEADS = D_INNER // SSD_HEAD_DIM
SSD_GROUPS = 4
HEADS_PER_GROUP = SSD_HEADS // SSD_GROUPS
SSD_STATE = 128
SSD_CONV = 4
SSD_CHUNK = 128
BC_DIM = SSD_GROUPS * SSD_STATE
CONV_DIM = D_INNER + 2 * BC_DIM
D_FF = 2816
N_MOD = 9
EPS = 1e-6

LANES = 128
SUBLANES = 8
VMEM_LIMIT = 56 * 1024 * 1024

TM_FFN = 512
FF_CHUNK = 256
TM_IN = 256
TM_OUT = 256
TB_CONV = 256
POOL_BLOCK = 256
POOL_PAD_ROWS = 8
ADA_TN = 1152


def _dot(a, b):
    return jnp.dot(a, b, preferred_element_type=F32)


def _dot_nt(a, b):
    return lax.dot_general(a, b, (((1,), (1,)), ((), ())), preferred_element_type=F32)


def _rmsnorm(x, g):
    return x * lax.rsqrt(jnp.mean(x * x, axis=-1, keepdims=True) + EPS) * g


def _params(sem):
    return pltpu.CompilerParams(dimension_semantics=sem, vmem_limit_bytes=VMEM_LIMIT)


def _resident(shape):
    nd = len(shape)
    return pl.BlockSpec(shape, lambda *_: (0,) * nd, pipeline_mode=pl.Buffered(1))


def _ada_kernel(cond_ref, w_ref, b_ref, o_ref):
    s = jax.nn.silu(cond_ref[...]).astype(BF16)
    o_ref[...] = _dot(s, w_ref[...].astype(BF16)) + b_ref[...]


def _adaln(cond, ada_w, ada_b):
    n = cond.shape[0]
    ncol = N_MOD * D_MODEL
    return pl.pallas_call(
        _ada_kernel,
        grid=(DEPTH, ncol // ADA_TN),
        in_specs=[
            pl.BlockSpec((n, D_MODEL), lambda l, j: (0, 0)),
            pl.BlockSpec((None, D_MODEL, ADA_TN), lambda l, j: (l, 0, j)),
            pl.BlockSpec((None, 1, ADA_TN), lambda l, j: (l, 0, j)),
        ],
        out_specs=pl.BlockSpec((None, n, ADA_TN), lambda l, j: (l, 0, j)),
        out_shape=jax.ShapeDtypeStruct((DEPTH, n, ncol), F32),
        compiler_params=_params(("parallel", "parallel")),
        name="adaln",
    )(cond, ada_w, ada_b.reshape(DEPTH, 1, ncol))


def _ffn_kernel(x_ref, mod_ref, g_ref, w13_ref, w2_ref, *rest, final):
    if final:
        fg_ref, o_ref, a_scr = rest
    else:
        o_ref, a_scr = rest
    x = x_ref[...]
    m = mod_ref[...]
    h = (_rmsnorm(x, g_ref[...]) * (1.0 + m[1:2]) + m[0:1]).astype(BF16)
    for j in range(D_FF // FF_CHUNK):
        lo, hi = j * FF_CHUNK, (j + 1) * FF_CHUNK
        gate = _dot(h, w13_ref[:, lo:hi])
        up = _dot(h, w13_ref[:, D_FF + lo:D_FF + hi])
        a_scr[:, lo:hi] = (jax.nn.silu(gate) * up).astype(BF16)
    y = _dot(a_scr[...], w2_ref[...])
    out = x + 0.5 * m[2:3] * y
    if final:
        out = _rmsnorm(out, fg_ref[...])
    o_ref[...] = out


def _ffn(x, mod, g, w13, w2, final_g=None):
    b, t, _ = x.shape
    nt = b * t
    tpm = nt // mod.shape[0]
    final = final_g is not None
    in_specs = [
        pl.BlockSpec((TM_FFN, D_MODEL), lambda i: (i, 0)),
        pl.BlockSpec((None, 3, D_MODEL), lambda i: (i * TM_FFN // tpm, 0, 0)),
        _resident((1, D_MODEL)),
        _resident((D_MODEL, 2 * D_FF)),
        _resident((D_FF, D_MODEL)),
    ]
    args = [x.reshape(nt, D_MODEL), mod, g, w13, w2]
    if final:
        in_specs.append(_resident((1, D_MODEL)))
        args.append(final_g)
    out = pl.pallas_call(
        functools.partial(_ffn_kernel, final=final),
        grid=(nt // TM_FFN,),
        in_specs=in_specs,
        out_specs=pl.BlockSpec((TM_FFN, D_MODEL), lambda i: (i, 0)),
        out_shape=jax.ShapeDtypeStruct((nt, D_MODEL), F32),
        scratch_shapes=[pltpu.VMEM((TM_FFN, D_FF), BF16)],
        compiler_params=_params(("parallel",)),
        name="ffn_final" if final else "ffn",
    )(*args)
    return out.reshape(b, t, D_MODEL)


def _in_kernel(x_ref, mod_ref, g_ref, wp_ref, wz_ref, wx_ref, wd_ref, wg_ref, dtb_ref,
               up_ref, z_ref, xbc_ref, dt_ref, gt_ref):
    m = mod_ref[...]
    h = (_rmsnorm(x_ref[...], g_ref[...]) * (1.0 + m[1:2]) + m[0:1]).astype(BF16)
    up_ref[...] = _dot(h, wp_ref[...])
    z_ref[...] = _dot(h, wz_ref[...])
    xbc_ref[...] = _dot(h, wx_ref[...])
    gt_ref[...] = _dot(h, wg_ref[...])
    dt_ref[...] = jax.nn.softplus(_dot(h, wd_ref[...]) + dtb_ref[...])


def _in_proj(x, mod, g, lw):
    b, t, _ = x.shape
    nt = b * t
    tpm = nt // mod.shape[0]
    widths = (POOL_DIM, D_INNER, CONV_DIM, LANES, 2 * D_MODEL)
    tok = lambda n: pl.BlockSpec((TM_IN, n), lambda i: (i, 0))
    outs = pl.pallas_call(
        _in_kernel,
        grid=(nt // TM_IN,),
        in_specs=[
            tok(D_MODEL),
            pl.BlockSpec((None, 3, D_MODEL), lambda i: (i * TM_IN // tpm, 0, 0)),
            _resident((1, D_MODEL)),
            _resident((D_MODEL, POOL_DIM)),
            _resident((D_MODEL, D_INNER)),
            _resident((D_MODEL, CONV_DIM)),
            _resident((D_MODEL, LANES)),
            _resident((D_MODEL, 2 * D_MODEL)),
            _resident((1, LANES)),
        ],
        out_specs=[tok(n) for n in widths],
        out_shape=[jax.ShapeDtypeStruct((nt, n), F32) for n in widths],
        compiler_params=_params(("parallel",)),
        name="in_proj",
    )(x.reshape(nt, D_MODEL), mod, g, lw["w_pool"], lw["w_z"], lw["w_xbc"], lw["w_dt"], lw["w_gates"],
      lw["dt_bias"])
    return [o.reshape(b, t, n) for o, n in zip(outs, widths)]


def _window_count(idx, w, n):
    lo = jnp.maximum(idx - w // 2, 0)
    hi = jnp.minimum(idx - w // 2 + w, n)
    return (hi - lo).astype(F32)


def _band_sum(band, u):
    hi = u.astype(BF16)
    lo = (u - hi.astype(F32)).astype(BF16)
    return _dot(band, hi) + _dot(band, lo)


def _pool_finish(tot, cnt, u, pw_ref, ps_ref):
    d = (tot / cnt - u).astype(BF16)
    return _dot(d, pw_ref[...]) * ps_ref[...]


def _pool_seq_body(u_ref, pw_ref, ps_ref, o_ref, *, w, t):
    ti = lax.broadcasted_iota(jnp.int32, (t, t), 0)
    si = lax.broadcasted_iota(jnp.int32, (t, t), 1)
    band = jnp.where((si >= ti - w // 2) & (si < ti - w // 2 + w), 1.0, 0.0).astype(BF16)
    u = u_ref[...]
    cnt = _window_count(lax.broadcasted_iota(jnp.int32, u.shape, 0), w, t)
    o_ref[...] = _pool_finish(_band_sum(band, u), cnt, u, pw_ref, ps_ref)


def _pool_grid_body(u_ref, pw_ref, ps_ref, o_ref, cp_ref, *, w, rows):
    nblk = rows * GRID_W // POOL_BLOCK
    pad = POOL_PAD_ROWS * GRID_W
    ti = lax.broadcasted_iota(jnp.int32, (POOL_BLOCK, POOL_BLOCK), 0)
    si = lax.broadcasted_iota(jnp.int32, (POOL_BLOCK, POOL_BLOCK), 1)
    tc = ti & (GRID_W - 1)
    sc = si & (GRID_W - 1)
    same_row = (ti >> 6) == (si >> 6)
    band = jnp.where(same_row & (sc >= tc - w // 2) & (sc < tc - w // 2 + w), 1.0, 0.0).astype(BF16)
    zeros = jnp.zeros((pad, POOL_GROUP_DIM), F32)
    cp_ref[0:pad, :] = zeros
    cp_ref[pad + rows * GRID_W:2 * pad + rows * GRID_W, :] = zeros

    def col_pool(i, carry):
        off = pl.multiple_of(i * POOL_BLOCK, POOL_BLOCK)
        cp_ref[pl.ds(pad + off, POOL_BLOCK), :] = _band_sum(band, u_ref[pl.ds(off, POOL_BLOCK), :])
        return carry

    lax.fori_loop(0, nblk, col_pool, 0)

    tt = lax.broadcasted_iota(jnp.int32, (POOL_BLOCK, POOL_GROUP_DIM), 0)
    ccnt = _window_count(tt & (GRID_W - 1), w, GRID_W)

    def row_pool(i, carry):
        off = pl.multiple_of(i * POOL_BLOCK, POOL_BLOCK)
        tot = cp_ref[pl.ds(pad + off - (w // 2) * GRID_W, POOL_BLOCK), :]
        for k in range(1, w):
            tot = tot + cp_ref[pl.ds(pad + off + (k - w // 2) * GRID_W, POOL_BLOCK), :]
        r = i * (POOL_BLOCK // GRID_W) + (tt >> 6)
        cnt = _window_count(r, w, rows) * ccnt
        o_ref[pl.ds(off, POOL_BLOCK), :] = _pool_finish(tot, cnt, u_ref[pl.ds(off, POOL_BLOCK), :], pw_ref, ps_ref)
        return carry

    lax.fori_loop(0, nblk, row_pool, 0)


def _pool_kernel(u_ref, pw_ref, ps_ref, o_ref, *scratch, rows, t):
    g = pl.program_id(1)
    for gi, w in enumerate(POOL_WINDOWS):
        @pl.when(g == gi)
        def _(w=w):
            if rows is None:
                _pool_seq_body(u_ref, pw_ref, ps_ref, o_ref, w=w, t=t)
            else:
                _pool_grid_body(u_ref, pw_ref, ps_ref, o_ref, scratch[0], w=w, rows=rows)


def _pool_mixer(u, pool_w, pool_scale, rows):
    b, t, _ = u.shape
    scratch = []
    if rows is not None:
        scratch = [pltpu.VMEM(((rows + 2 * POOL_PAD_ROWS) * GRID_W, POOL_GROUP_DIM), F32)]
    blk = pl.BlockSpec((None, t, POOL_GROUP_DIM), lambda i, g: (i, 0, g))
    return pl.pallas_call(
        functools.partial(_pool_kernel, rows=rows, t=t),
        grid=(b, POOL_GROUPS),
        in_specs=[
            blk,
            pl.BlockSpec((None, POOL_GROUP_DIM, POOL_GROUP_DIM), lambda i, g: (g, 0, 0)),
            pl.BlockSpec((1, POOL_GROUP_DIM), lambda i, g: (0, g)),
        ],
        out_specs=blk,
        out_shape=jax.ShapeDtypeStruct((b, t, POOL_DIM), F32),
        scratch_shapes=scratch,
        compiler_params=_params(("parallel", "parallel")),
        name="pool_seq" if rows is None else "pool_grid",
    )(u, pool_w, pool_scale)


def _conv_kernel(x_ref, prev_ref, next_ref, w_ref, b_ref, xt_ref, bc_ref, cc_ref, ext_ref, *, nblk):
    j = pl.program_id(1)
    left = SSD_CONV // 2
    ext_ref[0:SUBLANES, :] = jnp.where(j > 0, prev_ref[...], 0.0)
    ext_ref[SUBLANES:SUBLANES + TB_CONV, :] = x_ref[...]
    ext_ref[SUBLANES + TB_CONV:2 * SUBLANES + TB_CONV, :] = jnp.where(j < nblk - 1, next_ref[...], 0.0)
    ct = BC_DIM
    for c in range(CONV_DIM // ct):
        cols = slice(c * ct, (c + 1) * ct)
        acc = b_ref[:, cols] + ext_ref[SUBLANES - left:SUBLANES - left + TB_CONV, cols] * w_ref[0:1, cols]
        for k in range(1, SSD_CONV):
            s = SUBLANES - left + k
            acc = acc + ext_ref[s:s + TB_CONV, cols] * w_ref[k:k + 1, cols]
        y = jax.nn.silu(acc)
        if c < D_INNER // ct:
            for ci in range(TB_CONV // SSD_CHUNK):
                xt_ref[ci, cols, :] = y[ci * SSD_CHUNK:(ci + 1) * SSD_CHUNK, :].T.astype(BF16)
        elif c == D_INNER // ct:
            bc_ref[...] = y.astype(BF16)
        else:
            cc_ref[...] = y.astype(BF16)


def _conv(xbc, conv_w, conv_b):
    b, t, _ = xbc.shape
    nblk = t // TB_CONV
    nc = t // SSD_CHUNK
    per = TB_CONV // SUBLANES
    return pl.pallas_call(
        functools.partial(_conv_kernel, nblk=nblk),
        grid=(b, nblk),
        in_specs=[
            pl.BlockSpec((None, TB_CONV, CONV_DIM), lambda i, j: (i, j, 0)),
            pl.BlockSpec((None, SUBLANES, CONV_DIM), lambda i, j: (i, jnp.maximum(j * per - 1, 0), 0)),
            pl.BlockSpec((None, SUBLANES, CONV_DIM), lambda i, j: (i, jnp.minimum((j + 1) * per, t // SUBLANES - 1), 0)),
            pl.BlockSpec((SSD_CONV, CONV_DIM), lambda i, j: (0, 0)),
            pl.BlockSpec((1, CONV_DIM), lambda i, j: (0, 0)),
        ],
        out_specs=[
            pl.BlockSpec((None, TB_CONV // SSD_CHUNK, D_INNER, SSD_CHUNK), lambda i, j: (i, j, 0, 0)),
            pl.BlockSpec((None, TB_CONV, BC_DIM), lambda i, j: (i, j, 0)),
            pl.BlockSpec((None, TB_CONV, BC_DIM), lambda i, j: (i, j, 0)),
        ],
        out_shape=[
            jax.ShapeDtypeStruct((b, nc, D_INNER, SSD_CHUNK), BF16),
            jax.ShapeDtypeStruct((b, t, BC_DIM), BF16),
            jax.ShapeDtypeStruct((b, t, BC_DIM), BF16),
        ],
        scratch_shapes=[pltpu.VMEM((TB_CONV + 2 * SUBLANES, CONV_DIM), F32)],
        compiler_params=_params(("parallel", "parallel")),
        name="conv",
    )(xbc, xbc, xbc, conv_w, conv_b)


def _ssd_chunk_terms(dt_ref, alog_ref):
    n = SSD_CHUNK
    dtc = dt_ref[...]
    da = dtc * (-jnp.exp(alog_ref[...]))
    row = lax.broadcasted_iota(jnp.int32, da.shape, 0)
    cs = da
    rcs = da
    k = 1
    while k < n:
        cs = cs + jnp.where(row >= k, pltpu.roll(cs, k, 0), 0.0)
        rcs = rcs + jnp.where(row < n - k, pltpu.roll(rcs, n - k, 0), 0.0)
        k *= 2
    logdt = jnp.log(dtc)
    return dict(colf=cs - logdt, colb=rcs - logdt, cst=cs.T, rcst=rcs.T, dtt=dtc.T)


def _rows64(mat, h):
    return jnp.broadcast_to(mat[h:h + 1, :], (SSD_HEAD_DIM, mat.shape[1]))


def _pair_rows(mat, h0):
    return jnp.concatenate([_rows64(mat, h0), _rows64(mat, h0 + 1)], axis=0)


def _state_update(h_scr, g, xw, bc_g, dec_t, lane_off):
    rows = slice(g * HEADS_PER_GROUP * SSD_HEAD_DIM, (g + 1) * HEADS_PER_GROUP * SSD_HEAD_DIM)
    decay = jnp.concatenate(
        [_rows64(dec_t, lane_off + g * HEADS_PER_GROUP + h) for h in range(HEADS_PER_GROUP)], axis=0)
    h_scr[rows, :] = h_scr[rows, :] * decay + _dot(xw, bc_g)


def _ssd_f_kernel(*refs, zero_init, emit_state, nc):
    xt_ref, bc_ref, cc_ref, dt_ref, alog_ref = refs[:5]
    refs = refs[5:]
    if not zero_init:
        h0_ref, refs = refs[0], refs[1:]
    y_ref, refs = refs[0], refs[1:]
    if emit_state:
        hf_ref, refs = refs[0], refs[1:]
    h_scr = refs[0]
    n = SSD_CHUNK
    c = pl.program_id(1)

    @pl.when(c == 0)
    def _():
        h_scr[...] = jnp.zeros_like(h_scr) if zero_init else h0_ref[...]

    tm = _ssd_chunk_terms(dt_ref, alog_ref)
    cst, rcst, dtt = tm["cst"], tm["rcst"], tm["dtt"]
    tot = cst[:, n - 1:n]
    eoff_t = jnp.exp(cst)
    w_t = dtt * jnp.exp(tot - cst)
    dec_t = jnp.exp(jnp.broadcast_to(tot, (LANES, SSD_STATE)))
    si = lax.broadcasted_iota(jnp.int32, (n, n), 0)
    li = lax.broadcasted_iota(jnp.int32, (n, n), 1)
    src_before = si <= li
    src_after = si >= li
    zeros64 = jnp.zeros((SSD_HEAD_DIM, n), BF16)
    gw = HEADS_PER_GROUP * SSD_HEAD_DIM
    for g in range(SSD_GROUPS):
        bc_g = bc_ref[:, g * SSD_STATE:(g + 1) * SSD_STATE]
        cc_g = cc_ref[:, g * SSD_STATE:(g + 1) * SSD_STATE]
        sc_t = _dot_nt(bc_g, cc_g)
        y_off = _dot_nt(h_scr[g * gw:(g + 1) * gw, :].astype(BF16), cc_g)
        xw = []
        for j in range(HEADS_PER_GROUP // 2):
            h0 = g * HEADS_PER_GROUP + 2 * j
            mts = []
            for h in (h0, h0 + 1):
                seg_f = cst[h:h + 1, :] - tm["colf"][:, h:h + 1]
                hb = SSD_HEADS + h
                seg_b = rcst[hb:hb + 1, :] - tm["colb"][:, hb:hb + 1]
                dec = (jnp.exp(jnp.where(src_before, seg_f, -jnp.inf))
                       + jnp.exp(jnp.where(src_after, seg_b, -jnp.inf)))
                mts.append((sc_t * dec).astype(BF16))
            r0 = h0 * SSD_HEAD_DIM
            xp = xt_ref[r0:r0 + 2 * SSD_HEAD_DIM, :]
            lhs = jnp.concatenate([jnp.concatenate([xp[:SSD_HEAD_DIM], zeros64], axis=0),
                                   jnp.concatenate([zeros64, xp[SSD_HEAD_DIM:]], axis=0)], axis=1)
            y_diag = _dot(lhs, jnp.concatenate(mts, axis=0))
            yo = y_off[2 * j * SSD_HEAD_DIM:(2 * j + 2) * SSD_HEAD_DIM, :]
            y_ref[r0:r0 + 2 * SSD_HEAD_DIM, :] = y_diag + yo * _pair_rows(eoff_t, h0)
            xw.append((xp.astype(F32) * _pair_rows(w_t, h0)).astype(BF16))
        _state_update(h_scr, g, jnp.concatenate(xw, axis=0), bc_g, dec_t, 0)

    if emit_state:
        @pl.when(c == nc - 1)
        def _():
            hf_ref[...] = h_scr[...]


def _ssd_b_kernel(*refs, zero_init, emit_state, nc):
    xt_ref, bc_ref, cc_ref, dt_ref, alog_ref, dsk_ref, y1_ref = refs[:7]
    refs = refs[7:]
    if not zero_init:
        h0_ref, refs = refs[0], refs[1:]
    y_ref, refs = refs[0], refs[1:]
    if emit_state:
        hb_ref, refs = refs[0], refs[1:]
    h_scr = refs[0]
    c = pl.program_id(1)

    @pl.when(c == 0)
    def _():
        h_scr[...] = jnp.zeros_like(h_scr) if zero_init else h0_ref[...]

    tm = _ssd_chunk_terms(dt_ref, alog_ref)
    rcst, dtt = tm["rcst"], tm["dtt"]
    tot = rcst[:, 0:1]
    eoff_t = jnp.exp(rcst)
    w_t = dtt * jnp.exp(tot - rcst)
    dec_t = jnp.exp(jnp.broadcast_to(tot, (LANES, SSD_STATE)))
    dsk = dsk_ref[...]
    gw = HEADS_PER_GROUP * SSD_HEAD_DIM
    for g in range(SSD_GROUPS):
        bc_g = bc_ref[:, g * SSD_STATE:(g + 1) * SSD_STATE]
        cc_g = cc_ref[:, g * SSD_STATE:(g + 1) * SSD_STATE]
        y_off = _dot_nt(h_scr[g * gw:(g + 1) * gw, :].astype(BF16), cc_g)
        xw, yt = [], []
        for j in range(HEADS_PER_GROUP // 2):
            h0 = g * HEADS_PER_GROUP + 2 * j
            r0 = h0 * SSD_HEAD_DIM
            xp = xt_ref[r0:r0 + 2 * SSD_HEAD_DIM, :].astype(F32)
            yo = y_off[2 * j * SSD_HEAD_DIM:(2 * j + 2) * SSD_HEAD_DIM, :]
            yt.append(y1_ref[r0:r0 + 2 * SSD_HEAD_DIM, :] + yo * _pair_rows(eoff_t, SSD_HEADS + h0)
                      + _pair_rows(dsk, h0) * xp)
            xw.append((xp * _pair_rows(w_t, SSD_HEADS + h0)).astype(BF16))
        y_ref[:, g * gw:(g + 1) * gw] = jnp.concatenate(yt, axis=0).T
        _state_update(h_scr, g, jnp.concatenate(xw, axis=0), bc_g, dec_t, SSD_HEADS)

    if emit_state:
        @pl.when(c == nc - 1)
        def _():
            hb_ref[...] = h_scr[...]


def _ssd(xt, bcv, ccv, dt, alog, dsk, h0_f, h0_b, layer, emit_state):
    b, nc = xt.shape[0], xt.shape[1]
    t = nc * SSD_CHUNK
    zero_init = h0_f is None
    state_shape = jax.ShapeDtypeStruct((b, D_INNER, SSD_STATE), F32)

    def specs(cmap):
        return [
            pl.BlockSpec((None, None, D_INNER, SSD_CHUNK), lambda i, c: (i, cmap(c), 0, 0)),
            pl.BlockSpec((None, SSD_CHUNK, BC_DIM), lambda i, c: (i, cmap(c), 0)),
            pl.BlockSpec((None, SSD_CHUNK, BC_DIM), lambda i, c: (i, cmap(c), 0)),
            pl.BlockSpec((None, SSD_CHUNK, LANES), lambda i, c: (i, cmap(c), 0)),
            pl.BlockSpec((1, LANES), lambda i, c: (0, 0)),
        ]

    h0_spec = pl.BlockSpec((None, None, D_INNER, SSD_STATE), lambda i, c: (i, layer, 0, 0))
    state_spec = pl.BlockSpec((None, D_INNER, SSD_STATE), lambda i, c: (i, 0, 0))
    scratch = [pltpu.VMEM((D_INNER, SSD_STATE), F32)]
    cp = _params(("parallel", "arbitrary"))

    fwd = lambda c: c
    in_specs = specs(fwd)
    args = [xt, bcv, ccv, dt, alog]
    if not zero_init:
        in_specs.append(h0_spec)
        args.append(h0_f)
    y1_spec = pl.BlockSpec((None, None, D_INNER, SSD_CHUNK), lambda i, c: (i, c, 0, 0))
    out_specs = [y1_spec]
    out_shape = [jax.ShapeDtypeStruct((b, nc, D_INNER, SSD_CHUNK), F32)]
    if emit_state:
        out_specs.append(state_spec)
        out_shape.append(state_shape)
    res = pl.pallas_call(
        functools.partial(_ssd_f_kernel, zero_init=zero_init, emit_state=emit_state, nc=nc),
        grid=(b, nc), in_specs=in_specs, out_specs=out_specs, out_shape=out_shape,
        scratch_shapes=scratch, compiler_params=cp, name="ssd_fwd",
    )(*args)
    y1, hf = (res[0], res[1]) if emit_state else (res[0], None)

    bwd = lambda c: nc - 1 - c
    in_specs = specs(bwd) + [
        pl.BlockSpec((LANES, LANES), lambda i, c: (0, 0)),
        pl.BlockSpec((None, None, D_INNER, SSD_CHUNK), lambda i, c: (i, bwd(c), 0, 0)),
    ]
    args = [xt, bcv, ccv, dt, alog, dsk, y1]
    if not zero_init:
        in_specs.append(h0_spec)
        args.append(h0_b)
    out_specs = [pl.BlockSpec((None, SSD_CHUNK, D_INNER), lambda i, c: (i, bwd(c), 0))]
    out_shape = [jax.ShapeDtypeStruct((b, t, D_INNER), F32)]
    if emit_state:
        out_specs.append(state_spec)
        out_shape.append(state_shape)
    res = pl.pallas_call(
        functools.partial(_ssd_b_kernel, zero_init=zero_init, emit_state=emit_state, nc=nc),
        grid=(b, nc), in_specs=in_specs, out_specs=out_specs, out_shape=out_shape,
        scratch_shapes=scratch, compiler_params=cp, name="ssd_bwd",
    )(*args)
    y, hb = (res[0], res[1]) if emit_state else (res[0], None)
    return y, hf, hb


def _out_kernel(x_ref, y_ref, z_ref, gt_ref, yp_ref, mod_ref, ng_ref, wbp_ref, wbs_ref, wo_ref, o_ref):
    y = _rmsnorm(y_ref[...] * jax.nn.silu(z_ref[...]), ng_ref[...])
    gates = jax.nn.sigmoid(gt_ref[...])
    merged = (gates[:, :D_MODEL] * _dot(yp_ref[...].astype(BF16), wbp_ref[...])
              + gates[:, D_MODEL:] * _dot(y.astype(BF16), wbs_ref[...]))
    o_ref[...] = x_ref[...] + mod_ref[2:3, :] * _dot(merged.astype(BF16), wo_ref[...])


def _out_proj(x, y, z, gates, y_pool, mod, lw):
    b, t, _ = x.shape
    nt = b * t
    tpm = nt // mod.shape[0]
    tok = lambda n: pl.BlockSpec((TM_OUT, n), lambda i: (i, 0))
    flat = lambda a: a.reshape(nt, a.shape[-1])
    out = pl.pallas_call(
        _out_kernel,
        grid=(nt // TM_OUT,),
        in_specs=[
            tok(D_MODEL), tok(D_INNER), tok(D_INNER), tok(2 * D_MODEL), tok(POOL_DIM),
            pl.BlockSpec((None, 3, D_MODEL), lambda i: (i * TM_OUT // tpm, 0, 0)),
            _resident((1, D_INNER)),
            _resident((POOL_DIM, D_MODEL)),
            _resident((D_INNER, D_MODEL)),
            _resident((D_MODEL, D_MODEL)),
        ],
        out_specs=tok(D_MODEL),
        out_shape=jax.ShapeDtypeStruct((nt, D_MODEL), F32),
        compiler_params=_params(("parallel",)),
        name="out_proj",
    )(flat(x), flat(y), flat(z), flat(gates), flat(y_pool), mod, lw["ssd_norm"], lw["w_branch_pool"],
      lw["w_branch_ssd"], lw["w_out"])
    return out.reshape(b, t, D_MODEL)


def _layer(x, mod, rows, h0_f, h0_b, layer, lw, emit_state, final_g):
    ng = lw["norm_g"]
    x = _ffn(x, mod[:, 0:3], ng[0:1], lw["ffn1_w13"], lw["ffn1_w2"])
    mod2 = mod[:, 3:6]
    u_pool, z, xbc, dt, gates = _in_proj(x, mod2, ng[1:2], lw)
    y_pool = _pool_mixer(u_pool, lw["pool_w"], lw["pool_scale"], rows)
    xt, bcv, ccv = _conv(xbc, lw["conv_w"], lw["conv_b"])
    y, hf, hb = _ssd(xt, bcv, ccv, dt, lw["a_log"], lw["d_skip"], h0_f, h0_b, layer, emit_state)
    x = _out_proj(x, y, z, gates, y_pool, mod2, lw)
    x = _ffn(x, mod[:, 6:9], ng[2:3], lw["ffn2_w13"], lw["ffn2_w2"], final_g)
    return x, hf, hb


def _prep_layer(l, p):
    s1 = POOL_DIM
    s2 = s1 + D_INNER
    s3 = s2 + CONV_DIM
    s4 = s3 + 2 * SSD_HEADS
    w_in = p["w_in"][l]
    pad_lanes = lambda v: jnp.pad(v.reshape(1, -1), ((0, 0), (0, LANES - v.size)))
    return dict(
        norm_g=p["norm_g"][l],
        ffn1_w13=p["ffn1_w13"][l].astype(BF16), ffn1_w2=p["ffn1_w2"][l].astype(BF16),
        ffn2_w13=p["ffn2_w13"][l].astype(BF16), ffn2_w2=p["ffn2_w2"][l].astype(BF16),
        w_pool=w_in[:, :s1].astype(BF16), w_z=w_in[:, s1:s2].astype(BF16), w_xbc=w_in[:, s2:s3].astype(BF16),
        w_dt=jnp.pad(w_in[:, s3:s4], ((0, 0), (0, LANES - 2 * SSD_HEADS))).astype(BF16),
        w_gates=w_in[:, s4:].astype(BF16),
        dt_bias=pad_lanes(p["dt_bias"][l]),
        a_log=pad_lanes(p["a_log"][l]),
        d_skip=jnp.broadcast_to(pad_lanes(p["d_skip"][l]).reshape(LANES, 1), (LANES, LANES)),
        pool_w=p["pool_w"][l].astype(BF16), pool_scale=p["pool_scale"][l].reshape(1, POOL_DIM),
        conv_w=p["conv_w"][l], conv_b=p["conv_b"][l].reshape(1, CONV_DIM),
        ssd_norm=p["ssd_norm"][l].reshape(1, D_INNER),
        w_branch_pool=p["w_branch_pool"][l].astype(BF16), w_branch_ssd=p["w_branch_ssd"][l].astype(BF16),
        w_out=p["w_out"][l].astype(BF16),
    )


def kernel(x_prompt, x_sample, state_ssd_fwd, state_ssd_bwd, c, c_ctx, ada_w, ada_b, norm_g, ffn1_w13, ffn1_w2, ffn2_w13, ffn2_w2, w_in, pool_w, pool_scale, conv_w, conv_b, a_log, dt_bias, d_skip, ssd_norm, w_branch_pool, w_branch_ssd, w_out, final_norm):
    p = dict(norm_g=norm_g, ffn1_w13=ffn1_w13, ffn1_w2=ffn1_w2, ffn2_w13=ffn2_w13, ffn2_w2=ffn2_w2,
             w_in=w_in, pool_w=pool_w, pool_scale=pool_scale, conv_w=conv_w, conv_b=conv_b, a_log=a_log,
             dt_bias=dt_bias, d_skip=d_skip, ssd_norm=ssd_norm, w_branch_pool=w_branch_pool,
             w_branch_ssd=w_branch_ssd, w_out=w_out)
    nb = c.shape[0]
    rows = x_sample.shape[1] // GRID_W
    cond = jnp.concatenate([c_ctx[None, :], c], axis=0)
    cond = jnp.pad(cond, ((0, -(nb + 1) % SUBLANES), (0, 0)))
    mod = _adaln(cond, ada_w, ada_b).reshape(DEPTH, cond.shape[0], N_MOD, D_MODEL)
    fg = final_norm.reshape(1, D_MODEL)
    sd = state_ssd_fwd.shape
    h0_f = state_ssd_fwd.reshape(sd[0], DEPTH, D_INNER, SSD_STATE)
    h0_b = state_ssd_bwd.reshape(sd[0], DEPTH, D_INNER, SSD_STATE)
    xc, xl = x_prompt, x_sample
    new_f, new_b = [], []
    for l in range(DEPTH):
        lw = _prep_layer(l, p)
        last = fg if l == DEPTH - 1 else None
        xc, hf, hb = _layer(xc, mod[l, 0:1], None, None, None, l, lw, True, last)
        new_f.append(hf.reshape(-1, SSD_HEADS, SSD_HEAD_DIM, SSD_STATE))
        new_b.append(hb.reshape(-1, SSD_HEADS, SSD_HEAD_DIM, SSD_STATE))
        xl, _, _ = _layer(xl, mod[l, 1:1 + nb], rows, h0_f, h0_b, l, lw, False, last)
    return (xc, xl, jnp.stack(new_f, axis=1), jnp.stack(new_b, axis=1))
```

```python
import functools
import math

import jax
import jax.numpy as jnp
from jax import lax
from jax.experimental import pallas as pl
from jax.experimental.pallas import tpu as pltpu

F32 = jnp.float32
BF16 = jnp.bfloat16

D_MODEL = 1024
DEPTH = 2
GRID_W = 64
POOL_WINDOWS = (2, 4, 8, 16)
POOL_GROUPS = len(POOL_WINDOWS)
POOL_DIM = D_MODEL
POOL_GROUP_DIM = POOL_DIM // POOL_GROUPS
D_INNER = 2 * D_MODEL
SSD_HEAD_DIM = 64
SSD_HEADS = D_INNER // SSD_HEAD_DIM
SSD_GROUPS = 4
HEADS_PER_GROUP = SSD_HEADS // SSD_GROUPS
GROUP_DIM = HEADS_PER_GROUP * SSD_HEAD_DIM
SSD_STATE = 128
SSD_CONV = 4
SSD_CHUNK = 128
BC_DIM = SSD_GROUPS * SSD_STATE
CONV_DIM = D_INNER + 2 * BC_DIM
D_FF = 2816
N_MOD = 9
EPS = 1e-6
LOG2E = math.log2(math.e)

LANES = 128
SUBLANES = 8
VMEM_LIMIT = 56 * 1024 * 1024

TM_FFN = 512
FF_CHUNK = 256
TB_MIX = 256
CHUNKS_PER_TILE = TB_MIX // SSD_CHUNK
POOL_BLOCK = 256
POOL_PAD_ROWS = 8
ADA_TN = 1152


def _dot(a, b):
    return jnp.dot(a, b, preferred_element_type=F32)


def _dot_nt(a, b):
    return lax.dot_general(a, b, (((1,), (1,)), ((), ())), preferred_element_type=F32)


def _rmsnorm(x, g):
    return x * lax.rsqrt(jnp.mean(x * x, axis=-1, keepdims=True) + EPS) * g


def _params(sem):
    return pltpu.CompilerParams(dimension_semantics=sem, vmem_limit_bytes=VMEM_LIMIT)


def _resident(shape):
    nd = len(shape)
    return pl.BlockSpec(shape, lambda *_: (0,) * nd, pipeline_mode=pl.Buffered(1))


def _ada_kernel(cond_ref, w_ref, b_ref, o_ref):
    s = jax.nn.silu(cond_ref[...]).astype(BF16)
    o_ref[...] = _dot(s, w_ref[...].astype(BF16)) + b_ref[...]


def _adaln(cond, ada_w, ada_b):
    n = cond.shape[0]
    ncol = N_MOD * D_MODEL
    return pl.pallas_call(
        _ada_kernel,
        grid=(DEPTH, ncol // ADA_TN),
        in_specs=[
            pl.BlockSpec((n, D_MODEL), lambda l, j: (0, 0)),
            pl.BlockSpec((None, D_MODEL, ADA_TN), lambda l, j: (l, 0, j)),
            pl.BlockSpec((None, 1, ADA_TN), lambda l, j: (l, 0, j)),
        ],
        out_specs=pl.BlockSpec((None, n, ADA_TN), lambda l, j: (l, 0, j)),
        out_shape=jax.ShapeDtypeStruct((DEPTH, n, ncol), F32),
        compiler_params=_params(("parallel", "parallel")),
        name="adaln",
    )(cond, ada_w, ada_b.reshape(DEPTH, 1, ncol))


def _ffn_kernel(x_ref, mod_ref, g_ref, w13_ref, w2_ref, *rest, final):
    if final:
        fg_ref, o_ref, a_scr = rest
    else:
        o_ref, a_scr = rest
    x = x_ref[...]
    m = mod_ref[...]
    h = (_rmsnorm(x, g_ref[...]) * (1.0 + m[1:2]) + m[0:1]).astype(BF16)
    for j in range(D_FF // FF_CHUNK):
        lo, hi = j * FF_CHUNK, (j + 1) * FF_CHUNK
        gate = _dot(h, w13_ref[:, lo:hi])
        up = _dot(h, w13_ref[:, D_FF + lo:D_FF + hi])
        a_scr[:, lo:hi] = (jax.nn.silu(gate) * up).astype(BF16)
    y = _dot(a_scr[...], w2_ref[...])
    out = x + 0.5 * m[2:3] * y
    if final:
        out = _rmsnorm(out, fg_ref[...])
    o_ref[...] = out


def _ffn(x, mod, g, w13, w2, final_g=None):
    b, t, _ = x.shape
    nt = b * t
    tpm = nt // mod.shape[0]
    final = final_g is not None
    in_specs = [
        pl.BlockSpec((TM_FFN, D_MODEL), lambda i: (i, 0)),
        pl.BlockSpec((None, 3, D_MODEL), lambda i: (i * TM_FFN // tpm, 0, 0)),
        _resident((1, D_MODEL)),
        _resident((D_MODEL, 2 * D_FF)),
        _resident((D_FF, D_MODEL)),
    ]
    args = [x.reshape(nt, D_MODEL), mod, g, w13, w2]
    if final:
        in_specs.append(_resident((1, D_MODEL)))
        args.append(final_g)
    out = pl.pallas_call(
        functools.partial(_ffn_kernel, final=final),
        grid=(nt // TM_FFN,),
        in_specs=in_specs,
        out_specs=pl.BlockSpec((TM_FFN, D_MODEL), lambda i: (i, 0)),
        out_shape=jax.ShapeDtypeStruct((nt, D_MODEL), F32),
        scratch_shapes=[pltpu.VMEM((TM_FFN, D_FF), BF16)],
        compiler_params=_params(("parallel",)),
        name="ffn_final" if final else "ffn",
    )(*args)
    return out.reshape(b, t, D_MODEL)


def _window_count(idx, w, n):
    lo = jnp.maximum(idx - w // 2, 0)
    hi = jnp.minimum(idx - w // 2 + w, n)
    return (hi - lo).astype(F32)


def _band_sum(band, u):
    hi = u.astype(BF16)
    lo = (u - hi.astype(F32)).astype(BF16)
    return _dot(band, hi) + _dot(band, lo)


def _pool_finish(tot, cnt, u, pw_ref, ps_ref):
    d = (tot / cnt - u).astype(BF16)
    return _dot(d, pw_ref[...]) * ps_ref[...]


def _pool_seq_body(u_ref, pw_ref, ps_ref, o_ref, *, w, t):
    ti = lax.broadcasted_iota(jnp.int32, (t, t), 0)
    si = lax.broadcasted_iota(jnp.int32, (t, t), 1)
    band = jnp.where((si >= ti - w // 2) & (si < ti - w // 2 + w), 1.0, 0.0).astype(BF16)
    u = u_ref[...]
    cnt = _window_count(lax.broadcasted_iota(jnp.int32, u.shape, 0), w, t)
    o_ref[...] = _pool_finish(_band_sum(band, u), cnt, u, pw_ref, ps_ref)


def _pool_grid_body(u_ref, pw_ref, ps_ref, o_ref, cp_ref, *, w, rows):
    nblk = rows * GRID_W // POOL_BLOCK
    pad = POOL_PAD_ROWS * GRID_W
    ti = lax.broadcasted_iota(jnp.int32, (POOL_BLOCK, POOL_BLOCK), 0)
    si = lax.broadcasted_iota(jnp.int32, (POOL_BLOCK, POOL_BLOCK), 1)
    tc = ti & (GRID_W - 1)
    sc = si & (GRID_W - 1)
    same_row = (ti >> 6) == (si >> 6)
    band = jnp.where(same_row & (sc >= tc - w // 2) & (sc < tc - w // 2 + w), 1.0, 0.0).astype(BF16)
    zeros = jnp.zeros((pad, POOL_GROUP_DIM), F32)
    cp_ref[0:pad, :] = zeros
    cp_ref[pad + rows * GRID_W:2 * pad + rows * GRID_W, :] = zeros

    def col_pool(i, carry):
        off = pl.multiple_of(i * POOL_BLOCK, POOL_BLOCK)
        cp_ref[pl.ds(pad + off, POOL_BLOCK), :] = _band_sum(band, u_ref[pl.ds(off, POOL_BLOCK), :])
        return carry

    lax.fori_loop(0, nblk, col_pool, 0)

    tt = lax.broadcasted_iota(jnp.int32, (POOL_BLOCK, POOL_GROUP_DIM), 0)
    ccnt = _window_count(tt & (GRID_W - 1), w, GRID_W)

    def row_pool(i, carry):
        off = pl.multiple_of(i * POOL_BLOCK, POOL_BLOCK)
        tot = cp_ref[pl.ds(pad + off - (w // 2) * GRID_W, POOL_BLOCK), :]
        for k in range(1, w):
            tot = tot + cp_ref[pl.ds(pad + off + (k - w // 2) * GRID_W, POOL_BLOCK), :]
        r = i * (POOL_BLOCK // GRID_W) + (tt >> 6)
        cnt = _window_count(r, w, rows) * ccnt
        o_ref[pl.ds(off, POOL_BLOCK), :] = _pool_finish(tot, cnt, u_ref[pl.ds(off, POOL_BLOCK), :], pw_ref, ps_ref)
        return carry

    lax.fori_loop(0, nblk, row_pool, 0)


def _pool_kernel(u_ref, pw_ref, ps_ref, o_ref, *scratch, rows, t):
    g = pl.program_id(1)
    for gi, w in enumerate(POOL_WINDOWS):
        @pl.when(g == gi)
        def _(w=w):
            if rows is None:
                _pool_seq_body(u_ref, pw_ref, ps_ref, o_ref, w=w, t=t)
            else:
                _pool_grid_body(u_ref, pw_ref, ps_ref, o_ref, scratch[0], w=w, rows=rows)


def _pool_mixer(u, pool_w, pool_scale, rows):
    b, t, _ = u.shape
    scratch = []
    if rows is not None:
        scratch = [pltpu.VMEM(((rows + 2 * POOL_PAD_ROWS) * GRID_W, POOL_GROUP_DIM), F32)]
    blk = pl.BlockSpec((None, t, POOL_GROUP_DIM), lambda i, g: (i, 0, g))
    return pl.pallas_call(
        functools.partial(_pool_kernel, rows=rows, t=t),
        grid=(b, POOL_GROUPS),
        in_specs=[
            blk,
            pl.BlockSpec((None, POOL_GROUP_DIM, POOL_GROUP_DIM), lambda i, g: (g, 0, 0)),
            pl.BlockSpec((1, POOL_GROUP_DIM), lambda i, g: (0, g)),
        ],
        out_specs=blk,
        out_shape=jax.ShapeDtypeStruct((b, t, POOL_DIM), F32),
        scratch_shapes=scratch,
        compiler_params=_params(("parallel", "parallel")),
        name="pool_seq" if rows is None else "pool_grid",
    )(u, pool_w, pool_scale)


def _ssd_chunk_terms(dtc, alog):
    n = SSD_CHUNK
    da = dtc * (-jnp.exp(alog) * LOG2E)
    row = lax.broadcasted_iota(jnp.int32, da.shape, 0)
    cs = da
    rcs = da
    k = 1
    while k < n:
        cs = cs + jnp.where(row >= k, pltpu.roll(cs, k, 0), 0.0)
        rcs = rcs + jnp.where(row < n - k, pltpu.roll(rcs, n - k, 0), 0.0)
        k *= 2
    logdt = jnp.log(dtc) * LOG2E
    return dict(colf=cs - logdt, colb=rcs - logdt, cst=cs.T, rcst=rcs.T, dtt=dtc.T)


def _rows64(mat, h):
    return jnp.broadcast_to(mat[h:h + 1, :], (SSD_HEAD_DIM, mat.shape[1]))


def _pair_rows(mat, h0):
    return jnp.concatenate([_rows64(mat, h0), _rows64(mat, h0 + 1)], axis=0)


def _state_update(h_scr, g, xw, bc_g, dec_t, head_off):
    rows = slice(g * GROUP_DIM, (g + 1) * GROUP_DIM)
    decay = jnp.concatenate(
        [_rows64(dec_t, head_off + g * HEADS_PER_GROUP + h) for h in range(HEADS_PER_GROUP)], axis=0)
    h_scr[rows, :] = h_scr[rows, :] * decay + _dot(xw, bc_g)


def _ssd_fwd_chunk(ci, xt_ref, bc_ref, cc_ref, dt_ref, alog_ref, y1_ref, h_scr):
    n = SSD_CHUNK
    tok = slice(ci * n, (ci + 1) * n)
    tm = _ssd_chunk_terms(dt_ref[tok, :], alog_ref[...])
    cst, rcst, dtt = tm["cst"], tm["rcst"], tm["dtt"]
    tot = cst[:, n - 1:n]
    eoff_t = jnp.exp2(cst)
    w_t = dtt * jnp.exp2(tot - cst)
    dec_t = jnp.exp2(jnp.broadcast_to(tot, (LANES, SSD_STATE)))
    si = lax.broadcasted_iota(jnp.int32, (n, n), 0)
    li = lax.broadcasted_iota(jnp.int32, (n, n), 1)
    src_before = si <= li
    src_after = si >= li
    zeros64 = jnp.zeros((SSD_HEAD_DIM, n), BF16)
    for g in range(SSD_GROUPS):
        bc_g = bc_ref[tok, g * SSD_STATE:(g + 1) * SSD_STATE]
        cc_g = cc_ref[tok, g * SSD_STATE:(g + 1) * SSD_STATE]
        sc_t = _dot_nt(bc_g, cc_g)
        y_off = _dot_nt(h_scr[g * GROUP_DIM:(g + 1) * GROUP_DIM, :].astype(BF16), cc_g)
        xw = []
        for j in range(HEADS_PER_GROUP // 2):
            h0 = g * HEADS_PER_GROUP + 2 * j
            mts = []
            for h in (h0, h0 + 1):
                seg_f = cst[h:h + 1, :] - tm["colf"][:, h:h + 1]
                hb = SSD_HEADS + h
                seg_b = rcst[hb:hb + 1, :] - tm["colb"][:, hb:hb + 1]
                dec = (jnp.exp2(jnp.where(src_before, seg_f, -jnp.inf))
                       + jnp.exp2(jnp.where(src_after, seg_b, -jnp.inf)))
                mts.append((sc_t * dec).astype(BF16))
            r0 = h0 * SSD_HEAD_DIM
            xp = xt_ref[ci, r0:r0 + 2 * SSD_HEAD_DIM, :]
            lhs = jnp.concatenate([jnp.concatenate([xp[:SSD_HEAD_DIM], zeros64], axis=0),
                                   jnp.concatenate([zeros64, xp[SSD_HEAD_DIM:]], axis=0)], axis=1)
            y_diag = _dot(lhs, jnp.concatenate(mts, axis=0))
            yo = y_off[2 * j * SSD_HEAD_DIM:(2 * j + 2) * SSD_HEAD_DIM, :]
            y1_ref[ci, r0:r0 + 2 * SSD_HEAD_DIM, :] = (y_diag + yo * _pair_rows(eoff_t, h0)).astype(y1_ref.dtype)
            xw.append((xp.astype(F32) * _pair_rows(w_t, h0)).astype(BF16))
        _state_update(h_scr, g, jnp.concatenate(xw, axis=0), bc_g, dec_t, 0)


def _ssd_bwd_chunk(ci, xt_ref, bc_ref, cc_ref, dt_ref, alog_ref, dsk, y1_ref, y_scr, h_scr):
    n = SSD_CHUNK
    tok = slice(ci * n, (ci + 1) * n)
    tm = _ssd_chunk_terms(dt_ref[tok, :], alog_ref[...])
    rcst, dtt = tm["rcst"], tm["dtt"]
    tot = rcst[:, 0:1]
    eoff_t = jnp.exp2(rcst)
    w_t = dtt * jnp.exp2(tot - rcst)
    dec_t = jnp.exp2(jnp.broadcast_to(tot, (LANES, SSD_STATE)))
    for g in range(SSD_GROUPS):
        bc_g = bc_ref[tok, g * SSD_STATE:(g + 1) * SSD_STATE]
        cc_g = cc_ref[tok, g * SSD_STATE:(g + 1) * SSD_STATE]
        y_off = _dot_nt(h_scr[g * GROUP_DIM:(g + 1) * GROUP_DIM, :].astype(BF16), cc_g)
        xw, yt = [], []
        for j in range(HEADS_PER_GROUP // 2):
            h0 = g * HEADS_PER_GROUP + 2 * j
            r0 = h0 * SSD_HEAD_DIM
            xp = xt_ref[ci, r0:r0 + 2 * SSD_HEAD_DIM, :].astype(F32)
            yo = y_off[2 * j * SSD_HEAD_DIM:(2 * j + 2) * SSD_HEAD_DIM, :]
            yt.append(y1_ref[ci, r0:r0 + 2 * SSD_HEAD_DIM, :].astype(F32)
                      + yo * _pair_rows(eoff_t, SSD_HEADS + h0) + _pair_rows(dsk, h0) * xp)
            xw.append((xp * _pair_rows(w_t, SSD_HEADS + h0)).astype(BF16))
        y_scr[tok, g * GROUP_DIM:(g + 1) * GROUP_DIM] = jnp.concatenate(yt, axis=0).T
        _state_update(h_scr, g, jnp.concatenate(xw, axis=0), bc_g, dec_t, SSD_HEADS)


def _mixf_kernel(*refs, zero_init, emit_state, nblk):
    (x_ref, xp_ref, xn_ref, mod_ref, g_ref, wp_ref, wz_ref, wx_ref, wd_ref, wg_ref, dtb_ref,
     cw_ref, cb_ref, alog_ref) = refs[:14]
    refs = refs[14:]
    if not zero_init:
        h0_ref, refs = refs[0], refs[1:]
    up_ref, z_ref, gt_ref, dt_ref, xt_ref, bc_ref, cc_ref, y1_ref = refs[:8]
    refs = refs[8:]
    if emit_state:
        hf_ref, refs = refs[0], refs[1:]
    ext_ref, h_scr = refs
    j = pl.program_id(1)
    m = mod_ref[...]
    g = g_ref[...]

    def modulated(x):
        return (_rmsnorm(x, g) * (1.0 + m[1:2]) + m[0:1]).astype(BF16)

    h = modulated(x_ref[...])
    up_ref[...] = _dot(h, wp_ref[...])
    z_ref[...] = _dot(h, wz_ref[...]).astype(z_ref.dtype)
    gt_ref[...] = _dot(h, wg_ref[...]).astype(gt_ref.dtype)
    dt_ref[...] = jax.nn.softplus(_dot(h, wd_ref[...]) + dtb_ref[...])

    left = SSD_CONV // 2
    ext_ref[SUBLANES:SUBLANES + TB_MIX, :] = _dot(h, wx_ref[...])
    ext_ref[0:SUBLANES, :] = jnp.where(j > 0, _dot(modulated(xp_ref[...]), wx_ref[...]), 0.0)
    ext_ref[SUBLANES + TB_MIX:2 * SUBLANES + TB_MIX, :] = jnp.where(
        j < nblk - 1, _dot(modulated(xn_ref[...]), wx_ref[...]), 0.0)
    ct = BC_DIM
    for c in range(CONV_DIM // ct):
        cols = slice(c * ct, (c + 1) * ct)
        acc = cb_ref[:, cols] + ext_ref[SUBLANES - left:SUBLANES - left + TB_MIX, cols] * cw_ref[0:1, cols]
        for k in range(1, SSD_CONV):
            s = SUBLANES - left + k
            acc = acc + ext_ref[s:s + TB_MIX, cols] * cw_ref[k:k + 1, cols]
        y = jax.nn.silu(acc)
        if c < D_INNER // ct:
            for ci in range(CHUNKS_PER_TILE):
                xt_ref[ci, cols, :] = y[ci * SSD_CHUNK:(ci + 1) * SSD_CHUNK, :].T.astype(BF16)
        elif c == D_INNER // ct:
            bc_ref[...] = y.astype(BF16)
        else:
            cc_ref[...] = y.astype(BF16)

    @pl.when(j == 0)
    def _():
        h_scr[...] = jnp.zeros_like(h_scr) if zero_init else h0_ref[...]

    for ci in range(CHUNKS_PER_TILE):
        _ssd_fwd_chunk(ci, xt_ref, bc_ref, cc_ref, dt_ref, alog_ref, y1_ref, h_scr)

    if emit_state:
        @pl.when(j == nblk - 1)
        def _():
            hf_ref[...] = h_scr[...]


def _mix_fwd(x, mod, g, lw, h0_f, layer, emit_state):
    b, t, _ = x.shape
    nblk = t // TB_MIX
    nc = t // SSD_CHUNK
    per = TB_MIX // SUBLANES
    zero_init = h0_f is None
    mod_idx = (lambda i: i) if mod.shape[0] == b else (lambda i: 0)
    tok = lambda n: pl.BlockSpec((None, TB_MIX, n), lambda i, j: (i, j, 0))
    chunked = pl.BlockSpec((None, CHUNKS_PER_TILE, D_INNER, SSD_CHUNK), lambda i, j: (i, j, 0, 0))
    in_specs = [
        tok(D_MODEL),
        pl.BlockSpec((None, SUBLANES, D_MODEL), lambda i, j: (i, jnp.maximum(j * per - 1, 0), 0)),
        pl.BlockSpec((None, SUBLANES, D_MODEL), lambda i, j: (i, jnp.minimum((j + 1) * per, t // SUBLANES - 1), 0)),
        pl.BlockSpec((None, 3, D_MODEL), lambda i, j: (mod_idx(i), 0, 0)),
        _resident((1, D_MODEL)),
        _resident((D_MODEL, POOL_DIM)),
        _resident((D_MODEL, D_INNER)),
        _resident((D_MODEL, CONV_DIM)),
        _resident((D_MODEL, LANES)),
        _resident((D_MODEL, 2 * D_MODEL)),
        _resident((1, LANES)),
        _resident((SSD_CONV, CONV_DIM)),
        _resident((1, CONV_DIM)),
        _resident((1, LANES)),
    ]
    args = [x, x, x, mod, g, lw["w_pool"], lw["w_z"], lw["w_xbc"], lw["w_dt"], lw["w_gates"], lw["dt_bias"],
            lw["conv_w"], lw["conv_b"], lw["a_log"]]
    if not zero_init:
        in_specs.append(pl.BlockSpec((None, None, D_INNER, SSD_STATE), lambda i, j: (i, layer, 0, 0)))
        args.append(h0_f)
    out_specs = [tok(POOL_DIM), tok(D_INNER), tok(2 * D_MODEL), tok(LANES), chunked, tok(BC_DIM), tok(BC_DIM), chunked]
    out_shape = [
        jax.ShapeDtypeStruct((b, t, POOL_DIM), F32),
        jax.ShapeDtypeStruct((b, t, D_INNER), BF16),
        jax.ShapeDtypeStruct((b, t, 2 * D_MODEL), BF16),
        jax.ShapeDtypeStruct((b, t, LANES), F32),
        jax.ShapeDtypeStruct((b, nc, D_INNER, SSD_CHUNK), BF16),
        jax.ShapeDtypeStruct((b, t, BC_DIM), BF16),
        jax.ShapeDtypeStruct((b, t, BC_DIM), BF16),
        jax.ShapeDtypeStruct((b, nc, D_INNER, SSD_CHUNK), BF16),
    ]
    if emit_state:
        out_specs.append(pl.BlockSpec((None, D_INNER, SSD_STATE), lambda i, j: (i, 0, 0)))
        out_shape.append(jax.ShapeDtypeStruct((b, D_INNER, SSD_STATE), F32))
    return pl.pallas_call(
        functools.partial(_mixf_kernel, zero_init=zero_init, emit_state=emit_state, nblk=nblk),
        grid=(b, nblk), in_specs=in_specs, out_specs=out_specs, out_shape=out_shape,
        scratch_shapes=[pltpu.VMEM((TB_MIX + 2 * SUBLANES, CONV_DIM), F32), pltpu.VMEM((D_INNER, SSD_STATE), F32)],
        compiler_params=_params(("parallel", "arbitrary")),
        name="mix_fwd",
    )(*args)


def _mixb_kernel(*refs, zero_init, emit_state, nblk):
    (x_ref, xt_ref, bc_ref, cc_ref, dt_ref, y1_ref, z_ref, gt_ref, yp_ref, mod_ref, alog_ref, dsk_ref,
     ng_ref, wbp_ref, wbs_ref, wo_ref) = refs[:16]
    refs = refs[16:]
    if not zero_init:
        h0_ref, refs = refs[0], refs[1:]
    o_ref, refs = refs[0], refs[1:]
    if emit_state:
        hb_ref, refs = refs[0], refs[1:]
    y_scr, h_scr = refs
    j = pl.program_id(1)

    @pl.when(j == 0)
    def _():
        h_scr[...] = jnp.zeros_like(h_scr) if zero_init else h0_ref[...]

    dsk = dsk_ref[...]
    for ci in reversed(range(CHUNKS_PER_TILE)):
        _ssd_bwd_chunk(ci, xt_ref, bc_ref, cc_ref, dt_ref, alog_ref, dsk, y1_ref, y_scr, h_scr)

    if emit_state:
        @pl.when(j == nblk - 1)
        def _():
            hb_ref[...] = h_scr[...]

    y = _rmsnorm(y_scr[...] * jax.nn.silu(z_ref[...].astype(F32)), ng_ref[...])
    gates = jax.nn.sigmoid(gt_ref[...].astype(F32))
    merged = (gates[:, :D_MODEL] * _dot(yp_ref[...].astype(BF16), wbp_ref[...])
              + gates[:, D_MODEL:] * _dot(y.astype(BF16), wbs_ref[...]))
    o_ref[...] = x_ref[...] + mod_ref[2:3, :] * _dot(merged.astype(BF16), wo_ref[...])


def _mix_bwd(x, fwd, y_pool, mod, lw, h0_b, layer, emit_state):
    b, t, _ = x.shape
    nblk = t // TB_MIX
    zero_init = h0_b is None
    z, gates, dt, xt, bcv, ccv, y1 = fwd
    mod_idx = (lambda i: i) if mod.shape[0] == b else (lambda i: 0)
    tok = lambda n: pl.BlockSpec((None, TB_MIX, n), lambda i, j: (i, nblk - 1 - j, 0))
    chunked = pl.BlockSpec((None, CHUNKS_PER_TILE, D_INNER, SSD_CHUNK), lambda i, j: (i, nblk - 1 - j, 0, 0))
    in_specs = [
        tok(D_MODEL), chunked, tok(BC_DIM), tok(BC_DIM), tok(LANES), chunked, tok(D_INNER), tok(2 * D_MODEL),
        tok(POOL_DIM),
        pl.BlockSpec((None, 3, D_MODEL), lambda i, j: (mod_idx(i), 0, 0)),
        _resident((1, LANES)),
        _resident((LANES, LANES)),
        _resident((1, D_INNER)),
        _resident((POOL_DIM, D_MODEL)),
        _resident((D_INNER, D_MODEL)),
        _resident((D_MODEL, D_MODEL)),
    ]
    args = [x, xt, bcv, ccv, dt, y1, z, gates, y_pool, mod, lw["a_log"], lw["d_skip"], lw["ssd_norm"],
            lw["w_branch_pool"], lw["w_branch_ssd"], lw["w_out"]]
    if not zero_init:
        in_specs.append(pl.BlockSpec((None, None, D_INNER, SSD_STATE), lambda i, j: (i, layer, 0, 0)))
        args.append(h0_b)
    out_specs = [tok(D_MODEL)]
    out_shape = [jax.ShapeDtypeStruct((b, t, D_MODEL), F32)]
    if emit_state:
        out_specs.append(pl.BlockSpec((None, D_INNER, SSD_STATE), lambda i, j: (i, 0, 0)))
        out_shape.append(jax.ShapeDtypeStruct((b, D_INNER, SSD_STATE), F32))
    return pl.pallas_call(
        functools.partial(_mixb_kernel, zero_init=zero_init, emit_state=emit_state, nblk=nblk),
        grid=(b, nblk), in_specs=in_specs, out_specs=out_specs, out_shape=out_shape,
        scratch_shapes=[pltpu.VMEM((TB_MIX, D_INNER), F32), pltpu.VMEM((D_INNER, SSD_STATE), F32)],
        compiler_params=_params(("parallel", "arbitrary")),
        name="mix_bwd",
    )(*args)


def _layer(x, mod, rows, h0_f, h0_b, layer, lw, emit_state, final_g):
    ng = lw["norm_g"]
    x = _ffn(x, mod[:, 0:3], ng[0:1], lw["ffn1_w13"], lw["ffn1_w2"])
    mod2 = mod[:, 3:6]
    fwd = _mix_fwd(x, mod2, ng[1:2], lw, h0_f, layer, emit_state)
    hf = fwd[8] if emit_state else None
    y_pool = _pool_mixer(fwd[0], lw["pool_w"], lw["pool_scale"], rows)
    res = _mix_bwd(x, fwd[1:8], y_pool, mod2, lw, h0_b, layer, emit_state)
    x, hb = (res[0], res[1]) if emit_state else (res[0], None)
    x = _ffn(x, mod[:, 6:9], ng[2:3], lw["ffn2_w13"], lw["ffn2_w2"], final_g)
    return x, hf, hb


def _prep_layer(l, p):
    s1 = POOL_DIM
    s2 = s1 + D_INNER
    s3 = s2 + CONV_DIM
    s4 = s3 + 2 * SSD_HEADS
    w_in = p["w_in"][l]
    pad_lanes = lambda v: jnp.pad(v.reshape(1, -1), ((0, 0), (0, LANES - v.size)))
    return dict(
        norm_g=p["norm_g"][l],
        ffn1_w13=p["ffn1_w13"][l].astype(BF16), ffn1_w2=p["ffn1_w2"][l].astype(BF16),
        ffn2_w13=p["ffn2_w13"][l].astype(BF16), ffn2_w2=p["ffn2_w2"][l].astype(BF16),
        w_pool=w_in[:, :s1].astype(BF16), w_z=w_in[:, s1:s2].astype(BF16), w_xbc=w_in[:, s2:s3].astype(BF16),
        w_dt=jnp.pad(w_in[:, s3:s4], ((0, 0), (0, LANES - 2 * SSD_HEADS))).astype(BF16),
        w_gates=w_in[:, s4:].astype(BF16),
        dt_bias=pad_lanes(p["dt_bias"][l]),
        a_log=pad_lanes(p["a_log"][l]),
        d_skip=jnp.broadcast_to(pad_lanes(p["d_skip"][l]).reshape(LANES, 1), (LANES, LANES)),
        pool_w=p["pool_w"][l].astype(BF16), pool_scale=p["pool_scale"][l].reshape(1, POOL_DIM),
        conv_w=p["conv_w"][l], conv_b=p["conv_b"][l].reshape(1, CONV_DIM),
        ssd_norm=p["ssd_norm"][l].reshape(1, D_INNER),
        w_branch_pool=p["w_branch_pool"][l].astype(BF16), w_branch_ssd=p["w_branch_ssd"][l].astype(BF16),
        w_out=p["w_out"][l].astype(BF16),
    )


def kernel(x_prompt, x_sample, state_ssd_fwd, state_ssd_bwd, c, c_ctx, ada_w, ada_b, norm_g, ffn1_w13, ffn1_w2, ffn2_w13, ffn2_w2, w_in, pool_w, pool_scale, conv_w, conv_b, a_log, dt_bias, d_skip, ssd_norm, w_branch_pool, w_branch_ssd, w_out, final_norm):
    p = dict(norm_g=norm_g, ffn1_w13=ffn1_w13, ffn1_w2=ffn1_w2, ffn2_w13=ffn2_w13, ffn2_w2=ffn2_w2,
             w_in=w_in, pool_w=pool_w, pool_scale=pool_scale, conv_w=conv_w, conv_b=conv_b, a_log=a_log,
             dt_bias=dt_bias, d_skip=d_skip, ssd_norm=ssd_norm, w_branch_pool=w_branch_pool,
             w_branch_ssd=w_branch_ssd, w_out=w_out)
    nb = c.shape[0]
    rows = x_sample.shape[1] // GRID_W
    cond = jnp.concatenate([c_ctx[None, :], c], axis=0)
    cond = jnp.pad(cond, ((0, -(nb + 1) % SUBLANES), (0, 0)))
    mod = _adaln(cond, ada_w, ada_b).reshape(DEPTH, cond.shape[0], N_MOD, D_MODEL)
    fg = final_norm.reshape(1, D_MODEL)
    sd = state_ssd_fwd.shape
    h0_f = state_ssd_fwd.reshape(sd[0], DEPTH, D_INNER, SSD_STATE)
    h0_b = state_ssd_bwd.reshape(sd[0], DEPTH, D_INNER, SSD_STATE)
    xc, xl = x_prompt, x_sample
    new_f, new_b = [], []
    for l in range(DEPTH):
        lw = _prep_layer(l, p)
        last = fg if l == DEPTH - 1 else None
        xc, hf, hb = _layer(xc, mod[l, 0:1], None, None, None, l, lw, True, last)
        new_f.append(hf.reshape(-1, SSD_HEADS, SSD_HEAD_DIM, SSD_STATE))
        new_b.append(hb.reshape(-1, SSD_HEADS, SSD_HEAD_DIM, SSD_STATE))
        xl, _, _ = _layer(xl, mod[l, 1:1 + nb], rows, h0_f, h0_b, l, lw, False, last)
    return (xc, xl, jnp.stack(new_f, axis=1), jnp.stack(new_b, axis=1))
```

```python
import functools
import math

import jax
import jax.numpy as jnp
from jax import lax
from jax.experimental import pallas as pl
from jax.experimental.pallas import tpu as pltpu

F32 = jnp.float32
BF16 = jnp.bfloat16

D_MODEL = 1024
DEPTH = 2
GRID_W = 64
POOL_WINDOWS = (2, 4, 8, 16)
POOL_GROUPS = len(POOL_WINDOWS)
POOL_DIM = D_MODEL
POOL_GROUP_DIM = POOL_DIM // POOL_GROUPS
D_INNER = 2 * D_MODEL
SSD_HEAD_DIM = 64
SSD_HEADS = D_INNER // SSD_HEAD_DIM
SSD_GROUPS = 4
HEADS_PER_GROUP = SSD_HEADS // SSD_GROUPS
GROUP_DIM = HEADS_PER_GROUP * SSD_HEAD_DIM
SSD_STATE = 128
SSD_CONV = 4
SSD_CHUNK = 128
BC_DIM = SSD_GROUPS * SSD_STATE
CONV_DIM = D_INNER + 2 * BC_DIM
D_FF = 2816
N_MOD = 9
EPS = 1e-6
LOG2E = math.log2(math.e)

LANES = 128
SUBLANES = 8
VMEM_LIMIT = 56 * 1024 * 1024

TM_FFN = 512
FF_CHUNK = 256
TB_MIX = 256
CHUNKS_PER_TILE = TB_MIX // SSD_CHUNK
POOL_BLOCK = 256
POOL_PAD_ROWS = 8
ADA_TN = 1152
PROJ_COLS = 256
SCAN_SCHEDULE = (0,) + (1,) * 7 + (3,) * 9 + (2,) * 4


def _dot(a, b):
    return jnp.dot(a, b, preferred_element_type=F32)


def _dot_nt(a, b):
    return lax.dot_general(a, b, (((1,), (1,)), ((), ())), preferred_element_type=F32)


def _rmsnorm(x, g):
    return x * lax.rsqrt(jnp.mean(x * x, axis=-1, keepdims=True) + EPS) * g


def _params(sem):
    return pltpu.CompilerParams(dimension_semantics=sem, vmem_limit_bytes=VMEM_LIMIT)


def _resident(shape):
    nd = len(shape)
    return pl.BlockSpec(shape, lambda *_: (0,) * nd, pipeline_mode=pl.Buffered(1))


def _ada_kernel(cond_ref, w_ref, b_ref, o_ref):
    s = jax.nn.silu(cond_ref[...]).astype(BF16)
    o_ref[...] = _dot(s, w_ref[...].astype(BF16)) + b_ref[...]


def _adaln(cond, ada_w, ada_b):
    n = cond.shape[0]
    ncol = N_MOD * D_MODEL
    return pl.pallas_call(
        _ada_kernel,
        grid=(DEPTH, ncol // ADA_TN),
        in_specs=[
            pl.BlockSpec((n, D_MODEL), lambda l, j: (0, 0)),
            pl.BlockSpec((None, D_MODEL, ADA_TN), lambda l, j: (l, 0, j)),
            pl.BlockSpec((None, 1, ADA_TN), lambda l, j: (l, 0, j)),
        ],
        out_specs=pl.BlockSpec((None, n, ADA_TN), lambda l, j: (l, 0, j)),
        out_shape=jax.ShapeDtypeStruct((DEPTH, n, ncol), F32),
        compiler_params=_params(("parallel", "parallel")),
        name="adaln",
    )(cond, ada_w, ada_b.reshape(DEPTH, 1, ncol))


def _ffn_kernel(x_ref, mod_ref, g_ref, w13_ref, w2_ref, *rest, final):
    if final:
        fg_ref, o_ref, a_scr = rest
    else:
        o_ref, a_scr = rest
    x = x_ref[...]
    m = mod_ref[...]
    h = (_rmsnorm(x, g_ref[...]) * (1.0 + m[1:2]) + m[0:1]).astype(BF16)
    for j in range(D_FF // FF_CHUNK):
        lo, hi = j * FF_CHUNK, (j + 1) * FF_CHUNK
        gate = _dot(h, w13_ref[:, lo:hi])
        up = _dot(h, w13_ref[:, D_FF + lo:D_FF + hi])
        a_scr[:, lo:hi] = (jax.nn.silu(gate) * up).astype(BF16)
    y = _dot(a_scr[...], w2_ref[...])
    out = x + 0.5 * m[2:3] * y
    if final:
        out = _rmsnorm(out, fg_ref[...])
    o_ref[...] = out


def _ffn(x, mod, g, w13, w2, final_g=None):
    b, t, _ = x.shape
    nt = b * t
    tpm = nt // mod.shape[0]
    final = final_g is not None
    in_specs = [
        pl.BlockSpec((TM_FFN, D_MODEL), lambda i: (i, 0)),
        pl.BlockSpec((None, 3, D_MODEL), lambda i: (i * TM_FFN // tpm, 0, 0)),
        _resident((1, D_MODEL)),
        _resident((D_MODEL, 2 * D_FF)),
        _resident((D_FF, D_MODEL)),
    ]
    args = [x.reshape(nt, D_MODEL), mod, g, w13, w2]
    if final:
        in_specs.append(_resident((1, D_MODEL)))
        args.append(final_g)
    out = pl.pallas_call(
        functools.partial(_ffn_kernel, final=final),
        grid=(nt // TM_FFN,),
        in_specs=in_specs,
        out_specs=pl.BlockSpec((TM_FFN, D_MODEL), lambda i: (i, 0)),
        out_shape=jax.ShapeDtypeStruct((nt, D_MODEL), F32),
        scratch_shapes=[pltpu.VMEM((TM_FFN, D_FF), BF16)],
        compiler_params=_params(("parallel",)),
        name="ffn_final" if final else "ffn",
    )(*args)
    return out.reshape(b, t, D_MODEL)


def _window_count(idx, w, n):
    lo = jnp.maximum(idx - w // 2, 0)
    hi = jnp.minimum(idx - w // 2 + w, n)
    return (hi - lo).astype(F32)


def _band_sum(band, u):
    hi = u.astype(BF16)
    lo = (u - hi.astype(F32)).astype(BF16)
    return _dot(band, hi) + _dot(band, lo)


def _pool_finish(tot, cnt, u, pw_ref, ps_ref):
    d = (tot / cnt - u).astype(BF16)
    return _dot(d, pw_ref[...]) * ps_ref[...]


def _pool_seq_body(u_ref, pw_ref, ps_ref, o_ref, *, w, t):
    ti = lax.broadcasted_iota(jnp.int32, (t, t), 0)
    si = lax.broadcasted_iota(jnp.int32, (t, t), 1)
    band = jnp.where((si >= ti - w // 2) & (si < ti - w // 2 + w), 1.0, 0.0).astype(BF16)
    u = u_ref[...]
    cnt = _window_count(lax.broadcasted_iota(jnp.int32, u.shape, 0), w, t)
    o_ref[...] = _pool_finish(_band_sum(band, u), cnt, u, pw_ref, ps_ref)


def _pool_grid_body(u_ref, pw_ref, ps_ref, o_ref, cp_ref, *, w, rows):
    nblk = rows * GRID_W // POOL_BLOCK
    pad = POOL_PAD_ROWS * GRID_W
    ti = lax.broadcasted_iota(jnp.int32, (POOL_BLOCK, POOL_BLOCK), 0)
    si = lax.broadcasted_iota(jnp.int32, (POOL_BLOCK, POOL_BLOCK), 1)
    tc = ti & (GRID_W - 1)
    sc = si & (GRID_W - 1)
    same_row = (ti >> 6) == (si >> 6)
    band = jnp.where(same_row & (sc >= tc - w // 2) & (sc < tc - w // 2 + w), 1.0, 0.0).astype(BF16)
    zeros = jnp.zeros((pad, POOL_GROUP_DIM), F32)
    cp_ref[0:pad, :] = zeros
    cp_ref[pad + rows * GRID_W:2 * pad + rows * GRID_W, :] = zeros

    def col_pool(i, carry):
        off = pl.multiple_of(i * POOL_BLOCK, POOL_BLOCK)
        cp_ref[pl.ds(pad + off, POOL_BLOCK), :] = _band_sum(band, u_ref[pl.ds(off, POOL_BLOCK), :])
        return carry

    lax.fori_loop(0, nblk, col_pool, 0)

    tt = lax.broadcasted_iota(jnp.int32, (POOL_BLOCK, POOL_GROUP_DIM), 0)
    ccnt = _window_count(tt & (GRID_W - 1), w, GRID_W)

    def row_pool(i, carry):
        off = pl.multiple_of(i * POOL_BLOCK, POOL_BLOCK)
        tot = cp_ref[pl.ds(pad + off - (w // 2) * GRID_W, POOL_BLOCK), :]
        for k in range(1, w):
            tot = tot + cp_ref[pl.ds(pad + off + (k - w // 2) * GRID_W, POOL_BLOCK), :]
        r = i * (POOL_BLOCK // GRID_W) + (tt >> 6)
        cnt = _window_count(r, w, rows) * ccnt
        o_ref[pl.ds(off, POOL_BLOCK), :] = _pool_finish(tot, cnt, u_ref[pl.ds(off, POOL_BLOCK), :], pw_ref, ps_ref)
        return carry

    lax.fori_loop(0, nblk, row_pool, 0)


def _pool_kernel(u_ref, pw_ref, ps_ref, o_ref, *scratch, rows, t):
    g = pl.program_id(1)
    for gi, w in enumerate(POOL_WINDOWS):
        @pl.when(g == gi)
        def _(w=w):
            if rows is None:
                _pool_seq_body(u_ref, pw_ref, ps_ref, o_ref, w=w, t=t)
            else:
                _pool_grid_body(u_ref, pw_ref, ps_ref, o_ref, scratch[0], w=w, rows=rows)


def _pool_mixer(u, pool_w, pool_scale, rows):
    b, t, _ = u.shape
    scratch = []
    if rows is not None:
        scratch = [pltpu.VMEM(((rows + 2 * POOL_PAD_ROWS) * GRID_W, POOL_GROUP_DIM), F32)]
    blk = pl.BlockSpec((None, t, POOL_GROUP_DIM), lambda i, g: (i, 0, g))
    return pl.pallas_call(
        functools.partial(_pool_kernel, rows=rows, t=t),
        grid=(b, POOL_GROUPS),
        in_specs=[
            blk,
            pl.BlockSpec((None, POOL_GROUP_DIM, POOL_GROUP_DIM), lambda i, g: (g, 0, 0)),
            pl.BlockSpec((1, POOL_GROUP_DIM), lambda i, g: (0, g)),
        ],
        out_specs=blk,
        out_shape=jax.ShapeDtypeStruct((b, t, POOL_DIM), F32),
        scratch_shapes=scratch,
        compiler_params=_params(("parallel", "parallel")),
        name="pool_seq" if rows is None else "pool_grid",
    )(u, pool_w, pool_scale)


def _ssd_chunk_terms(dtc, alog):
    n = SSD_CHUNK
    da = dtc * (-jnp.exp(alog) * LOG2E)
    row = lax.broadcasted_iota(jnp.int32, da.shape, 0)
    cs = da
    rcs = da
    k = 1
    while k < n:
        cs = cs + jnp.where(row >= k, pltpu.roll(cs, k, 0), 0.0)
        rcs = rcs + jnp.where(row < n - k, pltpu.roll(rcs, n - k, 0), 0.0)
        k *= 2
    logdt = jnp.log(dtc) * LOG2E
    return dict(colf=cs - logdt, colb=rcs - logdt, cst=cs.T, rcst=rcs.T, dtt=dtc.T)


def _rows64(mat, h):
    return jnp.broadcast_to(mat[h:h + 1, :], (SSD_HEAD_DIM, mat.shape[1]))


def _pair_rows(mat, h0):
    return jnp.concatenate([_rows64(mat, h0), _rows64(mat, h0 + 1)], axis=0)


def _state_update(h_scr, g, xw, bc_g, dec_t, head_off):
    rows = slice(g * GROUP_DIM, (g + 1) * GROUP_DIM)
    decay = jnp.concatenate(
        [_rows64(dec_t, head_off + g * HEADS_PER_GROUP + h) for h in range(HEADS_PER_GROUP)], axis=0)
    h_scr[rows, :] = h_scr[rows, :] * decay + _dot(xw, bc_g)


def _ssd_fwd_chunk(ci, xt_ref, bc_ref, cc_ref, dt_ref, alog_ref, y1_ref, h_scr):
    n = SSD_CHUNK
    tok = slice(ci * n, (ci + 1) * n)
    tm = _ssd_chunk_terms(dt_ref[tok, :], alog_ref[...])
    cst, rcst, dtt = tm["cst"], tm["rcst"], tm["dtt"]
    tot = cst[:, n - 1:n]
    eoff_t = jnp.exp2(cst)
    w_t = dtt * jnp.exp2(tot - cst)
    dec_t = jnp.exp2(jnp.broadcast_to(tot, (LANES, SSD_STATE)))
    si = lax.broadcasted_iota(jnp.int32, (n, n), 0)
    li = lax.broadcasted_iota(jnp.int32, (n, n), 1)
    src_before = si <= li
    src_after = si >= li
    zeros64 = jnp.zeros((SSD_HEAD_DIM, n), BF16)
    yield
    for g in range(SSD_GROUPS):
        bc_g = bc_ref[tok, g * SSD_STATE:(g + 1) * SSD_STATE]
        cc_g = cc_ref[tok, g * SSD_STATE:(g + 1) * SSD_STATE]
        sc_t = _dot_nt(bc_g, cc_g)
        y_off = _dot_nt(h_scr[g * GROUP_DIM:(g + 1) * GROUP_DIM, :].astype(BF16), cc_g)
        xw = []
        for j in range(HEADS_PER_GROUP // 2):
            h0 = g * HEADS_PER_GROUP + 2 * j
            mts = []
            for h in (h0, h0 + 1):
                seg_f = cst[h:h + 1, :] - tm["colf"][:, h:h + 1]
                hb = SSD_HEADS + h
                seg_b = rcst[hb:hb + 1, :] - tm["colb"][:, hb:hb + 1]
                dec = (jnp.exp2(jnp.where(src_before, seg_f, -jnp.inf))
                       + jnp.exp2(jnp.where(src_after, seg_b, -jnp.inf)))
                mts.append((sc_t * dec).astype(BF16))
            r0 = h0 * SSD_HEAD_DIM
            xp = xt_ref[ci, r0:r0 + 2 * SSD_HEAD_DIM, :]
            lhs = jnp.concatenate([jnp.concatenate([xp[:SSD_HEAD_DIM], zeros64], axis=0),
                                   jnp.concatenate([zeros64, xp[SSD_HEAD_DIM:]], axis=0)], axis=1)
            y_diag = _dot(lhs, jnp.concatenate(mts, axis=0))
            yo = y_off[2 * j * SSD_HEAD_DIM:(2 * j + 2) * SSD_HEAD_DIM, :]
            y1_ref[ci, r0:r0 + 2 * SSD_HEAD_DIM, :] = (y_diag + yo * _pair_rows(eoff_t, h0)).astype(y1_ref.dtype)
            xw.append((xp.astype(F32) * _pair_rows(w_t, h0)).astype(BF16))
            yield
        _state_update(h_scr, g, jnp.concatenate(xw, axis=0), bc_g, dec_t, 0)
        yield


def _ssd_bwd_chunk(ci, xt_ref, bc_ref, cc_ref, dt_ref, alog_ref, dsk, y1_ref, y_scr, h_scr):
    n = SSD_CHUNK
    tok = slice(ci * n, (ci + 1) * n)
    tm = _ssd_chunk_terms(dt_ref[tok, :], alog_ref[...])
    rcst, dtt = tm["rcst"], tm["dtt"]
    tot = rcst[:, 0:1]
    eoff_t = jnp.exp2(rcst)
    w_t = dtt * jnp.exp2(tot - rcst)
    dec_t = jnp.exp2(jnp.broadcast_to(tot, (LANES, SSD_STATE)))
    for g in range(SSD_GROUPS):
        bc_g = bc_ref[tok, g * SSD_STATE:(g + 1) * SSD_STATE]
        cc_g = cc_ref[tok, g * SSD_STATE:(g + 1) * SSD_STATE]
        y_off = _dot_nt(h_scr[g * GROUP_DIM:(g + 1) * GROUP_DIM, :].astype(BF16), cc_g)
        xw, yt = [], []
        for j in range(HEADS_PER_GROUP // 2):
            h0 = g * HEADS_PER_GROUP + 2 * j
            r0 = h0 * SSD_HEAD_DIM
            xp = xt_ref[ci, r0:r0 + 2 * SSD_HEAD_DIM, :].astype(F32)
            yo = y_off[2 * j * SSD_HEAD_DIM:(2 * j + 2) * SSD_HEAD_DIM, :]
            yt.append(y1_ref[ci, r0:r0 + 2 * SSD_HEAD_DIM, :].astype(F32)
                      + yo * _pair_rows(eoff_t, SSD_HEADS + h0) + _pair_rows(dsk, h0) * xp)
            xw.append((xp * _pair_rows(w_t, SSD_HEADS + h0)).astype(BF16))
        y_scr[tok, g * GROUP_DIM:(g + 1) * GROUP_DIM] = jnp.concatenate(yt, axis=0).T
        _state_update(h_scr, g, jnp.concatenate(xw, axis=0), bc_g, dec_t, SSD_HEADS)


def _project_and_conv(j, nblk, x_ref, xp_ref, xn_ref, mod_ref, g_ref, wp_ref, wz_ref, wx_ref, wd_ref, wg_ref,
                      dtb_ref, cw_ref, cb_ref, up_ref, z_ref, gt_ref, dt_ref, xt_ref, bc_ref, cc_ref,
                      ext_ref, hm_scr):
    m = mod_ref[...]
    g = g_ref[...]

    def modulated(x):
        return (_rmsnorm(x, g) * (1.0 + m[1:2]) + m[0:1]).astype(BF16)

    hm_scr[...] = modulated(x_ref[...])
    left = SSD_CONV // 2
    prev = jnp.where(j > 0, _dot(modulated(xp_ref[...]), wx_ref[...]), 0.0)
    for k in range(left):
        ext_ref[k, left - k:left - k + SUBLANES, :] = prev
    nxt = jnp.where(j < nblk - 1, _dot(modulated(xn_ref[...]), wx_ref[...]), 0.0)
    for k in range(left + 1, SSD_CONV):
        ext_ref[k, SUBLANES + left - k + TB_MIX:2 * SUBLANES + left - k + TB_MIX, :] = nxt
    yield
    ct = BC_DIM
    n_ct = CONV_DIM // ct

    def project_xbc(c):
        cols = slice(c * ct, (c + 1) * ct)
        xbc = _dot(hm_scr[...], wx_ref[:, cols])
        for k in range(SSD_CONV):
            ext_ref[k, SUBLANES + left - k:SUBLANES + left - k + TB_MIX, cols] = xbc

    def conv(c):
        cols = slice(c * ct, (c + 1) * ct)
        acc = cb_ref[:, cols] + ext_ref[0, SUBLANES:SUBLANES + TB_MIX, cols] * cw_ref[0:1, cols]
        for k in range(1, SSD_CONV):
            acc = acc + ext_ref[k, SUBLANES:SUBLANES + TB_MIX, cols] * cw_ref[k:k + 1, cols]
        y = jax.nn.silu(acc)
        if c < D_INNER // ct:
            for ci in range(CHUNKS_PER_TILE):
                xt_ref[ci, cols, :] = y[ci * SSD_CHUNK:(ci + 1) * SSD_CHUNK, :].T.astype(BF16)
        elif c == D_INNER // ct:
            bc_ref[...] = y.astype(BF16)
        else:
            cc_ref[...] = y.astype(BF16)

    project_xbc(0)
    yield
    for c in range(n_ct):
        if c + 1 < n_ct:
            project_xbc(c + 1)
        if c < D_INNER // ct:
            zc = slice(c * ct, (c + 1) * ct)
            z_ref[:, zc] = _dot(hm_scr[...], wz_ref[:, zc]).astype(z_ref.dtype)
        conv(c)
        yield
    dt_ref[...] = jax.nn.softplus(_dot(hm_scr[...], wd_ref[...]) + dtb_ref[...])
    yield
    for c in range(2 * D_MODEL // PROJ_COLS):
        gc = slice(c * PROJ_COLS, (c + 1) * PROJ_COLS)
        gt_ref[:, gc] = _dot(hm_scr[...], wg_ref[:, gc]).astype(gt_ref.dtype)
        yield
    for c in range(POOL_DIM // PROJ_COLS):
        pc = slice(c * PROJ_COLS, (c + 1) * PROJ_COLS)
        up_ref[:, pc] = _dot(hm_scr[...], wp_ref[:, pc])
        yield


def _mixf_kernel(*refs, zero_init, emit_state, nblk, ntiles):
    (x_ref, xp_ref, xn_ref, mod_ref, g_ref, wp_ref, wz_ref, wx_ref, wd_ref, wg_ref, dtb_ref,
     cw_ref, cb_ref, alog_ref) = refs[:14]
    refs = refs[14:]
    if not zero_init:
        h0_ref, refs = refs[0], refs[1:]
    up_ref, z_ref, gt_ref, dt_ref, xt_ref, bc_ref, cc_ref, y1_ref = refs[:8]
    refs = refs[8:]
    if emit_state:
        hf_ref, refs = refs[0], refs[1:]
    ext_ref, hm_scr, xt_cur, bc_cur, cc_cur, dt_cur, h_scr = refs
    s = pl.program_id(0)
    ja = lax.rem(jnp.minimum(s, ntiles - 1), nblk)
    jb = lax.rem(jnp.maximum(s - 1, 0), nblk)

    @pl.when(s == 0)
    def _():
        xt_cur[...] = jnp.zeros_like(xt_cur)
        bc_cur[...] = jnp.zeros_like(bc_cur)
        cc_cur[...] = jnp.zeros_like(cc_cur)
        dt_cur[...] = jnp.ones_like(dt_cur)

    @pl.when(jb == 0)
    def _():
        h_scr[...] = jnp.zeros_like(h_scr) if zero_init else h0_ref[...]

    proj = _project_and_conv(ja, nblk, x_ref, xp_ref, xn_ref, mod_ref, g_ref, wp_ref, wz_ref, wx_ref, wd_ref,
                             wg_ref, dtb_ref, cw_ref, cb_ref, up_ref, z_ref, gt_ref, dt_ref, xt_ref, bc_ref,
                             cc_ref, ext_ref, hm_scr)
    scan = (None for ci in range(CHUNKS_PER_TILE)
            for _ in _ssd_fwd_chunk(ci, xt_cur, bc_cur, cc_cur, dt_cur, alog_ref, y1_ref, h_scr))
    for n_scan in SCAN_SCHEDULE:
        next(proj, None)
        for _ in range(n_scan):
            next(scan, None)
    for _ in proj:
        pass
    for _ in scan:
        pass

    if emit_state:
        @pl.when(jb == nblk - 1)
        def _():
            hf_ref[...] = h_scr[...]

    xt_cur[...] = xt_ref[...]
    bc_cur[...] = bc_ref[...]
    cc_cur[...] = cc_ref[...]
    dt_cur[...] = dt_ref[...]


def _mix_fwd(x, mod, g, lw, h0_f, layer, emit_state):
    b, t, _ = x.shape
    nblk = t // TB_MIX
    ntiles = b * nblk
    nc = t // SSD_CHUNK
    per = TB_MIX // SUBLANES
    zero_init = h0_f is None
    mod_idx = (lambda i: i) if mod.shape[0] == b else (lambda i: 0)
    ta = lambda s: jnp.minimum(s, ntiles - 1)
    tb = lambda s: jnp.maximum(s - 1, 0)
    tok = lambda n: pl.BlockSpec((None, TB_MIX, n), lambda s: (ta(s) // nblk, ta(s) % nblk, 0))
    chunked = lambda tile: pl.BlockSpec((None, CHUNKS_PER_TILE, D_INNER, SSD_CHUNK),
                                        lambda s: (tile(s) // nblk, tile(s) % nblk, 0, 0))
    in_specs = [
        tok(D_MODEL),
        pl.BlockSpec((None, SUBLANES, D_MODEL),
                     lambda s: (ta(s) // nblk, jnp.maximum((ta(s) % nblk) * per - 1, 0), 0)),
        pl.BlockSpec((None, SUBLANES, D_MODEL),
                     lambda s: (ta(s) // nblk, jnp.minimum((ta(s) % nblk + 1) * per, t // SUBLANES - 1), 0)),
        pl.BlockSpec((None, 3, D_MODEL), lambda s: (mod_idx(ta(s) // nblk), 0, 0)),
        _resident((1, D_MODEL)),
        _resident((D_MODEL, POOL_DIM)),
        _resident((D_MODEL, D_INNER)),
        _resident((D_MODEL, CONV_DIM)),
        _resident((D_MODEL, LANES)),
        _resident((D_MODEL, 2 * D_MODEL)),
        _resident((1, LANES)),
        _resident((SSD_CONV, CONV_DIM)),
        _resident((1, CONV_DIM)),
        _resident((1, LANES)),
    ]
    args = [x, x, x, mod, g, lw["w_pool"], lw["w_z"], lw["w_xbc"], lw["w_dt"], lw["w_gates"], lw["dt_bias"],
            lw["conv_w"], lw["conv_b"], lw["a_log"]]
    if not zero_init:
        in_specs.append(pl.BlockSpec((None, None, D_INNER, SSD_STATE), lambda s: (tb(s) // nblk, layer, 0, 0)))
        args.append(h0_f)
    out_specs = [tok(POOL_DIM), tok(D_INNER), tok(2 * D_MODEL), tok(LANES), chunked(ta), tok(BC_DIM), tok(BC_DIM),
                 chunked(tb)]
    out_shape = [
        jax.ShapeDtypeStruct((b, t, POOL_DIM), F32),
        jax.ShapeDtypeStruct((b, t, D_INNER), BF16),
        jax.ShapeDtypeStruct((b, t, 2 * D_MODEL), BF16),
        jax.ShapeDtypeStruct((b, t, LANES), F32),
        jax.ShapeDtypeStruct((b, nc, D_INNER, SSD_CHUNK), BF16),
        jax.ShapeDtypeStruct((b, t, BC_DIM), BF16),
        jax.ShapeDtypeStruct((b, t, BC_DIM), BF16),
        jax.ShapeDtypeStruct((b, nc, D_INNER, SSD_CHUNK), BF16),
    ]
    if emit_state:
        out_specs.append(pl.BlockSpec((None, D_INNER, SSD_STATE), lambda s: (tb(s) // nblk, 0, 0)))
        out_shape.append(jax.ShapeDtypeStruct((b, D_INNER, SSD_STATE), F32))
    scratch = [
        pltpu.VMEM((SSD_CONV, TB_MIX + 2 * SUBLANES, CONV_DIM), F32),
        pltpu.VMEM((TB_MIX, D_MODEL), BF16),
        pltpu.VMEM((CHUNKS_PER_TILE, D_INNER, SSD_CHUNK), BF16),
        pltpu.VMEM((TB_MIX, BC_DIM), BF16),
        pltpu.VMEM((TB_MIX, BC_DIM), BF16),
        pltpu.VMEM((TB_MIX, LANES), F32),
        pltpu.VMEM((D_INNER, SSD_STATE), F32),
    ]
    return pl.pallas_call(
        functools.partial(_mixf_kernel, zero_init=zero_init, emit_state=emit_state, nblk=nblk, ntiles=ntiles),
        grid=(ntiles + 1,), in_specs=in_specs, out_specs=out_specs, out_shape=out_shape,
        scratch_shapes=scratch,
        compiler_params=_params(("arbitrary",)),
        name="mix_fwd",
    )(*args)


def _mixb_kernel(*refs, zero_init, emit_state, nblk):
    (x_ref, xt_ref, bc_ref, cc_ref, dt_ref, y1_ref, z_ref, gt_ref, yp_ref, mod_ref, alog_ref, dsk_ref,
     ng_ref, wbp_ref, wbs_ref, wo_ref) = refs[:16]
    refs = refs[16:]
    if not zero_init:
        h0_ref, refs = refs[0], refs[1:]
    o_ref, refs = refs[0], refs[1:]
    if emit_state:
        hb_ref, refs = refs[0], refs[1:]
    y_scr, h_scr = refs
    j = pl.program_id(1)

    @pl.when(j == 0)
    def _():
        h_scr[...] = jnp.zeros_like(h_scr) if zero_init else h0_ref[...]

    dsk = dsk_ref[...]
    for ci in reversed(range(CHUNKS_PER_TILE)):
        _ssd_bwd_chunk(ci, xt_ref, bc_ref, cc_ref, dt_ref, alog_ref, dsk, y1_ref, y_scr, h_scr)

    if emit_state:
        @pl.when(j == nblk - 1)
        def _():
            hb_ref[...] = h_scr[...]

    y = _rmsnorm(y_scr[...] * jax.nn.silu(z_ref[...].astype(F32)), ng_ref[...])
    gates = jax.nn.sigmoid(gt_ref[...].astype(F32))
    merged = (gates[:, :D_MODEL] * _dot(yp_ref[...].astype(BF16), wbp_ref[...])
              + gates[:, D_MODEL:] * _dot(y.astype(BF16), wbs_ref[...]))
    o_ref[...] = x_ref[...] + mod_ref[2:3, :] * _dot(merged.astype(BF16), wo_ref[...])


def _mix_bwd(x, fwd, y_pool, mod, lw, h0_b, layer, emit_state):
    b, t, _ = x.shape
    nblk = t // TB_MIX
    zero_init = h0_b is None
    z, gates, dt, xt, bcv, ccv, y1 = fwd
    mod_idx = (lambda i: i) if mod.shape[0] == b else (lambda i: 0)
    tok = lambda n: pl.BlockSpec((None, TB_MIX, n), lambda i, j: (i, nblk - 1 - j, 0))
    chunked = pl.BlockSpec((None, CHUNKS_PER_TILE, D_INNER, SSD_CHUNK), lambda i, j: (i, nblk - 1 - j, 0, 0))
    in_specs = [
        tok(D_MODEL), chunked, tok(BC_DIM), tok(BC_DIM), tok(LANES), chunked, tok(D_INNER), tok(2 * D_MODEL),
        tok(POOL_DIM),
        pl.BlockSpec((None, 3, D_MODEL), lambda i, j: (mod_idx(i), 0, 0)),
        _resident((1, LANES)),
        _resident((LANES, LANES)),
        _resident((1, D_INNER)),
        _resident((POOL_DIM, D_MODEL)),
        _resident((D_INNER, D_MODEL)),
        _resident((D_MODEL, D_MODEL)),
    ]
    args = [x, xt, bcv, ccv, dt, y1, z, gates, y_pool, mod, lw["a_log"], lw["d_skip"], lw["ssd_norm"],
            lw["w_branch_pool"], lw["w_branch_ssd"], lw["w_out"]]
    if not zero_init:
        in_specs.append(pl.BlockSpec((None, None, D_INNER, SSD_STATE), lambda i, j: (i, layer, 0, 0)))
        args.append(h0_b)
    out_specs = [tok(D_MODEL)]
    out_shape = [jax.ShapeDtypeStruct((b, t, D_MODEL), F32)]
    if emit_state:
        out_specs.append(pl.BlockSpec((None, D_INNER, SSD_STATE), lambda i, j: (i, 0, 0)))
        out_shape.append(jax.ShapeDtypeStruct((b, D_INNER, SSD_STATE), F32))
    return pl.pallas_call(
        functools.partial(_mixb_kernel, zero_init=zero_init, emit_state=emit_state, nblk=nblk),
        grid=(b, nblk), in_specs=in_specs, out_specs=out_specs, out_shape=out_shape,
        scratch_shapes=[pltpu.VMEM((TB_MIX, D_INNER), F32), pltpu.VMEM((D_INNER, SSD_STATE), F32)],
        compiler_params=_params(("parallel", "arbitrary")),
        name="mix_bwd",
    )(*args)


def _layer(x, mod, rows, h0_f, h0_b, layer, lw, emit_state, final_g):
    ng = lw["norm_g"]
    x = _ffn(x, mod[:, 0:3], ng[0:1], lw["ffn1_w13"], lw["ffn1_w2"])
    mod2 = mod[:, 3:6]
    fwd = _mix_fwd(x, mod2, ng[1:2], lw, h0_f, layer, emit_state)
    hf = fwd[8] if emit_state else None
    y_pool = _pool_mixer(fwd[0], lw["pool_w"], lw["pool_scale"], rows)
    res = _mix_bwd(x, fwd[1:8], y_pool, mod2, lw, h0_b, layer, emit_state)
    x, hb = (res[0], res[1]) if emit_state else (res[0], None)
    x = _ffn(x, mod[:, 6:9], ng[2:3], lw["ffn2_w13"], lw["ffn2_w2"], final_g)
    return x, hf, hb


def _prep_layer(l, p):
    s1 = POOL_DIM
    s2 = s1 + D_INNER
    s3 = s2 + CONV_DIM
    s4 = s3 + 2 * SSD_HEADS
    w_in = p["w_in"][l]
    pad_lanes = lambda v: jnp.pad(v.reshape(1, -1), ((0, 0), (0, LANES - v.size)))
    return dict(
        norm_g=p["norm_g"][l],
        ffn1_w13=p["ffn1_w13"][l].astype(BF16), ffn1_w2=p["ffn1_w2"][l].astype(BF16),
        ffn2_w13=p["ffn2_w13"][l].astype(BF16), ffn2_w2=p["ffn2_w2"][l].astype(BF16),
        w_pool=w_in[:, :s1].astype(BF16), w_z=w_in[:, s1:s2].astype(BF16), w_xbc=w_in[:, s2:s3].astype(BF16),
        w_dt=jnp.pad(w_in[:, s3:s4], ((0, 0), (0, LANES - 2 * SSD_HEADS))).astype(BF16),
        w_gates=w_in[:, s4:].astype(BF16),
        dt_bias=pad_lanes(p["dt_bias"][l]),
        a_log=pad_lanes(p["a_log"][l]),
        d_skip=jnp.broadcast_to(pad_lanes(p["d_skip"][l]).reshape(LANES, 1), (LANES, LANES)),
        pool_w=p["pool_w"][l].astype(BF16), pool_scale=p["pool_scale"][l].reshape(1, POOL_DIM),
        conv_w=p["conv_w"][l], conv_b=p["conv_b"][l].reshape(1, CONV_DIM),
        ssd_norm=p["ssd_norm"][l].reshape(1, D_INNER),
        w_branch_pool=p["w_branch_pool"][l].astype(BF16), w_branch_ssd=p["w_branch_ssd"][l].astype(BF16),
        w_out=p["w_out"][l].astype(BF16),
    )


def kernel(x_prompt, x_sample, state_ssd_fwd, state_ssd_bwd, c, c_ctx, ada_w, ada_b, norm_g, ffn1_w13, ffn1_w2, ffn2_w13, ffn2_w2, w_in, pool_w, pool_scale, conv_w, conv_b, a_log, dt_bias, d_skip, ssd_norm, w_branch_pool, w_branch_ssd, w_out, final_norm):
    p = dict(norm_g=norm_g, ffn1_w13=ffn1_w13, ffn1_w2=ffn1_w2, ffn2_w13=ffn2_w13, ffn2_w2=ffn2_w2,
             w_in=w_in, pool_w=pool_w, pool_scale=pool_scale, conv_w=conv_w, conv_b=conv_b, a_log=a_log,
             dt_bias=dt_bias, d_skip=d_skip, ssd_norm=ssd_norm, w_branch_pool=w_branch_pool,
             w_branch_ssd=w_branch_ssd, w_out=w_out)
    nb = c.shape[0]
    rows = x_sample.shape[1] // GRID_W
    cond = jnp.concatenate([c_ctx[None, :], c], axis=0)
    cond = jnp.pad(cond, ((0, -(nb + 1) % SUBLANES), (0, 0)))
    mod = _adaln(cond, ada_w, ada_b).reshape(DEPTH, cond.shape[0], N_MOD, D_MODEL)
    fg = final_norm.reshape(1, D_MODEL)
    sd = state_ssd_fwd.shape
    h0_f = state_ssd_fwd.reshape(sd[0], DEPTH, D_INNER, SSD_STATE)
    h0_b = state_ssd_bwd.reshape(sd[0], DEPTH, D_INNER, SSD_STATE)
    xc, xl = x_prompt, x_sample
    new_f, new_b = [], []
    for l in range(DEPTH):
        lw = _prep_layer(l, p)
        last = fg if l == DEPTH - 1 else None
        xc, hf, hb = _layer(xc, mod[l, 0:1], None, None, None, l, lw, True, last)
        new_f.append(hf.reshape(-1, SSD_HEADS, SSD_HEAD_DIM, SSD_STATE))
        new_b.append(hb.reshape(-1, SSD_HEADS, SSD_HEAD_DIM, SSD_STATE))
        xl, _, _ = _layer(xl, mod[l, 1:1 + nb], rows, h0_f, h0_b, l, lw, False, last)
    return (xc, xl, jnp.stack(new_f, axis=1), jnp.stack(new_b, axis=1))
```

```python
import functools
import math

import jax
import jax.numpy as jnp
from jax import lax
from jax.experimental import pallas as pl
from jax.experimental.pallas import tpu as pltpu

F32 = jnp.float32
BF16 = jnp.bfloat16

D_MODEL = 1024
DEPTH = 2
GRID_W = 64
POOL_WINDOWS = (2, 4, 8, 16)
POOL_GROUPS = len(POOL_WINDOWS)
POOL_DIM = D_MODEL
POOL_GROUP_DIM = POOL_DIM // POOL_GROUPS
D_INNER = 2 * D_MODEL
SSD_HEAD_DIM = 64
SSD_HEADS = D_INNER // SSD_HEAD_DIM
SSD_GROUPS = 4
HEADS_PER_GROUP = SSD_HEADS // SSD_GROUPS
GROUP_DIM = HEADS_PER_GROUP * SSD_HEAD_DIM
SSD_STATE = 128
SSD_CONV = 4
SSD_CHUNK = 128
BC_DIM = SSD_GROUPS * SSD_STATE
CONV_DIM = D_INNER + 2 * BC_DIM
D_FF = 2816
N_MOD = 9
EPS = 1e-6
LOG2E = math.log2(math.e)

LANES = 128
SUBLANES = 8
VMEM_LIMIT = 56 * 1024 * 1024

TM_FFN = 512
FF_CHUNK = 256
TB_MIX = 256
CHUNKS_PER_TILE = TB_MIX // SSD_CHUNK
POOL_BLOCK = 256
POOL_PAD_ROWS = 8
POOL_SEQ_PER_STEP = 4
ADA_TN = 1152
PROJ_COLS = 256
SCAN_SCHEDULE = (0,) + (1,) * 7 + (3,) * 9 + (2,) * 4


def _dot(a, b):
    return jnp.dot(a, b, preferred_element_type=F32)


def _dot_nt(a, b):
    return lax.dot_general(a, b, (((1,), (1,)), ((), ())), preferred_element_type=F32)


def _rmsnorm(x, g):
    return x * lax.rsqrt(jnp.mean(x * x, axis=-1, keepdims=True) + EPS) * g


def _params(sem):
    return pltpu.CompilerParams(dimension_semantics=sem, vmem_limit_bytes=VMEM_LIMIT)


def _resident(shape):
    nd = len(shape)
    return pl.BlockSpec(shape, lambda *_: (0,) * nd, pipeline_mode=pl.Buffered(1))


def _ada_kernel(cond_ref, w_ref, b_ref, o_ref):
    s = jax.nn.silu(cond_ref[...]).astype(BF16)
    o_ref[...] = _dot(s, w_ref[...].astype(BF16)) + b_ref[...]


def _adaln(cond, ada_w, ada_b):
    n = cond.shape[0]
    ncol = N_MOD * D_MODEL
    return pl.pallas_call(
        _ada_kernel,
        grid=(DEPTH, ncol // ADA_TN),
        in_specs=[
            pl.BlockSpec((n, D_MODEL), lambda l, j: (0, 0)),
            pl.BlockSpec((None, D_MODEL, ADA_TN), lambda l, j: (l, 0, j)),
            pl.BlockSpec((None, 1, ADA_TN), lambda l, j: (l, 0, j)),
        ],
        out_specs=pl.BlockSpec((None, n, ADA_TN), lambda l, j: (l, 0, j)),
        out_shape=jax.ShapeDtypeStruct((DEPTH, n, ncol), F32),
        compiler_params=_params(("parallel", "parallel")),
        name="adaln",
    )(cond, ada_w, ada_b.reshape(DEPTH, 1, ncol))


def _ffn_kernel(x_ref, mod_ref, g_ref, w13_ref, w2_ref, *rest, final):
    if final:
        fg_ref, o_ref, a_scr = rest
    else:
        o_ref, a_scr = rest
    x = x_ref[...]
    m = mod_ref[...]
    h = (_rmsnorm(x, g_ref[...]) * (1.0 + m[1:2]) + m[0:1]).astype(BF16)
    for j in range(D_FF // FF_CHUNK):
        lo, hi = j * FF_CHUNK, (j + 1) * FF_CHUNK
        gate = _dot(h, w13_ref[:, lo:hi])
        up = _dot(h, w13_ref[:, D_FF + lo:D_FF + hi])
        a_scr[:, lo:hi] = (jax.nn.silu(gate) * up).astype(BF16)
    y = _dot(a_scr[...], w2_ref[...])
    out = x + 0.5 * m[2:3] * y
    if final:
        out = _rmsnorm(out, fg_ref[...])
    o_ref[...] = out


def _ffn(x, mod, g, w13, w2, final_g=None):
    b, t, _ = x.shape
    nt = b * t
    tpm = nt // mod.shape[0]
    final = final_g is not None
    in_specs = [
        pl.BlockSpec((TM_FFN, D_MODEL), lambda i: (i, 0)),
        pl.BlockSpec((None, 3, D_MODEL), lambda i: (i * TM_FFN // tpm, 0, 0)),
        _resident((1, D_MODEL)),
        _resident((D_MODEL, 2 * D_FF)),
        _resident((D_FF, D_MODEL)),
    ]
    args = [x.reshape(nt, D_MODEL), mod, g, w13, w2]
    if final:
        in_specs.append(_resident((1, D_MODEL)))
        args.append(final_g)
    out = pl.pallas_call(
        functools.partial(_ffn_kernel, final=final),
        grid=(nt // TM_FFN,),
        in_specs=in_specs,
        out_specs=pl.BlockSpec((TM_FFN, D_MODEL), lambda i: (i, 0)),
        out_shape=jax.ShapeDtypeStruct((nt, D_MODEL), F32),
        scratch_shapes=[pltpu.VMEM((TM_FFN, D_FF), BF16)],
        compiler_params=_params(("parallel",)),
        name="ffn_final" if final else "ffn",
    )(*args)
    return out.reshape(b, t, D_MODEL)


def _window_count(idx, w, n):
    lo = jnp.maximum(idx - w // 2, 0)
    hi = jnp.minimum(idx - w // 2 + w, n)
    return (hi - lo).astype(F32)


def _band_sum(band, u):
    hi = u.astype(BF16)
    lo = (u - hi.astype(F32)).astype(BF16)
    return _dot(band, hi) + _dot(band, lo)


def _pool_finish(tot, cnt, u, pw_ref, ps_ref):
    d = (tot / cnt - u).astype(BF16)
    return _dot(d, pw_ref[...]) * ps_ref[...]


def _pool_seq_body(u_ref, pw_ref, ps_ref, o_ref, *, w, t):
    ti = lax.broadcasted_iota(jnp.int32, (t, t), 0)
    si = lax.broadcasted_iota(jnp.int32, (t, t), 1)
    band = jnp.where((si >= ti - w // 2) & (si < ti - w // 2 + w), 1.0, 0.0).astype(BF16)
    cnt = _window_count(lax.broadcasted_iota(jnp.int32, (t, POOL_GROUP_DIM), 0), w, t)
    for s in range(u_ref.shape[0]):
        u = u_ref[s]
        o_ref[s] = _pool_finish(_band_sum(band, u), cnt, u, pw_ref, ps_ref)


def _pool_grid_body(u_ref, pw_ref, ps_ref, o_ref, cp_ref, *, w, rows):
    nblk = rows * GRID_W // POOL_BLOCK
    pad = POOL_PAD_ROWS * GRID_W
    ti = lax.broadcasted_iota(jnp.int32, (POOL_BLOCK, POOL_BLOCK), 0)
    si = lax.broadcasted_iota(jnp.int32, (POOL_BLOCK, POOL_BLOCK), 1)
    tc = ti & (GRID_W - 1)
    sc = si & (GRID_W - 1)
    same_row = (ti >> 6) == (si >> 6)
    band = jnp.where(same_row & (sc >= tc - w // 2) & (sc < tc - w // 2 + w), 1.0, 0.0).astype(BF16)
    cp_ref[0:pad + GRID_W, :] = jnp.zeros((pad + GRID_W, POOL_GROUP_DIM), F32)
    rows_per_block = POOL_BLOCK // GRID_W

    def col_pool(i, acc):
        off = pl.multiple_of(i * POOL_BLOCK, POOL_BLOCK)
        cp = _band_sum(band, u_ref[pl.ds(off, POOL_BLOCK), :])
        prefix = []
        for k in range(rows_per_block):
            acc = acc + cp[k * GRID_W:(k + 1) * GRID_W, :]
            prefix.append(acc)
        cp_ref[pl.ds(pad + GRID_W + off, POOL_BLOCK), :] = jnp.concatenate(prefix, axis=0)
        return acc

    total = lax.fori_loop(0, nblk, col_pool, jnp.zeros((GRID_W, POOL_GROUP_DIM), F32), unroll=2)
    for k in range(1, POOL_PAD_ROWS):
        cp_ref[pad + (rows + k) * GRID_W:pad + (rows + k + 1) * GRID_W, :] = total

    tt = lax.broadcasted_iota(jnp.int32, (POOL_BLOCK, POOL_GROUP_DIM), 0)
    ccnt = _window_count(tt & (GRID_W - 1), w, GRID_W)

    def row_pool(i, carry):
        off = pl.multiple_of(i * POOL_BLOCK, POOL_BLOCK)
        tot = (cp_ref[pl.ds(pad + off + (w // 2) * GRID_W, POOL_BLOCK), :]
               - cp_ref[pl.ds(pad + off - (w // 2) * GRID_W, POOL_BLOCK), :])
        r = i * rows_per_block + (tt >> 6)
        cnt = _window_count(r, w, rows) * ccnt
        o_ref[pl.ds(off, POOL_BLOCK), :] = _pool_finish(tot, cnt, u_ref[pl.ds(off, POOL_BLOCK), :], pw_ref, ps_ref)
        return carry

    lax.fori_loop(0, nblk, row_pool, 0, unroll=2)


def _pool_kernel(u_ref, pw_ref, ps_ref, o_ref, *scratch, rows, t):
    g = pl.program_id(1)
    for gi, w in enumerate(POOL_WINDOWS):
        @pl.when(g == gi)
        def _(w=w):
            if rows is None:
                _pool_seq_body(u_ref, pw_ref, ps_ref, o_ref, w=w, t=t)
            else:
                _pool_grid_body(u_ref, pw_ref, ps_ref, o_ref, scratch[0], w=w, rows=rows)


def _pool_mixer(u, pool_w, pool_scale, rows):
    b, t, _ = u.shape
    if rows is None:
        per_step = math.gcd(b, POOL_SEQ_PER_STEP)
        scratch = []
        blk = pl.BlockSpec((per_step, t, POOL_GROUP_DIM), lambda i, g: (i, 0, g))
    else:
        per_step = 1
        scratch = [pltpu.VMEM(((rows + 2 * POOL_PAD_ROWS) * GRID_W, POOL_GROUP_DIM), F32)]
        blk = pl.BlockSpec((None, t, POOL_GROUP_DIM), lambda i, g: (i, 0, g))
    return pl.pallas_call(
        functools.partial(_pool_kernel, rows=rows, t=t),
        grid=(b // per_step, POOL_GROUPS),
        in_specs=[
            blk,
            pl.BlockSpec((None, POOL_GROUP_DIM, POOL_GROUP_DIM), lambda i, g: (g, 0, 0)),
            pl.BlockSpec((1, POOL_GROUP_DIM), lambda i, g: (0, g)),
        ],
        out_specs=blk,
        out_shape=jax.ShapeDtypeStruct((b, t, POOL_DIM), F32),
        scratch_shapes=scratch,
        compiler_params=_params(("parallel", "parallel")),
        name="pool_seq" if rows is None else "pool_grid",
    )(u, pool_w, pool_scale)


def _ssd_chunk_terms(dtc, alog):
    n = SSD_CHUNK
    da = dtc * (-jnp.exp(alog) * LOG2E)
    row = lax.broadcasted_iota(jnp.int32, da.shape, 0)
    cs = da
    rcs = da
    k = 1
    while k < n:
        cs = cs + jnp.where(row >= k, pltpu.roll(cs, k, 0), 0.0)
        rcs = rcs + jnp.where(row < n - k, pltpu.roll(rcs, n - k, 0), 0.0)
        k *= 2
    logdt = jnp.log(dtc) * LOG2E
    return dict(colf=cs - logdt, colb=rcs - logdt, cst=cs.T, rcst=rcs.T, dtt=dtc.T)


def _rows64(mat, h):
    return jnp.broadcast_to(mat[h:h + 1, :], (SSD_HEAD_DIM, mat.shape[1]))


def _pair_rows(mat, h0):
    return jnp.concatenate([_rows64(mat, h0), _rows64(mat, h0 + 1)], axis=0)


def _state_update(h_scr, g, xw, bc_g, dec_t, head_off):
    rows = slice(g * GROUP_DIM, (g + 1) * GROUP_DIM)
    decay = jnp.concatenate(
        [_rows64(dec_t, head_off + g * HEADS_PER_GROUP + h) for h in range(HEADS_PER_GROUP)], axis=0)
    h_scr[rows, :] = h_scr[rows, :] * decay + _dot(xw, bc_g)


def _ssd_fwd_chunk(ci, xt_ref, bc_ref, cc_ref, dt_ref, alog_ref, y1_ref, h_scr):
    n = SSD_CHUNK
    tok = slice(ci * n, (ci + 1) * n)
    tm = _ssd_chunk_terms(dt_ref[tok, :], alog_ref[...])
    cst, rcst, dtt = tm["cst"], tm["rcst"], tm["dtt"]
    tot = cst[:, n - 1:n]
    eoff_t = jnp.exp2(cst)
    w_t = dtt * jnp.exp2(tot - cst)
    dec_t = jnp.exp2(jnp.broadcast_to(tot, (LANES, SSD_STATE)))
    si = lax.broadcasted_iota(jnp.int32, (n, n), 0)
    li = lax.broadcasted_iota(jnp.int32, (n, n), 1)
    src_before = si <= li
    src_after = si >= li
    zeros64 = jnp.zeros((SSD_HEAD_DIM, n), BF16)
    yield
    for g in range(SSD_GROUPS):
        bc_g = bc_ref[tok, g * SSD_STATE:(g + 1) * SSD_STATE]
        cc_g = cc_ref[tok, g * SSD_STATE:(g + 1) * SSD_STATE]
        sc_t = _dot_nt(bc_g, cc_g)
        y_off = _dot_nt(h_scr[g * GROUP_DIM:(g + 1) * GROUP_DIM, :].astype(BF16), cc_g)
        xw = []
        for j in range(HEADS_PER_GROUP // 2):
            h0 = g * HEADS_PER_GROUP + 2 * j
            mts = []
            for h in (h0, h0 + 1):
                seg_f = cst[h:h + 1, :] - tm["colf"][:, h:h + 1]
                hb = SSD_HEADS + h
                seg_b = rcst[hb:hb + 1, :] - tm["colb"][:, hb:hb + 1]
                dec = (jnp.exp2(jnp.where(src_before, seg_f, -jnp.inf))
                       + jnp.exp2(jnp.where(src_after, seg_b, -jnp.inf)))
                mts.append((sc_t * dec).astype(BF16))
            r0 = h0 * SSD_HEAD_DIM
            xp = xt_ref[ci, r0:r0 + 2 * SSD_HEAD_DIM, :]
            lhs = jnp.concatenate([jnp.concatenate([xp[:SSD_HEAD_DIM], zeros64], axis=0),
                                   jnp.concatenate([zeros64, xp[SSD_HEAD_DIM:]], axis=0)], axis=1)
            y_diag = _dot(lhs, jnp.concatenate(mts, axis=0))
            yo = y_off[2 * j * SSD_HEAD_DIM:(2 * j + 2) * SSD_HEAD_DIM, :]
            y1_ref[ci, r0:r0 + 2 * SSD_HEAD_DIM, :] = (y_diag + yo * _pair_rows(eoff_t, h0)).astype(y1_ref.dtype)
            xw.append((xp.astype(F32) * _pair_rows(w_t, h0)).astype(BF16))
            yield
        _state_update(h_scr, g, jnp.concatenate(xw, axis=0), bc_g, dec_t, 0)
        yield


def _ssd_bwd_chunk(ci, xt_ref, bc_ref, cc_ref, dt_ref, alog_ref, dsk, y1_ref, y_scr, h_scr):
    n = SSD_CHUNK
    tok = slice(ci * n, (ci + 1) * n)
    tm = _ssd_chunk_terms(dt_ref[tok, :], alog_ref[...])
    rcst, dtt = tm["rcst"], tm["dtt"]
    tot = rcst[:, 0:1]
    eoff_t = jnp.exp2(rcst)
    w_t = dtt * jnp.exp2(tot - rcst)
    dec_t = jnp.exp2(jnp.broadcast_to(tot, (LANES, SSD_STATE)))
    for g in range(SSD_GROUPS):
        bc_g = bc_ref[tok, g * SSD_STATE:(g + 1) * SSD_STATE]
        cc_g = cc_ref[tok, g * SSD_STATE:(g + 1) * SSD_STATE]
        y_off = _dot_nt(h_scr[g * GROUP_DIM:(g + 1) * GROUP_DIM, :].astype(BF16), cc_g)
        xw, yt = [], []
        for j in range(HEADS_PER_GROUP // 2):
            h0 = g * HEADS_PER_GROUP + 2 * j
            r0 = h0 * SSD_HEAD_DIM
            xp = xt_ref[ci, r0:r0 + 2 * SSD_HEAD_DIM, :].astype(F32)
            yo = y_off[2 * j * SSD_HEAD_DIM:(2 * j + 2) * SSD_HEAD_DIM, :]
            yt.append(y1_ref[ci, r0:r0 + 2 * SSD_HEAD_DIM, :].astype(F32)
                      + yo * _pair_rows(eoff_t, SSD_HEADS + h0) + _pair_rows(dsk, h0) * xp)
            xw.append((xp * _pair_rows(w_t, SSD_HEADS + h0)).astype(BF16))
        y_scr[tok, g * GROUP_DIM:(g + 1) * GROUP_DIM] = jnp.concatenate(yt, axis=0).T
        _state_update(h_scr, g, jnp.concatenate(xw, axis=0), bc_g, dec_t, SSD_HEADS)


def _project_and_conv(j, nblk, x_ref, xp_ref, xn_ref, mod_ref, g_ref, wp_ref, wz_ref, wx_ref, wd_ref, wg_ref,
                      dtb_ref, cw_ref, cb_ref, up_ref, z_ref, gt_ref, dt_ref, xt_ref, bc_ref, cc_ref,
                      ext_ref, hm_scr):
    m = mod_ref[...]
    g = g_ref[...]

    def modulated(x):
        return (_rmsnorm(x, g) * (1.0 + m[1:2]) + m[0:1]).astype(BF16)

    hm_scr[...] = modulated(x_ref[...])
    left = SSD_CONV // 2
    prev = jnp.where(j > 0, _dot(modulated(xp_ref[...]), wx_ref[...]), 0.0)
    for k in range(left):
        ext_ref[k, left - k:left - k + SUBLANES, :] = prev
    nxt = jnp.where(j < nblk - 1, _dot(modulated(xn_ref[...]), wx_ref[...]), 0.0)
    for k in range(left + 1, SSD_CONV):
        ext_ref[k, SUBLANES + left - k + TB_MIX:2 * SUBLANES + left - k + TB_MIX, :] = nxt
    yield
    ct = BC_DIM
    n_ct = CONV_DIM // ct

    def project_xbc(c):
        cols = slice(c * ct, (c + 1) * ct)
        xbc = _dot(hm_scr[...], wx_ref[:, cols])
        for k in range(SSD_CONV):
            ext_ref[k, SUBLANES + left - k:SUBLANES + left - k + TB_MIX, cols] = xbc

    def conv(c):
        cols = slice(c * ct, (c + 1) * ct)
        acc = cb_ref[:, cols] + ext_ref[0, SUBLANES:SUBLANES + TB_MIX, cols] * cw_ref[0:1, cols]
        for k in range(1, SSD_CONV):
            acc = acc + ext_ref[k, SUBLANES:SUBLANES + TB_MIX, cols] * cw_ref[k:k + 1, cols]
        y = jax.nn.silu(acc)
        if c < D_INNER // ct:
            for ci in range(CHUNKS_PER_TILE):
                xt_ref[ci, cols, :] = y[ci * SSD_CHUNK:(ci + 1) * SSD_CHUNK, :].T.astype(BF16)
        elif c == D_INNER // ct:
            bc_ref[...] = y.astype(BF16)
        else:
            cc_ref[...] = y.astype(BF16)

    project_xbc(0)
    yield
    for c in range(n_ct):
        if c + 1 < n_ct:
            project_xbc(c + 1)
        if c < D_INNER // ct:
            zc = slice(c * ct, (c + 1) * ct)
            z_ref[:, zc] = _dot(hm_scr[...], wz_ref[:, zc]).astype(z_ref.dtype)
        conv(c)
        yield
    dt_ref[...] = jax.nn.softplus(_dot(hm_scr[...], wd_ref[...]) + dtb_ref[...])
    yield
    for c in range(2 * D_MODEL // PROJ_COLS):
        gc = slice(c * PROJ_COLS, (c + 1) * PROJ_COLS)
        gt_ref[:, gc] = _dot(hm_scr[...], wg_ref[:, gc]).astype(gt_ref.dtype)
        yield
    for c in range(POOL_DIM // PROJ_COLS):
        pc = slice(c * PROJ_COLS, (c + 1) * PROJ_COLS)
        up_ref[:, pc] = _dot(hm_scr[...], wp_ref[:, pc])
        yield


def _mixf_kernel(*refs, zero_init, emit_state, nblk, ntiles):
    (x_ref, xp_ref, xn_ref, mod_ref, g_ref, wp_ref, wz_ref, wx_ref, wd_ref, wg_ref, dtb_ref,
     cw_ref, cb_ref, alog_ref) = refs[:14]
    refs = refs[14:]
    if not zero_init:
        h0_ref, refs = refs[0], refs[1:]
    up_ref, z_ref, gt_ref, dt_ref, xt_ref, bc_ref, cc_ref, y1_ref = refs[:8]
    refs = refs[8:]
    if emit_state:
        hf_ref, refs = refs[0], refs[1:]
    ext_ref, hm_scr, xt_cur, bc_cur, cc_cur, dt_cur, h_scr = refs
    s = pl.program_id(0)
    ja = lax.rem(jnp.minimum(s, ntiles - 1), nblk)
    jb = lax.rem(jnp.maximum(s - 1, 0), nblk)

    @pl.when(s == 0)
    def _():
        xt_cur[...] = jnp.zeros_like(xt_cur)
        bc_cur[...] = jnp.zeros_like(bc_cur)
        cc_cur[...] = jnp.zeros_like(cc_cur)
        dt_cur[...] = jnp.ones_like(dt_cur)

    @pl.when(jb == 0)
    def _():
        h_scr[...] = jnp.zeros_like(h_scr) if zero_init else h0_ref[...]

    proj = _project_and_conv(ja, nblk, x_ref, xp_ref, xn_ref, mod_ref, g_ref, wp_ref, wz_ref, wx_ref, wd_ref,
                             wg_ref, dtb_ref, cw_ref, cb_ref, up_ref, z_ref, gt_ref, dt_ref, xt_ref, bc_ref,
                             cc_ref, ext_ref, hm_scr)
    scan = (None for ci in range(CHUNKS_PER_TILE)
            for _ in _ssd_fwd_chunk(ci, xt_cur, bc_cur, cc_cur, dt_cur, alog_ref, y1_ref, h_scr))
    for n_scan in SCAN_SCHEDULE:
        next(proj, None)
        for _ in range(n_scan):
            next(scan, None)
    for _ in proj:
        pass
    for _ in scan:
        pass

    if emit_state:
        @pl.when(jb == nblk - 1)
        def _():
            hf_ref[...] = h_scr[...]

    xt_cur[...] = xt_ref[...]
    bc_cur[...] = bc_ref[...]
    cc_cur[...] = cc_ref[...]
    dt_cur[...] = dt_ref[...]


def _mix_fwd(x, mod, g, lw, h0_f, layer, emit_state):
    b, t, _ = x.shape
    nblk = t // TB_MIX
    ntiles = b * nblk
    nc = t // SSD_CHUNK
    per = TB_MIX // SUBLANES
    zero_init = h0_f is None
    mod_idx = (lambda i: i) if mod.shape[0] == b else (lambda i: 0)
    ta = lambda s: jnp.minimum(s, ntiles - 1)
    tb = lambda s: jnp.maximum(s - 1, 0)
    tok = lambda n: pl.BlockSpec((None, TB_MIX, n), lambda s: (ta(s) // nblk, ta(s) % nblk, 0))
    chunked = lambda tile: pl.BlockSpec((None, CHUNKS_PER_TILE, D_INNER, SSD_CHUNK),
                                        lambda s: (tile(s) // nblk, tile(s) % nblk, 0, 0))
    in_specs = [
        tok(D_MODEL),
        pl.BlockSpec((None, SUBLANES, D_MODEL),
                     lambda s: (ta(s) // nblk, jnp.maximum((ta(s) % nblk) * per - 1, 0), 0)),
        pl.BlockSpec((None, SUBLANES, D_MODEL),
                     lambda s: (ta(s) // nblk, jnp.minimum((ta(s) % nblk + 1) * per, t // SUBLANES - 1), 0)),
        pl.BlockSpec((None, 3, D_MODEL), lambda s: (mod_idx(ta(s) // nblk), 0, 0)),
        _resident((1, D_MODEL)),
        _resident((D_MODEL, POOL_DIM)),
        _resident((D_MODEL, D_INNER)),
        _resident((D_MODEL, CONV_DIM)),
        _resident((D_MODEL, LANES)),
        _resident((D_MODEL, 2 * D_MODEL)),
        _resident((1, LANES)),
        _resident((SSD_CONV, CONV_DIM)),
        _resident((1, CONV_DIM)),
        _resident((1, LANES)),
    ]
    args = [x, x, x, mod, g, lw["w_pool"], lw["w_z"], lw["w_xbc"], lw["w_dt"], lw["w_gates"], lw["dt_bias"],
            lw["conv_w"], lw["conv_b"], lw["a_log"]]
    if not zero_init:
        in_specs.append(pl.BlockSpec((None, None, D_INNER, SSD_STATE), lambda s: (tb(s) // nblk, layer, 0, 0)))
        args.append(h0_f)
    out_specs = [tok(POOL_DIM), tok(D_INNER), tok(2 * D_MODEL), tok(LANES), chunked(ta), tok(BC_DIM), tok(BC_DIM),
                 chunked(tb)]
    out_shape = [
        jax.ShapeDtypeStruct((b, t, POOL_DIM), F32),
        jax.ShapeDtypeStruct((b, t, D_INNER), BF16),
        jax.ShapeDtypeStruct((b, t, 2 * D_MODEL), BF16),
        jax.ShapeDtypeStruct((b, t, LANES), F32),
        jax.ShapeDtypeStruct((b, nc, D_INNER, SSD_CHUNK), BF16),
        jax.ShapeDtypeStruct((b, t, BC_DIM), BF16),
        jax.ShapeDtypeStruct((b, t, BC_DIM), BF16),
        jax.ShapeDtypeStruct((b, nc, D_INNER, SSD_CHUNK), BF16),
    ]
    if emit_state:
        out_specs.append(pl.BlockSpec((None, D_INNER, SSD_STATE), lambda s: (tb(s) // nblk, 0, 0)))
        out_shape.append(jax.ShapeDtypeStruct((b, D_INNER, SSD_STATE), F32))
    scratch = [
        pltpu.VMEM((SSD_CONV, TB_MIX + 2 * SUBLANES, CONV_DIM), F32),
        pltpu.VMEM((TB_MIX, D_MODEL), BF16),
        pltpu.VMEM((CHUNKS_PER_TILE, D_INNER, SSD_CHUNK), BF16),
        pltpu.VMEM((TB_MIX, BC_DIM), BF16),
        pltpu.VMEM((TB_MIX, BC_DIM), BF16),
        pltpu.VMEM((TB_MIX, LANES), F32),
        pltpu.VMEM((D_INNER, SSD_STATE), F32),
    ]
    return pl.pallas_call(
        functools.partial(_mixf_kernel, zero_init=zero_init, emit_state=emit_state, nblk=nblk, ntiles=ntiles),
        grid=(ntiles + 1,), in_specs=in_specs, out_specs=out_specs, out_shape=out_shape,
        scratch_shapes=scratch,
        compiler_params=_params(("arbitrary",)),
        name="mix_fwd",
    )(*args)


def _mixb_kernel(*refs, zero_init, emit_state, nblk):
    (x_ref, xt_ref, bc_ref, cc_ref, dt_ref, y1_ref, z_ref, gt_ref, yp_ref, mod_ref, alog_ref, dsk_ref,
     ng_ref, wbp_ref, wbs_ref, wo_ref) = refs[:16]
    refs = refs[16:]
    if not zero_init:
        h0_ref, refs = refs[0], refs[1:]
    o_ref, refs = refs[0], refs[1:]
    if emit_state:
        hb_ref, refs = refs[0], refs[1:]
    y_scr, h_scr = refs
    j = pl.program_id(1)

    @pl.when(j == 0)
    def _():
        h_scr[...] = jnp.zeros_like(h_scr) if zero_init else h0_ref[...]

    dsk = dsk_ref[...]
    for ci in reversed(range(CHUNKS_PER_TILE)):
        _ssd_bwd_chunk(ci, xt_ref, bc_ref, cc_ref, dt_ref, alog_ref, dsk, y1_ref, y_scr, h_scr)

    if emit_state:
        @pl.when(j == nblk - 1)
        def _():
            hb_ref[...] = h_scr[...]

    y = _rmsnorm(y_scr[...] * jax.nn.silu(z_ref[...].astype(F32)), ng_ref[...])
    gates = jax.nn.sigmoid(gt_ref[...].astype(F32))
    merged = (gates[:, :D_MODEL] * _dot(yp_ref[...].astype(BF16), wbp_ref[...])
              + gates[:, D_MODEL:] * _dot(y.astype(BF16), wbs_ref[...]))
    o_ref[...] = x_ref[...] + mod_ref[2:3, :] * _dot(merged.astype(BF16), wo_ref[...])


def _mix_bwd(x, fwd, y_pool, mod, lw, h0_b, layer, emit_state):
    b, t, _ = x.shape
    nblk = t // TB_MIX
    zero_init = h0_b is None
    z, gates, dt, xt, bcv, ccv, y1 = fwd
    mod_idx = (lambda i: i) if mod.shape[0] == b else (lambda i: 0)
    tok = lambda n: pl.BlockSpec((None, TB_MIX, n), lambda i, j: (i, nblk - 1 - j, 0))
    chunked = pl.BlockSpec((None, CHUNKS_PER_TILE, D_INNER, SSD_CHUNK), lambda i, j: (i, nblk - 1 - j, 0, 0))
    in_specs = [
        tok(D_MODEL), chunked, tok(BC_DIM), tok(BC_DIM), tok(LANES), chunked, tok(D_INNER), tok(2 * D_MODEL),
        tok(POOL_DIM),
        pl.BlockSpec((None, 3, D_MODEL), lambda i, j: (mod_idx(i), 0, 0)),
        _resident((1, LANES)),
        _resident((LANES, LANES)),
        _resident((1, D_INNER)),
        _resident((POOL_DIM, D_MODEL)),
        _resident((D_INNER, D_MODEL)),
        _resident((D_MODEL, D_MODEL)),
    ]
    args = [x, xt, bcv, ccv, dt, y1, z, gates, y_pool, mod, lw["a_log"], lw["d_skip"], lw["ssd_norm"],
            lw["w_branch_pool"], lw["w_branch_ssd"], lw["w_out"]]
    if not zero_init:
        in_specs.append(pl.BlockSpec((None, None, D_INNER, SSD_STATE), lambda i, j: (i, layer, 0, 0)))
        args.append(h0_b)
    out_specs = [tok(D_MODEL)]
    out_shape = [jax.ShapeDtypeStruct((b, t, D_MODEL), F32)]
    if emit_state:
        out_specs.append(pl.BlockSpec((None, D_INNER, SSD_STATE), lambda i, j: (i, 0, 0)))
        out_shape.append(jax.ShapeDtypeStruct((b, D_INNER, SSD_STATE), F32))
    return pl.pallas_call(
        functools.partial(_mixb_kernel, zero_init=zero_init, emit_state=emit_state, nblk=nblk),
        grid=(b, nblk), in_specs=in_specs, out_specs=out_specs, out_shape=out_shape,
        scratch_shapes=[pltpu.VMEM((TB_MIX, D_INNER), F32), pltpu.VMEM((D_INNER, SSD_STATE), F32)],
        compiler_params=_params(("parallel", "arbitrary")),
        name="mix_bwd",
    )(*args)


def _layer(x, mod, rows, h0_f, h0_b, layer, lw, emit_state, final_g):
    ng = lw["norm_g"]
    x = _ffn(x, mod[:, 0:3], ng[0:1], lw["ffn1_w13"], lw["ffn1_w2"])
    mod2 = mod[:, 3:6]
    fwd = _mix_fwd(x, mod2, ng[1:2], lw, h0_f, layer, emit_state)
    hf = fwd[8] if emit_state else None
    y_pool = _pool_mixer(fwd[0], lw["pool_w"], lw["pool_scale"], rows)
    res = _mix_bwd(x, fwd[1:8], y_pool, mod2, lw, h0_b, layer, emit_state)
    x, hb = (res[0], res[1]) if emit_state else (res[0], None)
    x = _ffn(x, mod[:, 6:9], ng[2:3], lw["ffn2_w13"], lw["ffn2_w2"], final_g)
    return x, hf, hb


def _prep_layer(l, p):
    s1 = POOL_DIM
    s2 = s1 + D_INNER
    s3 = s2 + CONV_DIM
    s4 = s3 + 2 * SSD_HEADS
    w_in = p["w_in"][l]
    pad_lanes = lambda v: jnp.pad(v.reshape(1, -1), ((0, 0), (0, LANES - v.size)))
    return dict(
        norm_g=p["norm_g"][l],
        ffn1_w13=p["ffn1_w13"][l].astype(BF16), ffn1_w2=p["ffn1_w2"][l].astype(BF16),
        ffn2_w13=p["ffn2_w13"][l].astype(BF16), ffn2_w2=p["ffn2_w2"][l].astype(BF16),
        w_pool=w_in[:, :s1].astype(BF16), w_z=w_in[:, s1:s2].astype(BF16), w_xbc=w_in[:, s2:s3].astype(BF16),
        w_dt=jnp.pad(w_in[:, s3:s4], ((0, 0), (0, LANES - 2 * SSD_HEADS))).astype(BF16),
        w_gates=w_in[:, s4:].astype(BF16),
        dt_bias=pad_lanes(p["dt_bias"][l]),
        a_log=pad_lanes(p["a_log"][l]),
        d_skip=jnp.broadcast_to(pad_lanes(p["d_skip"][l]).reshape(LANES, 1), (LANES, LANES)),
        pool_w=p["pool_w"][l].astype(BF16), pool_scale=p["pool_scale"][l].reshape(1, POOL_DIM),
        conv_w=p["conv_w"][l], conv_b=p["conv_b"][l].reshape(1, CONV_DIM),
        ssd_norm=p["ssd_norm"][l].reshape(1, D_INNER),
        w_branch_pool=p["w_branch_pool"][l].astype(BF16), w_branch_ssd=p["w_branch_ssd"][l].astype(BF16),
        w_out=p["w_out"][l].astype(BF16),
    )


def kernel(x_prompt, x_sample, state_ssd_fwd, state_ssd_bwd, c, c_ctx, ada_w, ada_b, norm_g, ffn1_w13, ffn1_w2, ffn2_w13, ffn2_w2, w_in, pool_w, pool_scale, conv_w, conv_b, a_log, dt_bias, d_skip, ssd_norm, w_branch_pool, w_branch_ssd, w_out, final_norm):
    p = dict(norm_g=norm_g, ffn1_w13=ffn1_w13, ffn1_w2=ffn1_w2, ffn2_w13=ffn2_w13, ffn2_w2=ffn2_w2,
             w_in=w_in, pool_w=pool_w, pool_scale=pool_scale, conv_w=conv_w, conv_b=conv_b, a_log=a_log,
             dt_bias=dt_bias, d_skip=d_skip, ssd_norm=ssd_norm, w_branch_pool=w_branch_pool,
             w_branch_ssd=w_branch_ssd, w_out=w_out)
    nb = c.shape[0]
    rows = x_sample.shape[1] // GRID_W
    cond = jnp.concatenate([c_ctx[None, :], c], axis=0)
    cond = jnp.pad(cond, ((0, -(nb + 1) % SUBLANES), (0, 0)))
    mod = _adaln(cond, ada_w, ada_b).reshape(DEPTH, cond.shape[0], N_MOD, D_MODEL)
    fg = final_norm.reshape(1, D_MODEL)
    sd = state_ssd_fwd.shape
    h0_f = state_ssd_fwd.reshape(sd[0], DEPTH, D_INNER, SSD_STATE)
    h0_b = state_ssd_bwd.reshape(sd[0], DEPTH, D_INNER, SSD_STATE)
    xc, xl = x_prompt, x_sample
    new_f, new_b = [], []
    for l in range(DEPTH):
        lw = _prep_layer(l, p)
        last = fg if l == DEPTH - 1 else None
        xc, hf, hb = _layer(xc, mod[l, 0:1], None, None, None, l, lw, True, last)
        new_f.append(hf.reshape(-1, SSD_HEADS, SSD_HEAD_DIM, SSD_STATE))
        new_b.append(hb.reshape(-1, SSD_HEADS, SSD_HEAD_DIM, SSD_STATE))
        xl, _, _ = _layer(xl, mod[l, 1:1 + nb], rows, h0_f, h0_b, l, lw, False, last)
    return (xc, xl, jnp.stack(new_f, axis=1), jnp.stack(new_b, axis=1))
```

```python
import functools
import math

import jax
import jax.numpy as jnp
from jax import lax
from jax.experimental import pallas as pl
from jax.experimental.pallas import tpu as pltpu

F32 = jnp.float32
BF16 = jnp.bfloat16

D_MODEL = 1024
DEPTH = 2
GRID_W = 64
POOL_WINDOWS = (2, 4, 8, 16)
POOL_GROUPS = len(POOL_WINDOWS)
POOL_DIM = D_MODEL
POOL_GROUP_DIM = POOL_DIM // POOL_GROUPS
D_INNER = 2 * D_MODEL
SSD_HEAD_DIM = 64
SSD_HEADS = D_INNER // SSD_HEAD_DIM
SSD_GROUPS = 4
HEADS_PER_GROUP = SSD_HEADS // SSD_GROUPS
GROUP_DIM = HEADS_PER_GROUP * SSD_HEAD_DIM
SSD_STATE = 128
SSD_CONV = 4
SSD_CHUNK = 128
BC_DIM = SSD_GROUPS * SSD_STATE
CONV_DIM = D_INNER + 2 * BC_DIM
D_FF = 2816
N_MOD = 9
EPS = 1e-6
LOG2E = math.log2(math.e)

LANES = 128
SUBLANES = 8
VMEM_LIMIT = 56 * 1024 * 1024

TM_FFN = 512
FF_CHUNK = 256
TB_MIX = 256
CHUNKS_PER_TILE = TB_MIX // SSD_CHUNK
CONV_TILE = BC_DIM
CONV_HALO = 16
POOL_BLOCK = 256
POOL_PAD_ROWS = 8
POOL_SEQ_PER_STEP = 4
ADA_TN = 1152
PROJ_COLS = 256
SCAN_SCHEDULE = (0,) + (1,) * 7 + (3,) * 9 + (2,) * 4


def _dot(a, b):
    return jnp.dot(a, b, preferred_element_type=F32)


def _dot_nt(a, b):
    return lax.dot_general(a, b, (((1,), (1,)), ((), ())), preferred_element_type=F32)


def _rmsnorm(x, g):
    return x * lax.rsqrt(jnp.mean(x * x, axis=-1, keepdims=True) + EPS) * g


def _params(sem):
    return pltpu.CompilerParams(dimension_semantics=sem, vmem_limit_bytes=VMEM_LIMIT)


def _resident(shape):
    nd = len(shape)
    return pl.BlockSpec(shape, lambda *_: (0,) * nd, pipeline_mode=pl.Buffered(1))


def _ada_kernel(cond_ref, w_ref, b_ref, o_ref):
    s = jax.nn.silu(cond_ref[...]).astype(BF16)
    o_ref[...] = _dot(s, w_ref[...].astype(BF16)) + b_ref[...]


def _adaln(cond, ada_w, ada_b):
    n = cond.shape[0]
    ncol = N_MOD * D_MODEL
    return pl.pallas_call(
        _ada_kernel,
        grid=(DEPTH, ncol // ADA_TN),
        in_specs=[
            pl.BlockSpec((n, D_MODEL), lambda l, j: (0, 0)),
            pl.BlockSpec((None, D_MODEL, ADA_TN), lambda l, j: (l, 0, j)),
            pl.BlockSpec((None, 1, ADA_TN), lambda l, j: (l, 0, j)),
        ],
        out_specs=pl.BlockSpec((None, n, ADA_TN), lambda l, j: (l, 0, j)),
        out_shape=jax.ShapeDtypeStruct((DEPTH, n, ncol), F32),
        compiler_params=_params(("parallel", "parallel")),
        name="adaln",
    )(cond, ada_w, ada_b.reshape(DEPTH, 1, ncol))


def _ffn_kernel(x_ref, mod_ref, g_ref, w13_ref, w2_ref, *rest, final):
    if final:
        fg_ref, o_ref, a_scr = rest
    else:
        o_ref, a_scr = rest
    x = x_ref[...]
    m = mod_ref[...]
    h = (_rmsnorm(x, g_ref[...]) * (1.0 + m[1:2]) + m[0:1]).astype(BF16)
    for j in range(D_FF // FF_CHUNK):
        lo, hi = j * FF_CHUNK, (j + 1) * FF_CHUNK
        gate = _dot(h, w13_ref[:, lo:hi])
        up = _dot(h, w13_ref[:, D_FF + lo:D_FF + hi])
        a_scr[:, lo:hi] = (jax.nn.silu(gate) * up).astype(BF16)
    y = _dot(a_scr[...], w2_ref[...])
    out = x + 0.5 * m[2:3] * y
    if final:
        out = _rmsnorm(out, fg_ref[...])
    o_ref[...] = out


def _ffn(x, mod, g, w13, w2, final_g=None):
    b, t, _ = x.shape
    nt = b * t
    tpm = nt // mod.shape[0]
    final = final_g is not None
    in_specs = [
        pl.BlockSpec((TM_FFN, D_MODEL), lambda i: (i, 0)),
        pl.BlockSpec((None, 3, D_MODEL), lambda i: (i * TM_FFN // tpm, 0, 0)),
        _resident((1, D_MODEL)),
        _resident((D_MODEL, 2 * D_FF)),
        _resident((D_FF, D_MODEL)),
    ]
    args = [x.reshape(nt, D_MODEL), mod, g, w13, w2]
    if final:
        in_specs.append(_resident((1, D_MODEL)))
        args.append(final_g)
    out = pl.pallas_call(
        functools.partial(_ffn_kernel, final=final),
        grid=(nt // TM_FFN,),
        in_specs=in_specs,
        out_specs=pl.BlockSpec((TM_FFN, D_MODEL), lambda i: (i, 0)),
        out_shape=jax.ShapeDtypeStruct((nt, D_MODEL), F32),
        scratch_shapes=[pltpu.VMEM((TM_FFN, D_FF), BF16)],
        compiler_params=_params(("parallel",)),
        name="ffn_final" if final else "ffn",
    )(*args)
    return out.reshape(b, t, D_MODEL)


def _window_count(idx, w, n):
    lo = jnp.maximum(idx - w // 2, 0)
    hi = jnp.minimum(idx - w // 2 + w, n)
    return (hi - lo).astype(F32)


def _band_sum(band, u):
    hi = u.astype(BF16)
    lo = (u - hi.astype(F32)).astype(BF16)
    return _dot(band, hi) + _dot(band, lo)


def _pool_finish(tot, cnt, u, pw_ref, ps_ref):
    d = (tot / cnt - u).astype(BF16)
    return _dot(d, pw_ref[...]) * ps_ref[...]


def _pool_seq_body(u_ref, pw_ref, ps_ref, o_ref, *, w, t):
    ti = lax.broadcasted_iota(jnp.int32, (t, t), 0)
    si = lax.broadcasted_iota(jnp.int32, (t, t), 1)
    band = jnp.where((si >= ti - w // 2) & (si < ti - w // 2 + w), 1.0, 0.0).astype(BF16)
    cnt = _window_count(lax.broadcasted_iota(jnp.int32, (t, POOL_GROUP_DIM), 0), w, t)
    for s in range(u_ref.shape[0]):
        u = u_ref[s]
        o_ref[s] = _pool_finish(_band_sum(band, u), cnt, u, pw_ref, ps_ref)


def _pool_grid_body(u_ref, pw_ref, ps_ref, o_ref, cp_ref, *, w, rows):
    nblk = rows * GRID_W // POOL_BLOCK
    pad = POOL_PAD_ROWS * GRID_W
    ti = lax.broadcasted_iota(jnp.int32, (POOL_BLOCK, POOL_BLOCK), 0)
    si = lax.broadcasted_iota(jnp.int32, (POOL_BLOCK, POOL_BLOCK), 1)
    tc = ti & (GRID_W - 1)
    sc = si & (GRID_W - 1)
    same_row = (ti >> 6) == (si >> 6)
    band = jnp.where(same_row & (sc >= tc - w // 2) & (sc < tc - w // 2 + w), 1.0, 0.0).astype(BF16)
    cp_ref[0:pad + GRID_W, :] = jnp.zeros((pad + GRID_W, POOL_GROUP_DIM), F32)
    rows_per_block = POOL_BLOCK // GRID_W

    def col_pool(i, acc):
        off = pl.multiple_of(i * POOL_BLOCK, POOL_BLOCK)
        cp = _band_sum(band, u_ref[pl.ds(off, POOL_BLOCK), :])
        prefix = []
        for k in range(rows_per_block):
            acc = acc + cp[k * GRID_W:(k + 1) * GRID_W, :]
            prefix.append(acc)
        cp_ref[pl.ds(pad + GRID_W + off, POOL_BLOCK), :] = jnp.concatenate(prefix, axis=0)
        return acc

    total = lax.fori_loop(0, nblk, col_pool, jnp.zeros((GRID_W, POOL_GROUP_DIM), F32), unroll=2)
    for k in range(1, POOL_PAD_ROWS):
        cp_ref[pad + (rows + k) * GRID_W:pad + (rows + k + 1) * GRID_W, :] = total

    tt = lax.broadcasted_iota(jnp.int32, (POOL_BLOCK, POOL_GROUP_DIM), 0)
    ccnt = _window_count(tt & (GRID_W - 1), w, GRID_W)

    def row_pool(i, carry):
        off = pl.multiple_of(i * POOL_BLOCK, POOL_BLOCK)
        tot = (cp_ref[pl.ds(pad + off + (w // 2) * GRID_W, POOL_BLOCK), :]
               - cp_ref[pl.ds(pad + off - (w // 2) * GRID_W, POOL_BLOCK), :])
        r = i * rows_per_block + (tt >> 6)
        cnt = _window_count(r, w, rows) * ccnt
        o_ref[pl.ds(off, POOL_BLOCK), :] = _pool_finish(tot, cnt, u_ref[pl.ds(off, POOL_BLOCK), :], pw_ref, ps_ref)
        return carry

    lax.fori_loop(0, nblk, row_pool, 0, unroll=2)


def _pool_kernel(u_ref, pw_ref, ps_ref, o_ref, *scratch, rows, t):
    g = pl.program_id(1)
    for gi, w in enumerate(POOL_WINDOWS):
        @pl.when(g == gi)
        def _(w=w):
            if rows is None:
                _pool_seq_body(u_ref, pw_ref, ps_ref, o_ref, w=w, t=t)
            else:
                _pool_grid_body(u_ref, pw_ref, ps_ref, o_ref, scratch[0], w=w, rows=rows)


def _pool_mixer(u, pool_w, pool_scale, rows):
    b, t, _ = u.shape
    if rows is None:
        per_step = math.gcd(b, POOL_SEQ_PER_STEP)
        scratch = []
        blk = pl.BlockSpec((per_step, t, POOL_GROUP_DIM), lambda i, g: (i, 0, g))
    else:
        per_step = 1
        scratch = [pltpu.VMEM(((rows + 2 * POOL_PAD_ROWS) * GRID_W, POOL_GROUP_DIM), F32)]
        blk = pl.BlockSpec((None, t, POOL_GROUP_DIM), lambda i, g: (i, 0, g))
    return pl.pallas_call(
        functools.partial(_pool_kernel, rows=rows, t=t),
        grid=(b // per_step, POOL_GROUPS),
        in_specs=[
            blk,
            pl.BlockSpec((None, POOL_GROUP_DIM, POOL_GROUP_DIM), lambda i, g: (g, 0, 0)),
            pl.BlockSpec((1, POOL_GROUP_DIM), lambda i, g: (0, g)),
        ],
        out_specs=blk,
        out_shape=jax.ShapeDtypeStruct((b, t, POOL_DIM), F32),
        scratch_shapes=scratch,
        compiler_params=_params(("parallel", "parallel")),
        name="pool_seq" if rows is None else "pool_grid",
    )(u, pool_w, pool_scale)


def _ssd_chunk_terms(dtc, alog):
    n = SSD_CHUNK
    da = dtc * (-jnp.exp(alog) * LOG2E)
    row = lax.broadcasted_iota(jnp.int32, da.shape, 0)
    cs = da
    rcs = da
    k = 1
    while k < n:
        cs = cs + jnp.where(row >= k, pltpu.roll(cs, k, 0), 0.0)
        rcs = rcs + jnp.where(row < n - k, pltpu.roll(rcs, n - k, 0), 0.0)
        k *= 2
    logdt = jnp.log(dtc) * LOG2E
    return dict(colf=cs - logdt, colb=rcs - logdt, cst=cs.T, rcst=rcs.T, dtt=dtc.T)


def _rows64(mat, h):
    return jnp.broadcast_to(mat[h:h + 1, :], (SSD_HEAD_DIM, mat.shape[1]))


def _pair_rows(mat, h0):
    return jnp.concatenate([_rows64(mat, h0), _rows64(mat, h0 + 1)], axis=0)


def _state_update(h_scr, g, xw, bc_g, dec_t, head_off):
    rows = slice(g * GROUP_DIM, (g + 1) * GROUP_DIM)
    decay = jnp.concatenate(
        [_rows64(dec_t, head_off + g * HEADS_PER_GROUP + h) for h in range(HEADS_PER_GROUP)], axis=0)
    h_scr[rows, :] = h_scr[rows, :] * decay + _dot(xw, bc_g)


def _ssd_fwd_chunk(ci, xt_ref, bc_ref, cc_ref, dt_ref, alog_ref, y1_ref, h_scr):
    n = SSD_CHUNK
    tok = slice(ci * n, (ci + 1) * n)
    tm = _ssd_chunk_terms(dt_ref[tok, :], alog_ref[...])
    cst, rcst, dtt = tm["cst"], tm["rcst"], tm["dtt"]
    tot = cst[:, n - 1:n]
    eoff_t = jnp.exp2(cst)
    w_t = dtt * jnp.exp2(tot - cst)
    dec_t = jnp.exp2(jnp.broadcast_to(tot, (LANES, SSD_STATE)))
    si = lax.broadcasted_iota(jnp.int32, (n, n), 0)
    li = lax.broadcasted_iota(jnp.int32, (n, n), 1)
    src_before = si <= li
    src_after = si >= li
    zeros64 = jnp.zeros((SSD_HEAD_DIM, n), BF16)
    yield
    for g in range(SSD_GROUPS):
        bc_g = bc_ref[tok, g * SSD_STATE:(g + 1) * SSD_STATE]
        cc_g = cc_ref[tok, g * SSD_STATE:(g + 1) * SSD_STATE]
        sc_t = _dot_nt(bc_g, cc_g)
        y_off = _dot_nt(h_scr[g * GROUP_DIM:(g + 1) * GROUP_DIM, :].astype(BF16), cc_g)
        xw = []
        for j in range(HEADS_PER_GROUP // 2):
            h0 = g * HEADS_PER_GROUP + 2 * j
            mts = []
            for h in (h0, h0 + 1):
                seg_f = cst[h:h + 1, :] - tm["colf"][:, h:h + 1]
                hb = SSD_HEADS + h
                seg_b = rcst[hb:hb + 1, :] - tm["colb"][:, hb:hb + 1]
                dec = (jnp.exp2(jnp.where(src_before, seg_f, -jnp.inf))
                       + jnp.exp2(jnp.where(src_after, seg_b, -jnp.inf)))
                mts.append((sc_t * dec).astype(BF16))
            r0 = h0 * SSD_HEAD_DIM
            xp = xt_ref[ci, r0:r0 + 2 * SSD_HEAD_DIM, :]
            lhs = jnp.concatenate([jnp.concatenate([xp[:SSD_HEAD_DIM], zeros64], axis=0),
                                   jnp.concatenate([zeros64, xp[SSD_HEAD_DIM:]], axis=0)], axis=1)
            y_diag = _dot(lhs, jnp.concatenate(mts, axis=0))
            yo = y_off[2 * j * SSD_HEAD_DIM:(2 * j + 2) * SSD_HEAD_DIM, :]
            y1_ref[ci, r0:r0 + 2 * SSD_HEAD_DIM, :] = (y_diag + yo * _pair_rows(eoff_t, h0)).astype(y1_ref.dtype)
            xw.append((xp.astype(F32) * _pair_rows(w_t, h0)).astype(BF16))
            yield
        _state_update(h_scr, g, jnp.concatenate(xw, axis=0), bc_g, dec_t, 0)
        yield


def _ssd_bwd_chunk(ci, xt_ref, bc_ref, cc_ref, dt_ref, alog_ref, dsk, y1_ref, y_scr, h_scr):
    n = SSD_CHUNK
    tok = slice(ci * n, (ci + 1) * n)
    tm = _ssd_chunk_terms(dt_ref[tok, :], alog_ref[...])
    rcst, dtt = tm["rcst"], tm["dtt"]
    tot = rcst[:, 0:1]
    eoff_t = jnp.exp2(rcst)
    w_t = dtt * jnp.exp2(tot - rcst)
    dec_t = jnp.exp2(jnp.broadcast_to(tot, (LANES, SSD_STATE)))
    for g in range(SSD_GROUPS):
        bc_g = bc_ref[tok, g * SSD_STATE:(g + 1) * SSD_STATE]
        cc_g = cc_ref[tok, g * SSD_STATE:(g + 1) * SSD_STATE]
        y_off = _dot_nt(h_scr[g * GROUP_DIM:(g + 1) * GROUP_DIM, :].astype(BF16), cc_g)
        xw, yt = [], []
        for j in range(HEADS_PER_GROUP // 2):
            h0 = g * HEADS_PER_GROUP + 2 * j
            r0 = h0 * SSD_HEAD_DIM
            xp = xt_ref[ci, r0:r0 + 2 * SSD_HEAD_DIM, :].astype(F32)
            yo = y_off[2 * j * SSD_HEAD_DIM:(2 * j + 2) * SSD_HEAD_DIM, :]
            yt.append(y1_ref[ci, r0:r0 + 2 * SSD_HEAD_DIM, :].astype(F32)
                      + yo * _pair_rows(eoff_t, SSD_HEADS + h0) + _pair_rows(dsk, h0) * xp)
            xw.append((xp * _pair_rows(w_t, SSD_HEADS + h0)).astype(BF16))
        y_scr[tok, g * GROUP_DIM:(g + 1) * GROUP_DIM] = jnp.concatenate(yt, axis=0).T
        _state_update(h_scr, g, jnp.concatenate(xw, axis=0), bc_g, dec_t, SSD_HEADS)


def _project_and_conv(j, nblk, x_ref, xp_ref, xn_ref, mod_ref, g_ref, wp_ref, wz_ref, wx_ref, wd_ref, wg_ref,
                      dtb_ref, cw_ref, cb_ref, up_ref, z_ref, gt_ref, dt_ref, xt_ref, bc_ref, cc_ref,
                      hm_scr, ext_refs):
    m = mod_ref[...]
    g = g_ref[...]

    def modulated(x_rows, keep=True):
        hm = _rmsnorm(x_rows, g) * (1.0 + m[1:2]) + m[0:1]
        return jnp.where(keep, hm, 0.0).astype(BF16)

    tile = slice(CONV_HALO, CONV_HALO + TB_MIX)
    hm_scr[0:CONV_HALO, :] = modulated(xp_ref[...], j > 0)
    hm_scr[tile, :] = modulated(x_ref[...])
    hm_scr[CONV_HALO + TB_MIX:2 * CONV_HALO + TB_MIX, :] = modulated(xn_ref[...], j < nblk - 1)
    yield
    left = SSD_CONV // 2
    ct = CONV_TILE
    n_ct = CONV_DIM // ct
    n_ext = TB_MIX + 2 * CONV_HALO
    conv_base = SUBLANES + CONV_HALO

    def project_xbc(c):
        xbc = _dot(hm_scr[...], wx_ref[:, c * ct:(c + 1) * ct])
        for k in range(SSD_CONV):
            ext_refs[c][k, SUBLANES + left - k:SUBLANES + left - k + n_ext, :] = xbc

    def conv(c):
        cols = slice(c * ct, (c + 1) * ct)
        acc = cb_ref[:, cols] + ext_refs[c][0, conv_base:conv_base + TB_MIX, :] * cw_ref[0:1, cols]
        for k in range(1, SSD_CONV):
            acc = acc + ext_refs[c][k, conv_base:conv_base + TB_MIX, :] * cw_ref[k:k + 1, cols]
        y = jax.nn.silu(acc)
        if c < D_INNER // ct:
            for ci in range(CHUNKS_PER_TILE):
                xt_ref[ci, cols, :] = y[ci * SSD_CHUNK:(ci + 1) * SSD_CHUNK, :].T.astype(BF16)
        elif c == D_INNER // ct:
            bc_ref[...] = y.astype(BF16)
        else:
            cc_ref[...] = y.astype(BF16)

    project_xbc(0)
    yield
    for c in range(n_ct):
        if c + 1 < n_ct:
            project_xbc(c + 1)
        if c < D_INNER // ct:
            zc = slice(c * ct, (c + 1) * ct)
            z_ref[:, zc] = _dot(hm_scr[tile, :], wz_ref[:, zc]).astype(z_ref.dtype)
        conv(c)
        yield
    dt_ref[...] = jax.nn.softplus(_dot(hm_scr[tile, :], wd_ref[...]) + dtb_ref[...])
    yield
    for c in range(2 * D_MODEL // PROJ_COLS):
        gc = slice(c * PROJ_COLS, (c + 1) * PROJ_COLS)
        gt_ref[:, gc] = _dot(hm_scr[tile, :], wg_ref[:, gc]).astype(gt_ref.dtype)
        yield
    for c in range(POOL_DIM // PROJ_COLS):
        pc = slice(c * PROJ_COLS, (c + 1) * PROJ_COLS)
        up_ref[:, pc] = _dot(hm_scr[tile, :], wp_ref[:, pc])
        yield


def _mixf_kernel(*refs, zero_init, emit_state, nblk, ntiles):
    (x_ref, xp_ref, xn_ref, mod_ref, g_ref, wp_ref, wz_ref, wx_ref, wd_ref, wg_ref, dtb_ref,
     cw_ref, cb_ref, alog_ref) = refs[:14]
    refs = refs[14:]
    if not zero_init:
        h0_ref, refs = refs[0], refs[1:]
    up_ref, z_ref, gt_ref, dt_ref, xt_ref, bc_ref, cc_ref, y1_ref = refs[:8]
    refs = refs[8:]
    if emit_state:
        hf_ref, refs = refs[0], refs[1:]
    hm_scr, xt_cur, bc_cur, cc_cur, dt_cur, h_scr = refs[:6]
    ext_refs = refs[6:]
    s = pl.program_id(0)
    ja = lax.rem(jnp.minimum(s, ntiles - 1), nblk)
    jb = lax.rem(jnp.maximum(s - 1, 0), nblk)

    @pl.when(s == 0)
    def _():
        xt_cur[...] = jnp.zeros_like(xt_cur)
        bc_cur[...] = jnp.zeros_like(bc_cur)
        cc_cur[...] = jnp.zeros_like(cc_cur)
        dt_cur[...] = jnp.ones_like(dt_cur)

    @pl.when(jb == 0)
    def _():
        h_scr[...] = jnp.zeros_like(h_scr) if zero_init else h0_ref[...]

    proj = _project_and_conv(ja, nblk, x_ref, xp_ref, xn_ref, mod_ref, g_ref, wp_ref, wz_ref, wx_ref, wd_ref,
                             wg_ref, dtb_ref, cw_ref, cb_ref, up_ref, z_ref, gt_ref, dt_ref, xt_ref, bc_ref,
                             cc_ref, hm_scr, ext_refs)
    scan = (None for ci in range(CHUNKS_PER_TILE)
            for _ in _ssd_fwd_chunk(ci, xt_cur, bc_cur, cc_cur, dt_cur, alog_ref, y1_ref, h_scr))
    for n_scan in SCAN_SCHEDULE:
        next(proj, None)
        for _ in range(n_scan):
            next(scan, None)
    for _ in proj:
        pass
    for _ in scan:
        pass

    if emit_state:
        @pl.when(jb == nblk - 1)
        def _():
            hf_ref[...] = h_scr[...]

    xt_cur[...] = xt_ref[...]
    bc_cur[...] = bc_ref[...]
    cc_cur[...] = cc_ref[...]
    dt_cur[...] = dt_ref[...]


def _mix_fwd(x, mod, g, lw, h0_f, layer, emit_state):
    b, t, _ = x.shape
    nblk = t // TB_MIX
    ntiles = b * nblk
    nc = t // SSD_CHUNK
    per = TB_MIX // CONV_HALO
    zero_init = h0_f is None
    mod_idx = (lambda i: i) if mod.shape[0] == b else (lambda i: 0)
    ta = lambda s: jnp.minimum(s, ntiles - 1)
    tb = lambda s: jnp.maximum(s - 1, 0)
    tok = lambda n: pl.BlockSpec((None, TB_MIX, n), lambda s: (ta(s) // nblk, ta(s) % nblk, 0))
    chunked = lambda tile: pl.BlockSpec((None, CHUNKS_PER_TILE, D_INNER, SSD_CHUNK),
                                        lambda s: (tile(s) // nblk, tile(s) % nblk, 0, 0))
    in_specs = [
        tok(D_MODEL),
        pl.BlockSpec((None, CONV_HALO, D_MODEL),
                     lambda s: (ta(s) // nblk, jnp.maximum((ta(s) % nblk) * per - 1, 0), 0)),
        pl.BlockSpec((None, CONV_HALO, D_MODEL),
                     lambda s: (ta(s) // nblk, jnp.minimum((ta(s) % nblk + 1) * per, t // CONV_HALO - 1), 0)),
        pl.BlockSpec((None, 3, D_MODEL), lambda s: (mod_idx(ta(s) // nblk), 0, 0)),
        _resident((1, D_MODEL)),
        _resident((D_MODEL, POOL_DIM)),
        _resident((D_MODEL, D_INNER)),
        _resident((D_MODEL, CONV_DIM)),
        _resident((D_MODEL, LANES)),
        _resident((D_MODEL, 2 * D_MODEL)),
        _resident((1, LANES)),
        _resident((SSD_CONV, CONV_DIM)),
        _resident((1, CONV_DIM)),
        _resident((1, LANES)),
    ]
    args = [x, x, x, mod, g, lw["w_pool"], lw["w_z"], lw["w_xbc"], lw["w_dt"], lw["w_gates"], lw["dt_bias"],
            lw["conv_w"], lw["conv_b"], lw["a_log"]]
    if not zero_init:
        in_specs.append(pl.BlockSpec((None, None, D_INNER, SSD_STATE), lambda s: (tb(s) // nblk, layer, 0, 0)))
        args.append(h0_f)
    out_specs = [tok(POOL_DIM), tok(D_INNER), tok(2 * D_MODEL), tok(LANES), chunked(ta), tok(BC_DIM), tok(BC_DIM),
                 chunked(tb)]
    out_shape = [
        jax.ShapeDtypeStruct((b, t, POOL_DIM), F32),
        jax.ShapeDtypeStruct((b, t, D_INNER), BF16),
        jax.ShapeDtypeStruct((b, t, 2 * D_MODEL), BF16),
        jax.ShapeDtypeStruct((b, t, LANES), F32),
        jax.ShapeDtypeStruct((b, nc, D_INNER, SSD_CHUNK), BF16),
        jax.ShapeDtypeStruct((b, t, BC_DIM), BF16),
        jax.ShapeDtypeStruct((b, t, BC_DIM), BF16),
        jax.ShapeDtypeStruct((b, nc, D_INNER, SSD_CHUNK), BF16),
    ]
    if emit_state:
        out_specs.append(pl.BlockSpec((None, D_INNER, SSD_STATE), lambda s: (tb(s) // nblk, 0, 0)))
        out_shape.append(jax.ShapeDtypeStruct((b, D_INNER, SSD_STATE), F32))
    scratch = [
        pltpu.VMEM((TB_MIX + 2 * CONV_HALO, D_MODEL), BF16),
        pltpu.VMEM((CHUNKS_PER_TILE, D_INNER, SSD_CHUNK), BF16),
        pltpu.VMEM((TB_MIX, BC_DIM), BF16),
        pltpu.VMEM((TB_MIX, BC_DIM), BF16),
        pltpu.VMEM((TB_MIX, LANES), F32),
        pltpu.VMEM((D_INNER, SSD_STATE), F32),
    ] + [pltpu.VMEM((SSD_CONV, TB_MIX + 2 * CONV_HALO + 2 * SUBLANES, CONV_TILE), F32)
         for _ in range(CONV_DIM // CONV_TILE)]
    return pl.pallas_call(
        functools.partial(_mixf_kernel, zero_init=zero_init, emit_state=emit_state, nblk=nblk, ntiles=ntiles),
        grid=(ntiles + 1,), in_specs=in_specs, out_specs=out_specs, out_shape=out_shape,
        scratch_shapes=scratch,
        compiler_params=_params(("arbitrary",)),
        name="mix_fwd",
    )(*args)


def _mixb_kernel(*refs, zero_init, emit_state, nblk):
    (x_ref, xt_ref, bc_ref, cc_ref, dt_ref, y1_ref, z_ref, gt_ref, yp_ref, mod_ref, alog_ref, dsk_ref,
     ng_ref, wbp_ref, wbs_ref, wo_ref) = refs[:16]
    refs = refs[16:]
    if not zero_init:
        h0_ref, refs = refs[0], refs[1:]
    o_ref, refs = refs[0], refs[1:]
    if emit_state:
        hb_ref, refs = refs[0], refs[1:]
    y_scr, h_scr = refs
    j = pl.program_id(1)

    @pl.when(j == 0)
    def _():
        h_scr[...] = jnp.zeros_like(h_scr) if zero_init else h0_ref[...]

    dsk = dsk_ref[...]
    for ci in reversed(range(CHUNKS_PER_TILE)):
        _ssd_bwd_chunk(ci, xt_ref, bc_ref, cc_ref, dt_ref, alog_ref, dsk, y1_ref, y_scr, h_scr)

    if emit_state:
        @pl.when(j == nblk - 1)
        def _():
            hb_ref[...] = h_scr[...]

    y = _rmsnorm(y_scr[...] * jax.nn.silu(z_ref[...].astype(F32)), ng_ref[...])
    gates = jax.nn.sigmoid(gt_ref[...].astype(F32))
    merged = (gates[:, :D_MODEL] * _dot(yp_ref[...].astype(BF16), wbp_ref[...])
              + gates[:, D_MODEL:] * _dot(y.astype(BF16), wbs_ref[...]))
    o_ref[...] = x_ref[...] + mod_ref[2:3, :] * _dot(merged.astype(BF16), wo_ref[...])


def _mix_bwd(x, fwd, y_pool, mod, lw, h0_b, layer, emit_state):
    b, t, _ = x.shape
    nblk = t // TB_MIX
    zero_init = h0_b is None
    z, gates, dt, xt, bcv, ccv, y1 = fwd
    mod_idx = (lambda i: i) if mod.shape[0] == b else (lambda i: 0)
    tok = lambda n: pl.BlockSpec((None, TB_MIX, n), lambda i, j: (i, nblk - 1 - j, 0))
    chunked = pl.BlockSpec((None, CHUNKS_PER_TILE, D_INNER, SSD_CHUNK), lambda i, j: (i, nblk - 1 - j, 0, 0))
    in_specs = [
        tok(D_MODEL), chunked, tok(BC_DIM), tok(BC_DIM), tok(LANES), chunked, tok(D_INNER), tok(2 * D_MODEL),
        tok(POOL_DIM),
        pl.BlockSpec((None, 3, D_MODEL), lambda i, j: (mod_idx(i), 0, 0)),
        _resident((1, LANES)),
        _resident((LANES, LANES)),
        _resident((1, D_INNER)),
        _resident((POOL_DIM, D_MODEL)),
        _resident((D_INNER, D_MODEL)),
        _resident((D_MODEL, D_MODEL)),
    ]
    args = [x, xt, bcv, ccv, dt, y1, z, gates, y_pool, mod, lw["a_log"], lw["d_skip"], lw["ssd_norm"],
            lw["w_branch_pool"], lw["w_branch_ssd"], lw["w_out"]]
    if not zero_init:
        in_specs.append(pl.BlockSpec((None, None, D_INNER, SSD_STATE), lambda i, j: (i, layer, 0, 0)))
        args.append(h0_b)
    out_specs = [tok(D_MODEL)]
    out_shape = [jax.ShapeDtypeStruct((b, t, D_MODEL), F32)]
    if emit_state:
        out_specs.append(pl.BlockSpec((None, D_INNER, SSD_STATE), lambda i, j: (i, 0, 0)))
        out_shape.append(jax.ShapeDtypeStruct((b, D_INNER, SSD_STATE), F32))
    return pl.pallas_call(
        functools.partial(_mixb_kernel, zero_init=zero_init, emit_state=emit_state, nblk=nblk),
        grid=(b, nblk), in_specs=in_specs, out_specs=out_specs, out_shape=out_shape,
        scratch_shapes=[pltpu.VMEM((TB_MIX, D_INNER), F32), pltpu.VMEM((D_INNER, SSD_STATE), F32)],
        compiler_params=_params(("parallel", "arbitrary")),
        name="mix_bwd",
    )(*args)


def _layer(x, mod, rows, h0_f, h0_b, layer, lw, emit_state, final_g):
    ng = lw["norm_g"]
    x = _ffn(x, mod[:, 0:3], ng[0:1], lw["ffn1_w13"], lw["ffn1_w2"])
    mod2 = mod[:, 3:6]
    fwd = _mix_fwd(x, mod2, ng[1:2], lw, h0_f, layer, emit_state)
    hf = fwd[8] if emit_state else None
    y_pool = _pool_mixer(fwd[0], lw["pool_w"], lw["pool_scale"], rows)
    res = _mix_bwd(x, fwd[1:8], y_pool, mod2, lw, h0_b, layer, emit_state)
    x, hb = (res[0], res[1]) if emit_state else (res[0], None)
    x = _ffn(x, mod[:, 6:9], ng[2:3], lw["ffn2_w13"], lw["ffn2_w2"], final_g)
    return x, hf, hb


def _prep_layer(l, p):
    s1 = POOL_DIM
    s2 = s1 + D_INNER
    s3 = s2 + CONV_DIM
    s4 = s3 + 2 * SSD_HEADS
    w_in = p["w_in"][l]
    pad_lanes = lambda v: jnp.pad(v.reshape(1, -1), ((0, 0), (0, LANES - v.size)))
    return dict(
        norm_g=p["norm_g"][l],
        ffn1_w13=p["ffn1_w13"][l].astype(BF16), ffn1_w2=p["ffn1_w2"][l].astype(BF16),
        ffn2_w13=p["ffn2_w13"][l].astype(BF16), ffn2_w2=p["ffn2_w2"][l].astype(BF16),
        w_pool=w_in[:, :s1].astype(BF16), w_z=w_in[:, s1:s2].astype(BF16), w_xbc=w_in[:, s2:s3].astype(BF16),
        w_dt=jnp.pad(w_in[:, s3:s4], ((0, 0), (0, LANES - 2 * SSD_HEADS))).astype(BF16),
        w_gates=w_in[:, s4:].astype(BF16),
        dt_bias=pad_lanes(p["dt_bias"][l]),
        a_log=pad_lanes(p["a_log"][l]),
        d_skip=jnp.broadcast_to(pad_lanes(p["d_skip"][l]).reshape(LANES, 1), (LANES, LANES)),
        pool_w=p["pool_w"][l].astype(BF16), pool_scale=p["pool_scale"][l].reshape(1, POOL_DIM),
        conv_w=p["conv_w"][l], conv_b=p["conv_b"][l].reshape(1, CONV_DIM),
        ssd_norm=p["ssd_norm"][l].reshape(1, D_INNER),
        w_branch_pool=p["w_branch_pool"][l].astype(BF16), w_branch_ssd=p["w_branch_ssd"][l].astype(BF16),
        w_out=p["w_out"][l].astype(BF16),
    )


def kernel(x_prompt, x_sample, state_ssd_fwd, state_ssd_bwd, c, c_ctx, ada_w, ada_b, norm_g, ffn1_w13, ffn1_w2, ffn2_w13, ffn2_w2, w_in, pool_w, pool_scale, conv_w, conv_b, a_log, dt_bias, d_skip, ssd_norm, w_branch_pool, w_branch_ssd, w_out, final_norm):
    p = dict(norm_g=norm_g, ffn1_w13=ffn1_w13, ffn1_w2=ffn1_w2, ffn2_w13=ffn2_w13, ffn2_w2=ffn2_w2,
             w_in=w_in, pool_w=pool_w, pool_scale=pool_scale, conv_w=conv_w, conv_b=conv_b, a_log=a_log,
             dt_bias=dt_bias, d_skip=d_skip, ssd_norm=ssd_norm, w_branch_pool=w_branch_pool,
             w_branch_ssd=w_branch_ssd, w_out=w_out)
    nb = c.shape[0]
    rows = x_sample.shape[1] // GRID_W
    cond = jnp.concatenate([c_ctx[None, :], c], axis=0)
    cond = jnp.pad(cond, ((0, -(nb + 1) % SUBLANES), (0, 0)))
    mod = _adaln(cond, ada_w, ada_b).reshape(DEPTH, cond.shape[0], N_MOD, D_MODEL)
    fg = final_norm.reshape(1, D_MODEL)
    sd = state_ssd_fwd.shape
    h0_f = state_ssd_fwd.reshape(sd[0], DEPTH, D_INNER, SSD_STATE)
    h0_b = state_ssd_bwd.reshape(sd[0], DEPTH, D_INNER, SSD_STATE)
    xc, xl = x_prompt, x_sample
    new_f, new_b = [], []
    for l in range(DEPTH):
        lw = _prep_layer(l, p)
        last = fg if l == DEPTH - 1 else None
        xc, hf, hb = _layer(xc, mod[l, 0:1], None, None, None, l, lw, True, last)
        new_f.append(hf.reshape(-1, SSD_HEADS, SSD_HEAD_DIM, SSD_STATE))
        new_b.append(hb.reshape(-1, SSD_HEADS, SSD_HEAD_DIM, SSD_STATE))
        xl, _, _ = _layer(xl, mod[l, 1:1 + nb], rows, h0_f, h0_b, l, lw, False, last)
    return (xc, xl, jnp.stack(new_f, axis=1), jnp.stack(new_b, axis=1))
```

```python
import functools
import math

import jax
import jax.numpy as jnp
from jax import lax
from jax.experimental import pallas as pl
from jax.experimental.pallas import tpu as pltpu

F32 = jnp.float32
BF16 = jnp.bfloat16

D_MODEL = 1024
DEPTH = 2
GRID_W = 64
POOL_WINDOWS = (2, 4, 8, 16)
POOL_GROUPS = len(POOL_WINDOWS)
POOL_DIM = D_MODEL
POOL_GROUP_DIM = POOL_DIM // POOL_GROUPS
D_INNER = 2 * D_MODEL
SSD_HEAD_DIM = 64
SSD_HEADS = D_INNER // SSD_HEAD_DIM
SSD_GROUPS = 4
HEADS_PER_GROUP = SSD_HEADS // SSD_GROUPS
GROUP_DIM = HEADS_PER_GROUP * SSD_HEAD_DIM
SSD_STATE = 128
SSD_CONV = 4
SSD_CHUNK = 128
BC_DIM = SSD_GROUPS * SSD_STATE
CONV_DIM = D_INNER + 2 * BC_DIM
D_FF = 2816
N_MOD = 9
EPS = 1e-6
LOG2E = math.log2(math.e)

LANES = 128
SUBLANES = 8
VMEM_LIMIT = 56 * 1024 * 1024

TM_FFN = 512
FF_CHUNK = 256
TB_MIX = 256
CHUNKS_PER_TILE = TB_MIX // SSD_CHUNK
CONV_TILE = BC_DIM
CONV_HALO = 16
POOL_BLOCK = 256
POOL_PAD_ROWS = 8
POOL_SEQ_PER_STEP = 4
POOL_UNROLL = 8
ADA_TN = 2304
PROJ_COLS = 256
SCAN_SCHEDULE = (0,) + (1,) * 7 + (3,) * 9 + (2,) * 4


def _dot(a, b):
    return jnp.dot(a, b, preferred_element_type=F32)


def _dot_nt(a, b):
    return lax.dot_general(a, b, (((1,), (1,)), ((), ())), preferred_element_type=F32)


def _rmsnorm(x, g):
    return x * lax.rsqrt(jnp.mean(x * x, axis=-1, keepdims=True) + EPS) * g


def _params(sem):
    return pltpu.CompilerParams(dimension_semantics=sem, vmem_limit_bytes=VMEM_LIMIT)


def _resident(shape):
    nd = len(shape)
    return pl.BlockSpec(shape, lambda *_: (0,) * nd, pipeline_mode=pl.Buffered(1))


def _ada_kernel(cond_ref, w_ref, b_ref, o_ref):
    s = jax.nn.silu(cond_ref[...]).astype(BF16)
    o_ref[...] = _dot(s, w_ref[...].astype(BF16)) + b_ref[...]


def _adaln(cond, ada_w, ada_b):
    n = cond.shape[0]
    ncol = N_MOD * D_MODEL
    return pl.pallas_call(
        _ada_kernel,
        grid=(DEPTH, ncol // ADA_TN),
        in_specs=[
            pl.BlockSpec((n, D_MODEL), lambda l, j: (0, 0)),
            pl.BlockSpec((None, D_MODEL, ADA_TN), lambda l, j: (l, 0, j)),
            pl.BlockSpec((None, 1, ADA_TN), lambda l, j: (l, 0, j)),
        ],
        out_specs=pl.BlockSpec((None, n, ADA_TN), lambda l, j: (l, 0, j)),
        out_shape=jax.ShapeDtypeStruct((DEPTH, n, ncol), F32),
        compiler_params=_params(("parallel", "parallel")),
        name="adaln",
    )(cond, ada_w, ada_b.reshape(DEPTH, 1, ncol))


def _ffn_kernel(x_ref, mod_ref, g_ref, w13_ref, w2_ref, *rest, final):
    if final:
        fg_ref, o_ref, a_scr = rest
    else:
        o_ref, a_scr = rest
    x = x_ref[...]
    m = mod_ref[...]
    h = (_rmsnorm(x, g_ref[...]) * (1.0 + m[1:2]) + m[0:1]).astype(BF16)
    for j in range(D_FF // FF_CHUNK):
        lo, hi = j * FF_CHUNK, (j + 1) * FF_CHUNK
        gate = _dot(h, w13_ref[:, lo:hi])
        up = _dot(h, w13_ref[:, D_FF + lo:D_FF + hi])
        a_scr[:, lo:hi] = (jax.nn.silu(gate) * up).astype(BF16)
    y = _dot(a_scr[...], w2_ref[...])
    out = x + 0.5 * m[2:3] * y
    if final:
        out = _rmsnorm(out, fg_ref[...])
    o_ref[...] = out


def _ffn(x, mod, g, w13, w2, final_g=None):
    b, t, _ = x.shape
    nt = b * t
    tpm = nt // mod.shape[0]
    final = final_g is not None
    in_specs = [
        pl.BlockSpec((TM_FFN, D_MODEL), lambda i: (i, 0)),
        pl.BlockSpec((None, 3, D_MODEL), lambda i: (i * TM_FFN // tpm, 0, 0)),
        _resident((1, D_MODEL)),
        _resident((D_MODEL, 2 * D_FF)),
        _resident((D_FF, D_MODEL)),
    ]
    args = [x.reshape(nt, D_MODEL), mod, g, w13, w2]
    if final:
        in_specs.append(_resident((1, D_MODEL)))
        args.append(final_g)
    out = pl.pallas_call(
        functools.partial(_ffn_kernel, final=final),
        grid=(nt // TM_FFN,),
        in_specs=in_specs,
        out_specs=pl.BlockSpec((TM_FFN, D_MODEL), lambda i: (i, 0)),
        out_shape=jax.ShapeDtypeStruct((nt, D_MODEL), F32),
        scratch_shapes=[pltpu.VMEM((TM_FFN, D_FF), BF16)],
        compiler_params=_params(("parallel",)),
        name="ffn_final" if final else "ffn",
    )(*args)
    return out.reshape(b, t, D_MODEL)


def _window_count(idx, w, n):
    lo = jnp.maximum(idx - w // 2, 0)
    hi = jnp.minimum(idx - w // 2 + w, n)
    return (hi - lo).astype(F32)


def _band_sum(band, u):
    hi = u.astype(BF16)
    lo = (u - hi.astype(F32)).astype(BF16)
    return _dot(band, hi) + _dot(band, lo)


def _pool_finish(tot, cnt, u, pw_ref, ps_ref):
    d = (tot / cnt - u).astype(BF16)
    return _dot(d, pw_ref[...]) * ps_ref[...]


def _pool_seq_body(u_ref, pw_ref, ps_ref, o_ref, *, w, t):
    ti = lax.broadcasted_iota(jnp.int32, (t, t), 0)
    si = lax.broadcasted_iota(jnp.int32, (t, t), 1)
    band = jnp.where((si >= ti - w // 2) & (si < ti - w // 2 + w), 1.0, 0.0).astype(BF16)
    cnt = _window_count(lax.broadcasted_iota(jnp.int32, (t, POOL_GROUP_DIM), 0), w, t)
    for s in range(u_ref.shape[0]):
        u = u_ref[s]
        o_ref[s] = _pool_finish(_band_sum(band, u), cnt, u, pw_ref, ps_ref)


def _pool_grid_body(u_ref, pw_ref, ps_ref, o_ref, cp_ref, *, w, rows):
    nblk = rows * GRID_W // POOL_BLOCK
    unroll = math.gcd(nblk, POOL_UNROLL)
    pad = POOL_PAD_ROWS * GRID_W
    ti = lax.broadcasted_iota(jnp.int32, (POOL_BLOCK, POOL_BLOCK), 0)
    si = lax.broadcasted_iota(jnp.int32, (POOL_BLOCK, POOL_BLOCK), 1)
    tc = ti & (GRID_W - 1)
    sc = si & (GRID_W - 1)
    same_row = (ti >> 6) == (si >> 6)
    band = jnp.where(same_row & (sc >= tc - w // 2) & (sc < tc - w // 2 + w), 1.0, 0.0).astype(BF16)
    cp_ref[0:pad + GRID_W, :] = jnp.zeros((pad + GRID_W, POOL_GROUP_DIM), F32)
    rows_per_block = POOL_BLOCK // GRID_W

    def col_pool(i, acc):
        off = pl.multiple_of(i * POOL_BLOCK, POOL_BLOCK)
        cp = _band_sum(band, u_ref[pl.ds(off, POOL_BLOCK), :])
        prefix = []
        for k in range(rows_per_block):
            acc = acc + cp[k * GRID_W:(k + 1) * GRID_W, :]
            prefix.append(acc)
        cp_ref[pl.ds(pad + GRID_W + off, POOL_BLOCK), :] = jnp.concatenate(prefix, axis=0)
        return acc

    total = lax.fori_loop(0, nblk, col_pool, jnp.zeros((GRID_W, POOL_GROUP_DIM), F32), unroll=unroll)
    for k in range(1, POOL_PAD_ROWS):
        cp_ref[pad + (rows + k) * GRID_W:pad + (rows + k + 1) * GRID_W, :] = total

    tt = lax.broadcasted_iota(jnp.int32, (POOL_BLOCK, POOL_GROUP_DIM), 0)
    ccnt = _window_count(tt & (GRID_W - 1), w, GRID_W)

    def row_pool(i, carry):
        off = pl.multiple_of(i * POOL_BLOCK, POOL_BLOCK)
        tot = (cp_ref[pl.ds(pad + off + (w // 2) * GRID_W, POOL_BLOCK), :]
               - cp_ref[pl.ds(pad + off - (w // 2) * GRID_W, POOL_BLOCK), :])
        r = i * rows_per_block + (tt >> 6)
        cnt = _window_count(r, w, rows) * ccnt
        o_ref[pl.ds(off, POOL_BLOCK), :] = _pool_finish(tot, cnt, u_ref[pl.ds(off, POOL_BLOCK), :], pw_ref, ps_ref)
        return carry

    lax.fori_loop(0, nblk, row_pool, 0, unroll=unroll)


def _pool_kernel(u_ref, pw_ref, ps_ref, o_ref, *scratch, rows, t):
    g = pl.program_id(1)
    for gi, w in enumerate(POOL_WINDOWS):
        @pl.when(g == gi)
        def _(w=w):
            if rows is None:
                _pool_seq_body(u_ref, pw_ref, ps_ref, o_ref, w=w, t=t)
            else:
                _pool_grid_body(u_ref, pw_ref, ps_ref, o_ref, scratch[0], w=w, rows=rows)


def _pool_mixer(u, pool_w, pool_scale, rows):
    b, t, _ = u.shape
    if rows is None:
        per_step = math.gcd(b, POOL_SEQ_PER_STEP)
        scratch = []
        blk = pl.BlockSpec((per_step, t, POOL_GROUP_DIM), lambda i, g: (i, 0, g))
    else:
        per_step = 1
        scratch = [pltpu.VMEM(((rows + 2 * POOL_PAD_ROWS) * GRID_W, POOL_GROUP_DIM), F32)]
        blk = pl.BlockSpec((None, t, POOL_GROUP_DIM), lambda i, g: (i, 0, g))
    return pl.pallas_call(
        functools.partial(_pool_kernel, rows=rows, t=t),
        grid=(b // per_step, POOL_GROUPS),
        in_specs=[
            blk,
            pl.BlockSpec((None, POOL_GROUP_DIM, POOL_GROUP_DIM), lambda i, g: (g, 0, 0)),
            pl.BlockSpec((1, POOL_GROUP_DIM), lambda i, g: (0, g)),
        ],
        out_specs=blk,
        out_shape=jax.ShapeDtypeStruct((b, t, POOL_DIM), F32),
        scratch_shapes=scratch,
        compiler_params=_params(("parallel", "parallel")),
        name="pool_seq" if rows is None else "pool_grid",
    )(u, pool_w, pool_scale)


def _ssd_chunk_terms(dtc, alog):
    n = SSD_CHUNK
    da = dtc * (-jnp.exp(alog) * LOG2E)
    row = lax.broadcasted_iota(jnp.int32, da.shape, 0)
    cs = da
    rcs = da
    k = 1
    while k < n:
        cs = cs + jnp.where(row >= k, pltpu.roll(cs, k, 0), 0.0)
        rcs = rcs + jnp.where(row < n - k, pltpu.roll(rcs, n - k, 0), 0.0)
        k *= 2
    logdt = jnp.log(dtc) * LOG2E
    return dict(colf=cs - logdt, colb=rcs - logdt, cst=cs.T, rcst=rcs.T, dtt=dtc.T)


def _rows64(mat, h):
    return jnp.broadcast_to(mat[h:h + 1, :], (SSD_HEAD_DIM, mat.shape[1]))


def _pair_rows(mat, h0):
    return jnp.concatenate([_rows64(mat, h0), _rows64(mat, h0 + 1)], axis=0)


def _state_update(h_scr, g, xw, bc_g, dec_t, head_off):
    rows = slice(g * GROUP_DIM, (g + 1) * GROUP_DIM)
    decay = jnp.concatenate(
        [_rows64(dec_t, head_off + g * HEADS_PER_GROUP + h) for h in range(HEADS_PER_GROUP)], axis=0)
    h_scr[rows, :] = h_scr[rows, :] * decay + _dot(xw, bc_g)


def _ssd_fwd_chunk(ci, xt_ref, bc_ref, cc_ref, dt_ref, alog_ref, y1_ref, h_scr):
    n = SSD_CHUNK
    tok = slice(ci * n, (ci + 1) * n)
    tm = _ssd_chunk_terms(dt_ref[tok, :], alog_ref[...])
    cst, rcst, dtt = tm["cst"], tm["rcst"], tm["dtt"]
    tot = cst[:, n - 1:n]
    eoff_t = jnp.exp2(cst)
    w_t = dtt * jnp.exp2(tot - cst)
    dec_t = jnp.exp2(jnp.broadcast_to(tot, (LANES, SSD_STATE)))
    si = lax.broadcasted_iota(jnp.int32, (n, n), 0)
    li = lax.broadcasted_iota(jnp.int32, (n, n), 1)
    src_before = si <= li
    src_after = si >= li
    zeros64 = jnp.zeros((SSD_HEAD_DIM, n), BF16)
    yield
    for g in range(SSD_GROUPS):
        bc_g = bc_ref[tok, g * SSD_STATE:(g + 1) * SSD_STATE]
        cc_g = cc_ref[tok, g * SSD_STATE:(g + 1) * SSD_STATE]
        sc_t = _dot_nt(bc_g, cc_g)
        y_off = _dot_nt(h_scr[g * GROUP_DIM:(g + 1) * GROUP_DIM, :].astype(BF16), cc_g)
        xw = []
        for j in range(HEADS_PER_GROUP // 2):
            h0 = g * HEADS_PER_GROUP + 2 * j
            mts = []
            for h in (h0, h0 + 1):
                seg_f = cst[h:h + 1, :] - tm["colf"][:, h:h + 1]
                hb = SSD_HEADS + h
                seg_b = rcst[hb:hb + 1, :] - tm["colb"][:, hb:hb + 1]
                dec = (jnp.exp2(jnp.where(src_before, seg_f, -jnp.inf))
                       + jnp.exp2(jnp.where(src_after, seg_b, -jnp.inf)))
                mts.append((sc_t * dec).astype(BF16))
            r0 = h0 * SSD_HEAD_DIM
            xp = xt_ref[ci, r0:r0 + 2 * SSD_HEAD_DIM, :]
            lhs = jnp.concatenate([jnp.concatenate([xp[:SSD_HEAD_DIM], zeros64], axis=0),
                                   jnp.concatenate([zeros64, xp[SSD_HEAD_DIM:]], axis=0)], axis=1)
            y_diag = _dot(lhs, jnp.concatenate(mts, axis=0))
            yo = y_off[2 * j * SSD_HEAD_DIM:(2 * j + 2) * SSD_HEAD_DIM, :]
            y1_ref[ci, r0:r0 + 2 * SSD_HEAD_DIM, :] = (y_diag + yo * _pair_rows(eoff_t, h0)).astype(y1_ref.dtype)
            xw.append((xp.astype(F32) * _pair_rows(w_t, h0)).astype(BF16))
            yield
        _state_update(h_scr, g, jnp.concatenate(xw, axis=0), bc_g, dec_t, 0)
        yield


def _ssd_bwd_chunk(ci, xt_ref, bc_ref, cc_ref, dt_ref, alog_ref, dsk, y1_ref, y_scr, h_scr):
    n = SSD_CHUNK
    tok = slice(ci * n, (ci + 1) * n)
    tm = _ssd_chunk_terms(dt_ref[tok, :], alog_ref[...])
    rcst, dtt = tm["rcst"], tm["dtt"]
    tot = rcst[:, 0:1]
    eoff_t = jnp.exp2(rcst)
    w_t = dtt * jnp.exp2(tot - rcst)
    dec_t = jnp.exp2(jnp.broadcast_to(tot, (LANES, SSD_STATE)))
    for g in range(SSD_GROUPS):
        bc_g = bc_ref[tok, g * SSD_STATE:(g + 1) * SSD_STATE]
        cc_g = cc_ref[tok, g * SSD_STATE:(g + 1) * SSD_STATE]
        y_off = _dot_nt(h_scr[g * GROUP_DIM:(g + 1) * GROUP_DIM, :].astype(BF16), cc_g)
        xw, yt = [], []
        for j in range(HEADS_PER_GROUP // 2):
            h0 = g * HEADS_PER_GROUP + 2 * j
            r0 = h0 * SSD_HEAD_DIM
            xp = xt_ref[ci, r0:r0 + 2 * SSD_HEAD_DIM, :].astype(F32)
            yo = y_off[2 * j * SSD_HEAD_DIM:(2 * j + 2) * SSD_HEAD_DIM, :]
            yt.append(y1_ref[ci, r0:r0 + 2 * SSD_HEAD_DIM, :].astype(F32)
                      + yo * _pair_rows(eoff_t, SSD_HEADS + h0) + _pair_rows(dsk, h0) * xp)
            xw.append((xp * _pair_rows(w_t, SSD_HEADS + h0)).astype(BF16))
        y_scr[tok, g * GROUP_DIM:(g + 1) * GROUP_DIM] = jnp.concatenate(yt, axis=0).T
        _state_update(h_scr, g, jnp.concatenate(xw, axis=0), bc_g, dec_t, SSD_HEADS)


def _project_and_conv(j, nblk, x_ref, xp_ref, xn_ref, mod_ref, g_ref, wp_ref, wz_ref, wx_ref, wd_ref, wg_ref,
                      dtb_ref, cw_ref, cb_ref, up_ref, z_ref, gt_ref, dt_ref, xt_ref, bc_ref, cc_ref,
                      hm_scr, ext_refs):
    m = mod_ref[...]
    g = g_ref[...]

    def modulated(x_rows, keep=True):
        hm = _rmsnorm(x_rows, g) * (1.0 + m[1:2]) + m[0:1]
        return jnp.where(keep, hm, 0.0).astype(BF16)

    tile = slice(CONV_HALO, CONV_HALO + TB_MIX)
    hm_scr[0:CONV_HALO, :] = modulated(xp_ref[...], j > 0)
    hm_scr[tile, :] = modulated(x_ref[...])
    hm_scr[CONV_HALO + TB_MIX:2 * CONV_HALO + TB_MIX, :] = modulated(xn_ref[...], j < nblk - 1)
    yield
    left = SSD_CONV // 2
    ct = CONV_TILE
    n_ct = CONV_DIM // ct
    n_ext = TB_MIX + 2 * CONV_HALO
    conv_base = SUBLANES + CONV_HALO

    def project_xbc(c):
        xbc = _dot(hm_scr[...], wx_ref[:, c * ct:(c + 1) * ct])
        for k in range(SSD_CONV):
            ext_refs[c][k, SUBLANES + left - k:SUBLANES + left - k + n_ext, :] = xbc

    def conv(c):
        cols = slice(c * ct, (c + 1) * ct)
        acc = cb_ref[:, cols] + ext_refs[c][0, conv_base:conv_base + TB_MIX, :] * cw_ref[0:1, cols]
        for k in range(1, SSD_CONV):
            acc = acc + ext_refs[c][k, conv_base:conv_base + TB_MIX, :] * cw_ref[k:k + 1, cols]
        y = jax.nn.silu(acc)
        if c < D_INNER // ct:
            for ci in range(CHUNKS_PER_TILE):
                xt_ref[ci, cols, :] = y[ci * SSD_CHUNK:(ci + 1) * SSD_CHUNK, :].T.astype(BF16)
        elif c == D_INNER // ct:
            bc_ref[...] = y.astype(BF16)
        else:
            cc_ref[...] = y.astype(BF16)

    project_xbc(0)
    yield
    for c in range(n_ct):
        if c + 1 < n_ct:
            project_xbc(c + 1)
        if c < D_INNER // ct:
            zc = slice(c * ct, (c + 1) * ct)
            z_ref[:, zc] = _dot(hm_scr[tile, :], wz_ref[:, zc]).astype(z_ref.dtype)
        conv(c)
        yield
    dt_ref[...] = jax.nn.softplus(_dot(hm_scr[tile, :], wd_ref[...]) + dtb_ref[...])
    yield
    for c in range(2 * D_MODEL // PROJ_COLS):
        gc = slice(c * PROJ_COLS, (c + 1) * PROJ_COLS)
        gt_ref[:, gc] = _dot(hm_scr[tile, :], wg_ref[:, gc]).astype(gt_ref.dtype)
        yield
    for c in range(POOL_DIM // PROJ_COLS):
        pc = slice(c * PROJ_COLS, (c + 1) * PROJ_COLS)
        up_ref[:, pc] = _dot(hm_scr[tile, :], wp_ref[:, pc])
        yield


def _mixf_kernel(*refs, zero_init, emit_state, nblk, ntiles):
    (x_ref, xp_ref, xn_ref, mod_ref, g_ref, wp_ref, wz_ref, wx_ref, wd_ref, wg_ref, dtb_ref,
     cw_ref, cb_ref, alog_ref) = refs[:14]
    refs = refs[14:]
    if not zero_init:
        h0_ref, refs = refs[0], refs[1:]
    up_ref, z_ref, gt_ref, dt_ref, xt_ref, bc_ref, cc_ref, y1_ref = refs[:8]
    refs = refs[8:]
    if emit_state:
        hf_ref, refs = refs[0], refs[1:]
    hm_scr, xt_cur, bc_cur, cc_cur, dt_cur, h_scr = refs[:6]
    ext_refs = refs[6:]
    s = pl.program_id(0)
    ja = lax.rem(jnp.minimum(s, ntiles - 1), nblk)
    jb = lax.rem(jnp.maximum(s - 1, 0), nblk)

    @pl.when(s == 0)
    def _():
        xt_cur[...] = jnp.zeros_like(xt_cur)
        bc_cur[...] = jnp.zeros_like(bc_cur)
        cc_cur[...] = jnp.zeros_like(cc_cur)
        dt_cur[...] = jnp.ones_like(dt_cur)

    @pl.when(jb == 0)
    def _():
        h_scr[...] = jnp.zeros_like(h_scr) if zero_init else h0_ref[...]

    proj = _project_and_conv(ja, nblk, x_ref, xp_ref, xn_ref, mod_ref, g_ref, wp_ref, wz_ref, wx_ref, wd_ref,
                             wg_ref, dtb_ref, cw_ref, cb_ref, up_ref, z_ref, gt_ref, dt_ref, xt_ref, bc_ref,
                             cc_ref, hm_scr, ext_refs)
    scan = (None for ci in range(CHUNKS_PER_TILE)
            for _ in _ssd_fwd_chunk(ci, xt_cur, bc_cur, cc_cur, dt_cur, alog_ref, y1_ref, h_scr))
    for n_scan in SCAN_SCHEDULE:
        next(proj, None)
        for _ in range(n_scan):
            next(scan, None)
    for _ in proj:
        pass
    for _ in scan:
        pass

    if emit_state:
        @pl.when(jb == nblk - 1)
        def _():
            hf_ref[...] = h_scr[...]

    xt_cur[...] = xt_ref[...]
    bc_cur[...] = bc_ref[...]
    cc_cur[...] = cc_ref[...]
    dt_cur[...] = dt_ref[...]


def _mix_fwd(x, mod, g, lw, h0_f, layer, emit_state):
    b, t, _ = x.shape
    nblk = t // TB_MIX
    ntiles = b * nblk
    nc = t // SSD_CHUNK
    per = TB_MIX // CONV_HALO
    zero_init = h0_f is None
    mod_idx = (lambda i: i) if mod.shape[0] == b else (lambda i: 0)
    ta = lambda s: jnp.minimum(s, ntiles - 1)
    tb = lambda s: jnp.maximum(s - 1, 0)
    tok = lambda n: pl.BlockSpec((None, TB_MIX, n), lambda s: (ta(s) // nblk, ta(s) % nblk, 0))
    chunked = lambda tile: pl.BlockSpec((None, CHUNKS_PER_TILE, D_INNER, SSD_CHUNK),
                                        lambda s: (tile(s) // nblk, tile(s) % nblk, 0, 0))
    in_specs = [
        tok(D_MODEL),
        pl.BlockSpec((None, CONV_HALO, D_MODEL),
                     lambda s: (ta(s) // nblk, jnp.maximum((ta(s) % nblk) * per - 1, 0), 0)),
        pl.BlockSpec((None, CONV_HALO, D_MODEL),
                     lambda s: (ta(s) // nblk, jnp.minimum((ta(s) % nblk + 1) * per, t // CONV_HALO - 1), 0)),
        pl.BlockSpec((None, 3, D_MODEL), lambda s: (mod_idx(ta(s) // nblk), 0, 0)),
        _resident((1, D_MODEL)),
        _resident((D_MODEL, POOL_DIM)),
        _resident((D_MODEL, D_INNER)),
        _resident((D_MODEL, CONV_DIM)),
        _resident((D_MODEL, LANES)),
        _resident((D_MODEL, 2 * D_MODEL)),
        _resident((1, LANES)),
        _resident((SSD_CONV, CONV_DIM)),
        _resident((1, CONV_DIM)),
        _resident((1, LANES)),
    ]
    args = [x, x, x, mod, g, lw["w_pool"], lw["w_z"], lw["w_xbc"], lw["w_dt"], lw["w_gates"], lw["dt_bias"],
            lw["conv_w"], lw["conv_b"], lw["a_log"]]
    if not zero_init:
        in_specs.append(pl.BlockSpec((None, None, D_INNER, SSD_STATE), lambda s: (tb(s) // nblk, layer, 0, 0)))
        args.append(h0_f)
    out_specs = [tok(POOL_DIM), tok(D_INNER), tok(2 * D_MODEL), tok(LANES), chunked(ta), tok(BC_DIM), tok(BC_DIM),
                 chunked(tb)]
    out_shape = [
        jax.ShapeDtypeStruct((b, t, POOL_DIM), F32),
        jax.ShapeDtypeStruct((b, t, D_INNER), BF16),
        jax.ShapeDtypeStruct((b, t, 2 * D_MODEL), BF16),
        jax.ShapeDtypeStruct((b, t, LANES), F32),
        jax.ShapeDtypeStruct((b, nc, D_INNER, SSD_CHUNK), BF16),
        jax.ShapeDtypeStruct((b, t, BC_DIM), BF16),
        jax.ShapeDtypeStruct((b, t, BC_DIM), BF16),
        jax.ShapeDtypeStruct((b, nc, D_INNER, SSD_CHUNK), BF16),
    ]
    if emit_state:
        out_specs.append(pl.BlockSpec((None, D_INNER, SSD_STATE), lambda s: (tb(s) // nblk, 0, 0)))
        out_shape.append(jax.ShapeDtypeStruct((b, D_INNER, SSD_STATE), F32))
    scratch = [
        pltpu.VMEM((TB_MIX + 2 * CONV_HALO, D_MODEL), BF16),
        pltpu.VMEM((CHUNKS_PER_TILE, D_INNER, SSD_CHUNK), BF16),
        pltpu.VMEM((TB_MIX, BC_DIM), BF16),
        pltpu.VMEM((TB_MIX, BC_DIM), BF16),
        pltpu.VMEM((TB_MIX, LANES), F32),
        pltpu.VMEM((D_INNER, SSD_STATE), F32),
    ] + [pltpu.VMEM((SSD_CONV, TB_MIX + 2 * CONV_HALO + 2 * SUBLANES, CONV_TILE), F32)
         for _ in range(CONV_DIM // CONV_TILE)]
    return pl.pallas_call(
        functools.partial(_mixf_kernel, zero_init=zero_init, emit_state=emit_state, nblk=nblk, ntiles=ntiles),
        grid=(ntiles + 1,), in_specs=in_specs, out_specs=out_specs, out_shape=out_shape,
        scratch_shapes=scratch,
        compiler_params=_params(("arbitrary",)),
        name="mix_fwd",
    )(*args)


def _mixb_kernel(*refs, zero_init, emit_state, nblk):
    (x_ref, xt_ref, bc_ref, cc_ref, dt_ref, y1_ref, z_ref, gt_ref, yp_ref, mod_ref, alog_ref, dsk_ref,
     ng_ref, wbp_ref, wbs_ref, wo_ref) = refs[:16]
    refs = refs[16:]
    if not zero_init:
        h0_ref, refs = refs[0], refs[1:]
    o_ref, refs = refs[0], refs[1:]
    if emit_state:
        hb_ref, refs = refs[0], refs[1:]
    y_scr, h_scr = refs
    j = pl.program_id(1)

    @pl.when(j == 0)
    def _():
        h_scr[...] = jnp.zeros_like(h_scr) if zero_init else h0_ref[...]

    dsk = dsk_ref[...]
    for ci in reversed(range(CHUNKS_PER_TILE)):
        _ssd_bwd_chunk(ci, xt_ref, bc_ref, cc_ref, dt_ref, alog_ref, dsk, y1_ref, y_scr, h_scr)

    if emit_state:
        @pl.when(j == nblk - 1)
        def _():
            hb_ref[...] = h_scr[...]

    y = _rmsnorm(y_scr[...] * jax.nn.silu(z_ref[...].astype(F32)), ng_ref[...])
    gates = jax.nn.sigmoid(gt_ref[...].astype(F32))
    merged = (gates[:, :D_MODEL] * _dot(yp_ref[...].astype(BF16), wbp_ref[...])
              + gates[:, D_MODEL:] * _dot(y.astype(BF16), wbs_ref[...]))
    o_ref[...] = x_ref[...] + mod_ref[2:3, :] * _dot(merged.astype(BF16), wo_ref[...])


def _mix_bwd(x, fwd, y_pool, mod, lw, h0_b, layer, emit_state):
    b, t, _ = x.shape
    nblk = t // TB_MIX
    zero_init = h0_b is None
    z, gates, dt, xt, bcv, ccv, y1 = fwd
    mod_idx = (lambda i: i) if mod.shape[0] == b else (lambda i: 0)
    tok = lambda n: pl.BlockSpec((None, TB_MIX, n), lambda i, j: (i, nblk - 1 - j, 0))
    chunked = pl.BlockSpec((None, CHUNKS_PER_TILE, D_INNER, SSD_CHUNK), lambda i, j: (i, nblk - 1 - j, 0, 0))
    in_specs = [
        tok(D_MODEL), chunked, tok(BC_DIM), tok(BC_DIM), tok(LANES), chunked, tok(D_INNER), tok(2 * D_MODEL),
        tok(POOL_DIM),
        pl.BlockSpec((None, 3, D_MODEL), lambda i, j: (mod_idx(i), 0, 0)),
        _resident((1, LANES)),
        _resident((LANES, LANES)),
        _resident((1, D_INNER)),
        _resident((POOL_DIM, D_MODEL)),
        _resident((D_INNER, D_MODEL)),
        _resident((D_MODEL, D_MODEL)),
    ]
    args = [x, xt, bcv, ccv, dt, y1, z, gates, y_pool, mod, lw["a_log"], lw["d_skip"], lw["ssd_norm"],
            lw["w_branch_pool"], lw["w_branch_ssd"], lw["w_out"]]
    if not zero_init:
        in_specs.append(pl.BlockSpec((None, None, D_INNER, SSD_STATE), lambda i, j: (i, layer, 0, 0)))
        args.append(h0_b)
    out_specs = [tok(D_MODEL)]
    out_shape = [jax.ShapeDtypeStruct((b, t, D_MODEL), F32)]
    if emit_state:
        out_specs.append(pl.BlockSpec((None, D_INNER, SSD_STATE), lambda i, j: (i, 0, 0)))
        out_shape.append(jax.ShapeDtypeStruct((b, D_INNER, SSD_STATE), F32))
    return pl.pallas_call(
        functools.partial(_mixb_kernel, zero_init=zero_init, emit_state=emit_state, nblk=nblk),
        grid=(b, nblk), in_specs=in_specs, out_specs=out_specs, out_shape=out_shape,
        scratch_shapes=[pltpu.VMEM((TB_MIX, D_INNER), F32), pltpu.VMEM((D_INNER, SSD_STATE), F32)],
        compiler_params=_params(("parallel", "arbitrary")),
        name="mix_bwd",
    )(*args)


def _layer(x, mod, rows, h0_f, h0_b, layer, lw, emit_state, final_g):
    ng = lw["norm_g"]
    x = _ffn(x, mod[:, 0:3], ng[0:1], lw["ffn1_w13"], lw["ffn1_w2"])
    mod2 = mod[:, 3:6]
    fwd = _mix_fwd(x, mod2, ng[1:2], lw, h0_f, layer, emit_state)
    hf = fwd[8] if emit_state else None
    y_pool = _pool_mixer(fwd[0], lw["pool_w"], lw["pool_scale"], rows)
    res = _mix_bwd(x, fwd[1:8], y_pool, mod2, lw, h0_b, layer, emit_state)
    x, hb = (res[0], res[1]) if emit_state else (res[0], None)
    x = _ffn(x, mod[:, 6:9], ng[2:3], lw["ffn2_w13"], lw["ffn2_w2"], final_g)
    return x, hf, hb


def _prep_layer(l, p):
    s1 = POOL_DIM
    s2 = s1 + D_INNER
    s3 = s2 + CONV_DIM
    s4 = s3 + 2 * SSD_HEADS
    w_in = p["w_in"][l]
    pad_lanes = lambda v: jnp.pad(v.reshape(1, -1), ((0, 0), (0, LANES - v.size)))
    return dict(
        norm_g=p["norm_g"][l],
        ffn1_w13=p["ffn1_w13"][l].astype(BF16), ffn1_w2=p["ffn1_w2"][l].astype(BF16),
        ffn2_w13=p["ffn2_w13"][l].astype(BF16), ffn2_w2=p["ffn2_w2"][l].astype(BF16),
        w_pool=w_in[:, :s1].astype(BF16), w_z=w_in[:, s1:s2].astype(BF16), w_xbc=w_in[:, s2:s3].astype(BF16),
        w_dt=jnp.pad(w_in[:, s3:s4], ((0, 0), (0, LANES - 2 * SSD_HEADS))).astype(BF16),
        w_gates=w_in[:, s4:].astype(BF16),
        dt_bias=pad_lanes(p["dt_bias"][l]),
        a_log=pad_lanes(p["a_log"][l]),
        d_skip=jnp.broadcast_to(pad_lanes(p["d_skip"][l]).reshape(LANES, 1), (LANES, LANES)),
        pool_w=p["pool_w"][l].astype(BF16), pool_scale=p["pool_scale"][l].reshape(1, POOL_DIM),
        conv_w=p["conv_w"][l], conv_b=p["conv_b"][l].reshape(1, CONV_DIM),
        ssd_norm=p["ssd_norm"][l].reshape(1, D_INNER),
        w_branch_pool=p["w_branch_pool"][l].astype(BF16), w_branch_ssd=p["w_branch_ssd"][l].astype(BF16),
        w_out=p["w_out"][l].astype(BF16),
    )


def kernel(x_prompt, x_sample, state_ssd_fwd, state_ssd_bwd, c, c_ctx, ada_w, ada_b, norm_g, ffn1_w13, ffn1_w2, ffn2_w13, ffn2_w2, w_in, pool_w, pool_scale, conv_w, conv_b, a_log, dt_bias, d_skip, ssd_norm, w_branch_pool, w_branch_ssd, w_out, final_norm):
    p = dict(norm_g=norm_g, ffn1_w13=ffn1_w13, ffn1_w2=ffn1_w2, ffn2_w13=ffn2_w13, ffn2_w2=ffn2_w2,
             w_in=w_in, pool_w=pool_w, pool_scale=pool_scale, conv_w=conv_w, conv_b=conv_b, a_log=a_log,
             dt_bias=dt_bias, d_skip=d_skip, ssd_norm=ssd_norm, w_branch_pool=w_branch_pool,
             w_branch_ssd=w_branch_ssd, w_out=w_out)
    nb = c.shape[0]
    rows = x_sample.shape[1] // GRID_W
    cond = jnp.concatenate([c_ctx[None, :], c], axis=0)
    cond = jnp.pad(cond, ((0, -(nb + 1) % SUBLANES), (0, 0)))
    mod = _adaln(cond, ada_w, ada_b).reshape(DEPTH, cond.shape[0], N_MOD, D_MODEL)
    fg = final_norm.reshape(1, D_MODEL)
    sd = state_ssd_fwd.shape
    h0_f = state_ssd_fwd.reshape(sd[0], DEPTH, D_INNER, SSD_STATE)
    h0_b = state_ssd_bwd.reshape(sd[0], DEPTH, D_INNER, SSD_STATE)
    xc, xl = x_prompt, x_sample
    new_f, new_b = [], []
    for l in range(DEPTH):
        lw = _prep_layer(l, p)
        last = fg if l == DEPTH - 1 else None
        xc, hf, hb = _layer(xc, mod[l, 0:1], None, None, None, l, lw, True, last)
        new_f.append(hf.reshape(-1, SSD_HEADS, SSD_HEAD_DIM, SSD_STATE))
        new_b.append(hb.reshape(-1, SSD_HEADS, SSD_HEAD_DIM, SSD_STATE))
        xl, _, _ = _layer(xl, mod[l, 1:1 + nb], rows, h0_f, h0_b, l, lw, False, last)
    return (xc, xl, jnp.stack(new_f, axis=1), jnp.stack(new_b, axis=1))
```

```python
import functools
import math

import jax
import jax.numpy as jnp
from jax import lax
from jax.experimental import pallas as pl
from jax.experimental.pallas import tpu as pltpu

F32 = jnp.float32
BF16 = jnp.bfloat16

D_MODEL = 1024
DEPTH = 2
GRID_W = 64
POOL_WINDOWS = (2, 4, 8, 16)
POOL_GROUPS = len(POOL_WINDOWS)
POOL_DIM = D_MODEL
POOL_GROUP_DIM = POOL_DIM // POOL_GROUPS
D_INNER = 2 * D_MODEL
SSD_HEAD_DIM = 64
SSD_HEADS = D_INNER // SSD_HEAD_DIM
SSD_GROUPS = 4
HEADS_PER_GROUP = SSD_HEADS // SSD_GROUPS
GROUP_DIM = HEADS_PER_GROUP * SSD_HEAD_DIM
SSD_STATE = 128
SSD_CONV = 4
SSD_CHUNK = 128
BC_DIM = SSD_GROUPS * SSD_STATE
CONV_DIM = D_INNER + 2 * BC_DIM
W_Z_AT = POOL_DIM
W_XBC_AT = W_Z_AT + D_INNER
W_MAIN_COLS = W_XBC_AT + CONV_DIM
D_FF = 2816
N_MOD = 9
EPS = 1e-6
LOG2E = math.log2(math.e)

LANES = 128
SUBLANES = 8
VMEM_LIMIT = 56 * 1024 * 1024

TM_FFN = 512
FF_CHUNK = 256
TB_MIX = 256
CHUNKS_PER_TILE = TB_MIX // SSD_CHUNK
CONV_TILE = BC_DIM
CONV_HALO = 16
POOL_BLOCK = 256
POOL_PAD_ROWS = 8
POOL_SEQ_PER_STEP = 4
POOL_UNROLL = 8
ADA_TN = 2304
ADA_STREAMS = 2
PROJ_COLS = 256
SCAN_SCHEDULE = (0,) + (1,) * 7 + (3,) * 9 + (2,) * 4


def _dot(a, b):
    return jnp.dot(a, b, preferred_element_type=F32)


def _dot_nt(a, b):
    return lax.dot_general(a, b, (((1,), (1,)), ((), ())), preferred_element_type=F32)


def _rmsnorm(x, g):
    return x * lax.rsqrt(jnp.mean(x * x, axis=-1, keepdims=True) + EPS) * g


def _params(sem):
    return pltpu.CompilerParams(dimension_semantics=sem, vmem_limit_bytes=VMEM_LIMIT)


def _resident(shape):
    nd = len(shape)
    return pl.BlockSpec(shape, lambda *_: (0,) * nd, pipeline_mode=pl.Buffered(1))


def _ada_kernel(cond_ref, *refs):
    w_refs, b_ref, o_ref = refs[:ADA_STREAMS], refs[ADA_STREAMS], refs[ADA_STREAMS + 1]
    s = jax.nn.silu(cond_ref[...]).astype(BF16)
    part = ADA_TN // ADA_STREAMS
    for k, w_ref in enumerate(w_refs):
        cols = slice(k * part, (k + 1) * part)
        o_ref[:, cols] = _dot(s, w_ref[...].astype(BF16)) + b_ref[:, cols]


def _adaln(cond, ada_w, ada_b):
    n = cond.shape[0]
    ncol = N_MOD * D_MODEL
    part = ADA_TN // ADA_STREAMS
    w_specs = [pl.BlockSpec((None, D_MODEL, part), lambda l, j, k=k: (l, 0, j * ADA_STREAMS + k))
               for k in range(ADA_STREAMS)]
    return pl.pallas_call(
        _ada_kernel,
        grid=(DEPTH, ncol // ADA_TN),
        in_specs=[pl.BlockSpec((n, D_MODEL), lambda l, j: (0, 0))] + w_specs + [
            pl.BlockSpec((None, 1, ADA_TN), lambda l, j: (l, 0, j)),
        ],
        out_specs=pl.BlockSpec((None, n, ADA_TN), lambda l, j: (l, 0, j)),
        out_shape=jax.ShapeDtypeStruct((DEPTH, n, ncol), F32),
        compiler_params=_params(("parallel", "parallel")),
        name="adaln",
    )(cond, *([ada_w] * ADA_STREAMS), ada_b.reshape(DEPTH, 1, ncol))


def _ffn_kernel(x_ref, mod_ref, g_ref, w13_ref, w2_ref, *rest, final):
    if final:
        fg_ref, o_ref, a_scr = rest
    else:
        o_ref, a_scr = rest
    x = x_ref[...]
    m = mod_ref[...]
    h = (_rmsnorm(x, g_ref[...]) * (1.0 + m[1:2]) + m[0:1]).astype(BF16)
    for j in range(D_FF // FF_CHUNK):
        lo, hi = j * FF_CHUNK, (j + 1) * FF_CHUNK
        gate = _dot(h, w13_ref[:, lo:hi])
        up = _dot(h, w13_ref[:, D_FF + lo:D_FF + hi])
        a_scr[:, lo:hi] = (jax.nn.silu(gate) * up).astype(BF16)
    y = _dot(a_scr[...], w2_ref[...])
    out = x + 0.5 * m[2:3] * y
    if final:
        out = _rmsnorm(out, fg_ref[...])
    o_ref[...] = out


def _ffn(x, mod, g, w13, w2, final_g=None):
    b, t, _ = x.shape
    nt = b * t
    tpm = nt // mod.shape[0]
    final = final_g is not None
    in_specs = [
        pl.BlockSpec((TM_FFN, D_MODEL), lambda i: (i, 0)),
        pl.BlockSpec((None, 3, D_MODEL), lambda i: (i * TM_FFN // tpm, 0, 0)),
        _resident((1, D_MODEL)),
        _resident((D_MODEL, 2 * D_FF)),
        _resident((D_FF, D_MODEL)),
    ]
    args = [x.reshape(nt, D_MODEL), mod, g, w13, w2]
    if final:
        in_specs.append(_resident((1, D_MODEL)))
        args.append(final_g)
    out = pl.pallas_call(
        functools.partial(_ffn_kernel, final=final),
        grid=(nt // TM_FFN,),
        in_specs=in_specs,
        out_specs=pl.BlockSpec((TM_FFN, D_MODEL), lambda i: (i, 0)),
        out_shape=jax.ShapeDtypeStruct((nt, D_MODEL), F32),
        scratch_shapes=[pltpu.VMEM((TM_FFN, D_FF), BF16)],
        compiler_params=_params(("parallel",)),
        name="ffn_final" if final else "ffn",
    )(*args)
    return out.reshape(b, t, D_MODEL)


def _window_count(idx, w, n):
    lo = jnp.maximum(idx - w // 2, 0)
    hi = jnp.minimum(idx - w // 2 + w, n)
    return (hi - lo).astype(F32)


def _band_sum(band, u):
    hi = u.astype(BF16)
    lo = (u - hi.astype(F32)).astype(BF16)
    return _dot(band, hi) + _dot(band, lo)


def _pool_finish(tot, cnt, u, pw_ref, ps_ref):
    d = (tot / cnt - u).astype(BF16)
    return _dot(d, pw_ref[...]) * ps_ref[...]


def _pool_seq_body(u_ref, pw_ref, ps_ref, o_ref, *, w, t):
    ti = lax.broadcasted_iota(jnp.int32, (t, t), 0)
    si = lax.broadcasted_iota(jnp.int32, (t, t), 1)
    band = jnp.where((si >= ti - w // 2) & (si < ti - w // 2 + w), 1.0, 0.0).astype(BF16)
    cnt = _window_count(lax.broadcasted_iota(jnp.int32, (t, POOL_GROUP_DIM), 0), w, t)
    for s in range(u_ref.shape[0]):
        u = u_ref[s]
        o_ref[s] = _pool_finish(_band_sum(band, u), cnt, u, pw_ref, ps_ref)


def _pool_grid_body(u_ref, pw_ref, ps_ref, o_ref, cp_ref, *, w, rows):
    nblk = rows * GRID_W // POOL_BLOCK
    unroll = math.gcd(nblk, POOL_UNROLL)
    pad = POOL_PAD_ROWS * GRID_W
    ti = lax.broadcasted_iota(jnp.int32, (POOL_BLOCK, POOL_BLOCK), 0)
    si = lax.broadcasted_iota(jnp.int32, (POOL_BLOCK, POOL_BLOCK), 1)
    tc = ti & (GRID_W - 1)
    sc = si & (GRID_W - 1)
    same_row = (ti >> 6) == (si >> 6)
    band = jnp.where(same_row & (sc >= tc - w // 2) & (sc < tc - w // 2 + w), 1.0, 0.0).astype(BF16)
    cp_ref[0:pad + GRID_W, :] = jnp.zeros((pad + GRID_W, POOL_GROUP_DIM), F32)
    rows_per_block = POOL_BLOCK // GRID_W

    def col_pool(i, acc):
        off = pl.multiple_of(i * POOL_BLOCK, POOL_BLOCK)
        cp = _band_sum(band, u_ref[pl.ds(off, POOL_BLOCK), :])
        prefix = []
        for k in range(rows_per_block):
            acc = acc + cp[k * GRID_W:(k + 1) * GRID_W, :]
            prefix.append(acc)
        cp_ref[pl.ds(pad + GRID_W + off, POOL_BLOCK), :] = jnp.concatenate(prefix, axis=0)
        return acc

    total = lax.fori_loop(0, nblk, col_pool, jnp.zeros((GRID_W, POOL_GROUP_DIM), F32), unroll=unroll)
    for k in range(1, POOL_PAD_ROWS):
        cp_ref[pad + (rows + k) * GRID_W:pad + (rows + k + 1) * GRID_W, :] = total

    tt = lax.broadcasted_iota(jnp.int32, (POOL_BLOCK, POOL_GROUP_DIM), 0)
    ccnt = _window_count(tt & (GRID_W - 1), w, GRID_W)

    def row_pool(i, carry):
        off = pl.multiple_of(i * POOL_BLOCK, POOL_BLOCK)
        tot = (cp_ref[pl.ds(pad + off + (w // 2) * GRID_W, POOL_BLOCK), :]
               - cp_ref[pl.ds(pad + off - (w // 2) * GRID_W, POOL_BLOCK), :])
        r = i * rows_per_block + (tt >> 6)
        cnt = _window_count(r, w, rows) * ccnt
        o_ref[pl.ds(off, POOL_BLOCK), :] = _pool_finish(tot, cnt, u_ref[pl.ds(off, POOL_BLOCK), :], pw_ref, ps_ref)
        return carry

    lax.fori_loop(0, nblk, row_pool, 0, unroll=unroll)


def _pool_kernel(u_ref, pw_ref, ps_ref, o_ref, *scratch, rows, t):
    g = pl.program_id(1)
    for gi, w in enumerate(POOL_WINDOWS):
        @pl.when(g == gi)
        def _(w=w):
            if rows is None:
                _pool_seq_body(u_ref, pw_ref, ps_ref, o_ref, w=w, t=t)
            else:
                _pool_grid_body(u_ref, pw_ref, ps_ref, o_ref, scratch[0], w=w, rows=rows)


def _pool_mixer(u, pool_w, pool_scale, rows):
    b, t, _ = u.shape
    if rows is None:
        per_step = math.gcd(b, POOL_SEQ_PER_STEP)
        scratch = []
        blk = pl.BlockSpec((per_step, t, POOL_GROUP_DIM), lambda i, g: (i, 0, g))
    else:
        per_step = 1
        scratch = [pltpu.VMEM(((rows + 2 * POOL_PAD_ROWS) * GRID_W, POOL_GROUP_DIM), F32)]
        blk = pl.BlockSpec((None, t, POOL_GROUP_DIM), lambda i, g: (i, 0, g))
    return pl.pallas_call(
        functools.partial(_pool_kernel, rows=rows, t=t),
        grid=(b // per_step, POOL_GROUPS),
        in_specs=[
            blk,
            pl.BlockSpec((None, POOL_GROUP_DIM, POOL_GROUP_DIM), lambda i, g: (g, 0, 0)),
            pl.BlockSpec((1, POOL_GROUP_DIM), lambda i, g: (0, g)),
        ],
        out_specs=blk,
        out_shape=jax.ShapeDtypeStruct((b, t, POOL_DIM), F32),
        scratch_shapes=scratch,
        compiler_params=_params(("parallel", "parallel")),
        name="pool_seq" if rows is None else "pool_grid",
    )(u, pool_w, pool_scale)


def _ssd_chunk_terms(dtc, alog):
    n = SSD_CHUNK
    da = dtc * (-jnp.exp(alog) * LOG2E)
    row = lax.broadcasted_iota(jnp.int32, da.shape, 0)
    cs = da
    rcs = da
    k = 1
    while k < n:
        cs = cs + jnp.where(row >= k, pltpu.roll(cs, k, 0), 0.0)
        rcs = rcs + jnp.where(row < n - k, pltpu.roll(rcs, n - k, 0), 0.0)
        k *= 2
    logdt = jnp.log(dtc) * LOG2E
    return dict(colf=cs - logdt, colb=rcs - logdt, cst=cs.T, rcst=rcs.T, dtt=dtc.T)


def _rows64(mat, h):
    return jnp.broadcast_to(mat[h:h + 1, :], (SSD_HEAD_DIM, mat.shape[1]))


def _pair_rows(mat, h0):
    return jnp.concatenate([_rows64(mat, h0), _rows64(mat, h0 + 1)], axis=0)


def _state_update(h_scr, g, xw, bc_g, dec_t, head_off):
    rows = slice(g * GROUP_DIM, (g + 1) * GROUP_DIM)
    decay = jnp.concatenate(
        [_rows64(dec_t, head_off + g * HEADS_PER_GROUP + h) for h in range(HEADS_PER_GROUP)], axis=0)
    h_scr[rows, :] = h_scr[rows, :] * decay + _dot(xw, bc_g)


def _ssd_fwd_chunk(ci, xt_ref, bc_ref, cc_ref, dt_ref, alog_ref, y1_ref, h_scr):
    n = SSD_CHUNK
    tok = slice(ci * n, (ci + 1) * n)
    tm = _ssd_chunk_terms(dt_ref[tok, :], alog_ref[...])
    cst, rcst, dtt = tm["cst"], tm["rcst"], tm["dtt"]
    tot = cst[:, n - 1:n]
    eoff_t = jnp.exp2(cst)
    w_t = dtt * jnp.exp2(tot - cst)
    dec_t = jnp.exp2(jnp.broadcast_to(tot, (LANES, SSD_STATE)))
    si = lax.broadcasted_iota(jnp.int32, (n, n), 0)
    li = lax.broadcasted_iota(jnp.int32, (n, n), 1)
    src_before = si <= li
    src_after = si >= li
    zeros64 = jnp.zeros((SSD_HEAD_DIM, n), BF16)
    yield
    for g in range(SSD_GROUPS):
        bc_g = bc_ref[tok, g * SSD_STATE:(g + 1) * SSD_STATE]
        cc_g = cc_ref[tok, g * SSD_STATE:(g + 1) * SSD_STATE]
        sc_t = _dot_nt(bc_g, cc_g)
        y_off = _dot_nt(h_scr[g * GROUP_DIM:(g + 1) * GROUP_DIM, :].astype(BF16), cc_g)
        xw = []
        for j in range(HEADS_PER_GROUP // 2):
            h0 = g * HEADS_PER_GROUP + 2 * j
            mts = []
            for h in (h0, h0 + 1):
                seg_f = cst[h:h + 1, :] - tm["colf"][:, h:h + 1]
                hb = SSD_HEADS + h
                seg_b = rcst[hb:hb + 1, :] - tm["colb"][:, hb:hb + 1]
                dec = (jnp.exp2(jnp.where(src_before, seg_f, -jnp.inf))
                       + jnp.exp2(jnp.where(src_after, seg_b, -jnp.inf)))
                mts.append((sc_t * dec).astype(BF16))
            r0 = h0 * SSD_HEAD_DIM
            xp = xt_ref[ci, r0:r0 + 2 * SSD_HEAD_DIM, :]
            lhs = jnp.concatenate([jnp.concatenate([xp[:SSD_HEAD_DIM], zeros64], axis=0),
                                   jnp.concatenate([zeros64, xp[SSD_HEAD_DIM:]], axis=0)], axis=1)
            y_diag = _dot(lhs, jnp.concatenate(mts, axis=0))
            yo = y_off[2 * j * SSD_HEAD_DIM:(2 * j + 2) * SSD_HEAD_DIM, :]
            y1_ref[ci, r0:r0 + 2 * SSD_HEAD_DIM, :] = (y_diag + yo * _pair_rows(eoff_t, h0)).astype(y1_ref.dtype)
            xw.append((xp.astype(F32) * _pair_rows(w_t, h0)).astype(BF16))
            yield
        _state_update(h_scr, g, jnp.concatenate(xw, axis=0), bc_g, dec_t, 0)
        yield


def _ssd_bwd_chunk(ci, xt_ref, bc_ref, cc_ref, dt_ref, alog_ref, dsk, y1_ref, y_scr, h_scr):
    n = SSD_CHUNK
    tok = slice(ci * n, (ci + 1) * n)
    tm = _ssd_chunk_terms(dt_ref[tok, :], alog_ref[...])
    rcst, dtt = tm["rcst"], tm["dtt"]
    tot = rcst[:, 0:1]
    eoff_t = jnp.exp2(rcst)
    w_t = dtt * jnp.exp2(tot - rcst)
    dec_t = jnp.exp2(jnp.broadcast_to(tot, (LANES, SSD_STATE)))
    for g in range(SSD_GROUPS):
        bc_g = bc_ref[tok, g * SSD_STATE:(g + 1) * SSD_STATE]
        cc_g = cc_ref[tok, g * SSD_STATE:(g + 1) * SSD_STATE]
        y_off = _dot_nt(h_scr[g * GROUP_DIM:(g + 1) * GROUP_DIM, :].astype(BF16), cc_g)
        xw, yt = [], []
        for j in range(HEADS_PER_GROUP // 2):
            h0 = g * HEADS_PER_GROUP + 2 * j
            r0 = h0 * SSD_HEAD_DIM
            xp = xt_ref[ci, r0:r0 + 2 * SSD_HEAD_DIM, :].astype(F32)
            yo = y_off[2 * j * SSD_HEAD_DIM:(2 * j + 2) * SSD_HEAD_DIM, :]
            yt.append(y1_ref[ci, r0:r0 + 2 * SSD_HEAD_DIM, :].astype(F32)
                      + yo * _pair_rows(eoff_t, SSD_HEADS + h0) + _pair_rows(dsk, h0) * xp)
            xw.append((xp * _pair_rows(w_t, SSD_HEADS + h0)).astype(BF16))
        y_scr[tok, g * GROUP_DIM:(g + 1) * GROUP_DIM] = jnp.concatenate(yt, axis=0).T
        _state_update(h_scr, g, jnp.concatenate(xw, axis=0), bc_g, dec_t, SSD_HEADS)


def _project_and_conv(j, nblk, x_ref, xp_ref, xn_ref, mod_ref, g_ref, wm_ref, wt_ref,
                      dtb_ref, cw_ref, cb_ref, up_ref, z_ref, gt_ref, dt_ref, xt_ref, bc_ref, cc_ref,
                      hm_scr, ext_refs):
    m = mod_ref[...]
    g = g_ref[...]

    def modulated(x_rows, keep=True):
        hm = _rmsnorm(x_rows, g) * (1.0 + m[1:2]) + m[0:1]
        return jnp.where(keep, hm, 0.0).astype(BF16)

    tile = slice(CONV_HALO, CONV_HALO + TB_MIX)
    hm_scr[0:CONV_HALO, :] = modulated(xp_ref[...], j > 0)
    hm_scr[tile, :] = modulated(x_ref[...])
    hm_scr[CONV_HALO + TB_MIX:2 * CONV_HALO + TB_MIX, :] = modulated(xn_ref[...], j < nblk - 1)
    yield
    left = SSD_CONV // 2
    ct = CONV_TILE
    n_ct = CONV_DIM // ct
    n_ext = TB_MIX + 2 * CONV_HALO
    conv_base = SUBLANES + CONV_HALO

    def project_xbc(c):
        xbc = _dot(hm_scr[...], wm_ref[:, W_XBC_AT + c * ct:W_XBC_AT + (c + 1) * ct])
        for k in range(SSD_CONV):
            ext_refs[c][k, SUBLANES + left - k:SUBLANES + left - k + n_ext, :] = xbc

    def conv(c):
        cols = slice(c * ct, (c + 1) * ct)
        acc = cb_ref[:, cols] + ext_refs[c][0, conv_base:conv_base + TB_MIX, :] * cw_ref[0:1, cols]
        for k in range(1, SSD_CONV):
            acc = acc + ext_refs[c][k, conv_base:conv_base + TB_MIX, :] * cw_ref[k:k + 1, cols]
        y = jax.nn.silu(acc)
        if c < D_INNER // ct:
            for ci in range(CHUNKS_PER_TILE):
                xt_ref[ci, cols, :] = y[ci * SSD_CHUNK:(ci + 1) * SSD_CHUNK, :].T.astype(BF16)
        elif c == D_INNER // ct:
            bc_ref[...] = y.astype(BF16)
        else:
            cc_ref[...] = y.astype(BF16)

    project_xbc(0)
    yield
    for c in range(n_ct):
        if c + 1 < n_ct:
            project_xbc(c + 1)
        if c < D_INNER // ct:
            zc = slice(c * ct, (c + 1) * ct)
            z_ref[:, zc] = _dot(hm_scr[tile, :], wm_ref[:, W_Z_AT + c * ct:W_Z_AT + (c + 1) * ct]).astype(z_ref.dtype)
        conv(c)
        yield
    dt_ref[...] = jax.nn.softplus(_dot(hm_scr[tile, :], wt_ref[:, 0:LANES]) + dtb_ref[...])
    yield
    for c in range(2 * D_MODEL // PROJ_COLS):
        gc = slice(c * PROJ_COLS, (c + 1) * PROJ_COLS)
        gt_ref[:, gc] = _dot(hm_scr[tile, :],
                             wt_ref[:, LANES + c * PROJ_COLS:LANES + (c + 1) * PROJ_COLS]).astype(gt_ref.dtype)
        yield
    for c in range(POOL_DIM // PROJ_COLS):
        pc = slice(c * PROJ_COLS, (c + 1) * PROJ_COLS)
        up_ref[:, pc] = _dot(hm_scr[tile, :], wm_ref[:, pc])
        yield


def _mixf_kernel(*refs, zero_init, emit_state, nblk, ntiles):
    x_ref, xp_ref, xn_ref, mod_ref, g_ref, wm_ref, wt_ref, dtb_ref, cw_ref, cb_ref, alog_ref = refs[:11]
    refs = refs[11:]
    if not zero_init:
        h0_ref, refs = refs[0], refs[1:]
    up_ref, z_ref, gt_ref, dt_ref, xt_ref, bc_ref, cc_ref, y1_ref = refs[:8]
    refs = refs[8:]
    if emit_state:
        hf_ref, refs = refs[0], refs[1:]
    hm_scr, xt_cur, bc_cur, cc_cur, dt_cur, h_scr = refs[:6]
    ext_refs = refs[6:]
    s = pl.program_id(0)
    ja = lax.rem(jnp.minimum(s, ntiles - 1), nblk)
    jb = lax.rem(jnp.maximum(s - 1, 0), nblk)

    @pl.when(s == 0)
    def _():
        xt_cur[...] = jnp.zeros_like(xt_cur)
        bc_cur[...] = jnp.zeros_like(bc_cur)
        cc_cur[...] = jnp.zeros_like(cc_cur)
        dt_cur[...] = jnp.ones_like(dt_cur)

    @pl.when(jb == 0)
    def _():
        h_scr[...] = jnp.zeros_like(h_scr) if zero_init else h0_ref[...]

    proj = _project_and_conv(ja, nblk, x_ref, xp_ref, xn_ref, mod_ref, g_ref, wm_ref, wt_ref,
                             dtb_ref, cw_ref, cb_ref, up_ref, z_ref, gt_ref, dt_ref, xt_ref, bc_ref,
                             cc_ref, hm_scr, ext_refs)
    scan = (None for ci in range(CHUNKS_PER_TILE)
            for _ in _ssd_fwd_chunk(ci, xt_cur, bc_cur, cc_cur, dt_cur, alog_ref, y1_ref, h_scr))
    for n_scan in SCAN_SCHEDULE:
        next(proj, None)
        for _ in range(n_scan):
            next(scan, None)
    for _ in proj:
        pass
    for _ in scan:
        pass

    if emit_state:
        @pl.when(jb == nblk - 1)
        def _():
            hf_ref[...] = h_scr[...]

    xt_cur[...] = xt_ref[...]
    bc_cur[...] = bc_ref[...]
    cc_cur[...] = cc_ref[...]
    dt_cur[...] = dt_ref[...]


def _mix_fwd(x, mod, g, lw, h0_f, layer, emit_state):
    b, t, _ = x.shape
    nblk = t // TB_MIX
    ntiles = b * nblk
    nc = t // SSD_CHUNK
    per = TB_MIX // CONV_HALO
    zero_init = h0_f is None
    mod_idx = (lambda i: i) if mod.shape[0] == b else (lambda i: 0)
    ta = lambda s: jnp.minimum(s, ntiles - 1)
    tb = lambda s: jnp.maximum(s - 1, 0)
    tok = lambda n: pl.BlockSpec((None, TB_MIX, n), lambda s: (ta(s) // nblk, ta(s) % nblk, 0))
    chunked = lambda tile: pl.BlockSpec((None, CHUNKS_PER_TILE, D_INNER, SSD_CHUNK),
                                        lambda s: (tile(s) // nblk, tile(s) % nblk, 0, 0))
    in_specs = [
        tok(D_MODEL),
        pl.BlockSpec((None, CONV_HALO, D_MODEL),
                     lambda s: (ta(s) // nblk, jnp.maximum((ta(s) % nblk) * per - 1, 0), 0)),
        pl.BlockSpec((None, CONV_HALO, D_MODEL),
                     lambda s: (ta(s) // nblk, jnp.minimum((ta(s) % nblk + 1) * per, t // CONV_HALO - 1), 0)),
        pl.BlockSpec((None, 3, D_MODEL), lambda s: (mod_idx(ta(s) // nblk), 0, 0)),
        _resident((1, D_MODEL)),
        _resident((D_MODEL, W_MAIN_COLS)),
        _resident((D_MODEL, LANES + 2 * D_MODEL)),
        _resident((1, LANES)),
        _resident((SSD_CONV, CONV_DIM)),
        _resident((1, CONV_DIM)),
        _resident((1, LANES)),
    ]
    args = [x, x, x, mod, g, lw["w_in"], lw["w_tail"], lw["dt_bias"], lw["conv_w"], lw["conv_b"], lw["a_log"]]
    if not zero_init:
        in_specs.append(pl.BlockSpec((None, None, D_INNER, SSD_STATE), lambda s: (tb(s) // nblk, layer, 0, 0)))
        args.append(h0_f)
    out_specs = [tok(POOL_DIM), tok(D_INNER), tok(2 * D_MODEL), tok(LANES), chunked(ta), tok(BC_DIM), tok(BC_DIM),
                 chunked(tb)]
    out_shape = [
        jax.ShapeDtypeStruct((b, t, POOL_DIM), F32),
        jax.ShapeDtypeStruct((b, t, D_INNER), BF16),
        jax.ShapeDtypeStruct((b, t, 2 * D_MODEL), BF16),
        jax.ShapeDtypeStruct((b, t, LANES), F32),
        jax.ShapeDtypeStruct((b, nc, D_INNER, SSD_CHUNK), BF16),
        jax.ShapeDtypeStruct((b, t, BC_DIM), BF16),
        jax.ShapeDtypeStruct((b, t, BC_DIM), BF16),
        jax.ShapeDtypeStruct((b, nc, D_INNER, SSD_CHUNK), BF16),
    ]
    if emit_state:
        out_specs.append(pl.BlockSpec((None, D_INNER, SSD_STATE), lambda s: (tb(s) // nblk, 0, 0)))
        out_shape.append(jax.ShapeDtypeStruct((b, D_INNER, SSD_STATE), F32))
    scratch = [
        pltpu.VMEM((TB_MIX + 2 * CONV_HALO, D_MODEL), BF16),
        pltpu.VMEM((CHUNKS_PER_TILE, D_INNER, SSD_CHUNK), BF16),
        pltpu.VMEM((TB_MIX, BC_DIM), BF16),
        pltpu.VMEM((TB_MIX, BC_DIM), BF16),
        pltpu.VMEM((TB_MIX, LANES), F32),
        pltpu.VMEM((D_INNER, SSD_STATE), F32),
    ] + [pltpu.VMEM((SSD_CONV, TB_MIX + 2 * CONV_HALO + 2 * SUBLANES, CONV_TILE), F32)
         for _ in range(CONV_DIM // CONV_TILE)]
    return pl.pallas_call(
        functools.partial(_mixf_kernel, zero_init=zero_init, emit_state=emit_state, nblk=nblk, ntiles=ntiles),
        grid=(ntiles + 1,), in_specs=in_specs, out_specs=out_specs, out_shape=out_shape,
        scratch_shapes=scratch,
        compiler_params=_params(("arbitrary",)),
        name="mix_fwd",
    )(*args)


def _mixb_kernel(*refs, zero_init, emit_state, nblk):
    (x_ref, xt_ref, bc_ref, cc_ref, dt_ref, y1_ref, z_ref, gt_ref, yp_ref, mod_ref, alog_ref, dsk_ref,
     ng_ref, wbp_ref, wbs_ref, wo_ref) = refs[:16]
    refs = refs[16:]
    if not zero_init:
        h0_ref, refs = refs[0], refs[1:]
    o_ref, refs = refs[0], refs[1:]
    if emit_state:
        hb_ref, refs = refs[0], refs[1:]
    y_scr, h_scr = refs
    j = pl.program_id(1)

    @pl.when(j == 0)
    def _():
        h_scr[...] = jnp.zeros_like(h_scr) if zero_init else h0_ref[...]

    dsk = dsk_ref[...]
    for ci in reversed(range(CHUNKS_PER_TILE)):
        _ssd_bwd_chunk(ci, xt_ref, bc_ref, cc_ref, dt_ref, alog_ref, dsk, y1_ref, y_scr, h_scr)

    if emit_state:
        @pl.when(j == nblk - 1)
        def _():
            hb_ref[...] = h_scr[...]

    y = _rmsnorm(y_scr[...] * jax.nn.silu(z_ref[...].astype(F32)), ng_ref[...])
    gates = jax.nn.sigmoid(gt_ref[...].astype(F32))
    merged = (gates[:, :D_MODEL] * _dot(yp_ref[...].astype(BF16), wbp_ref[...])
              + gates[:, D_MODEL:] * _dot(y.astype(BF16), wbs_ref[...]))
    o_ref[...] = x_ref[...] + mod_ref[2:3, :] * _dot(merged.astype(BF16), wo_ref[...])


def _mix_bwd(x, fwd, y_pool, mod, lw, h0_b, layer, emit_state):
    b, t, _ = x.shape
    nblk = t // TB_MIX
    zero_init = h0_b is None
    z, gates, dt, xt, bcv, ccv, y1 = fwd
    mod_idx = (lambda i: i) if mod.shape[0] == b else (lambda i: 0)
    tok = lambda n: pl.BlockSpec((None, TB_MIX, n), lambda i, j: (i, nblk - 1 - j, 0))
    chunked = pl.BlockSpec((None, CHUNKS_PER_TILE, D_INNER, SSD_CHUNK), lambda i, j: (i, nblk - 1 - j, 0, 0))
    in_specs = [
        tok(D_MODEL), chunked, tok(BC_DIM), tok(BC_DIM), tok(LANES), chunked, tok(D_INNER), tok(2 * D_MODEL),
        tok(POOL_DIM),
        pl.BlockSpec((None, 3, D_MODEL), lambda i, j: (mod_idx(i), 0, 0)),
        _resident((1, LANES)),
        _resident((LANES, LANES)),
        _resident((1, D_INNER)),
        _resident((POOL_DIM, D_MODEL)),
        _resident((D_INNER, D_MODEL)),
        _resident((D_MODEL, D_MODEL)),
    ]
    args = [x, xt, bcv, ccv, dt, y1, z, gates, y_pool, mod, lw["a_log"], lw["d_skip"], lw["ssd_norm"],
            lw["w_branch_pool"], lw["w_branch_ssd"], lw["w_out"]]
    if not zero_init:
        in_specs.append(pl.BlockSpec((None, None, D_INNER, SSD_STATE), lambda i, j: (i, layer, 0, 0)))
        args.append(h0_b)
    out_specs = [tok(D_MODEL)]
    out_shape = [jax.ShapeDtypeStruct((b, t, D_MODEL), F32)]
    if emit_state:
        out_specs.append(pl.BlockSpec((None, D_INNER, SSD_STATE), lambda i, j: (i, 0, 0)))
        out_shape.append(jax.ShapeDtypeStruct((b, D_INNER, SSD_STATE), F32))
    return pl.pallas_call(
        functools.partial(_mixb_kernel, zero_init=zero_init, emit_state=emit_state, nblk=nblk),
        grid=(b, nblk), in_specs=in_specs, out_specs=out_specs, out_shape=out_shape,
        scratch_shapes=[pltpu.VMEM((TB_MIX, D_INNER), F32), pltpu.VMEM((D_INNER, SSD_STATE), F32)],
        compiler_params=_params(("parallel", "arbitrary")),
        name="mix_bwd",
    )(*args)


def _layer(x, mod, rows, h0_f, h0_b, layer, lw, emit_state, final_g):
    ng = lw["norm_g"]
    x = _ffn(x, mod[:, 0:3], ng[0:1], lw["ffn1_w13"], lw["ffn1_w2"])
    mod2 = mod[:, 3:6]
    fwd = _mix_fwd(x, mod2, ng[1:2], lw, h0_f, layer, emit_state)
    hf = fwd[8] if emit_state else None
    y_pool = _pool_mixer(fwd[0], lw["pool_w"], lw["pool_scale"], rows)
    res = _mix_bwd(x, fwd[1:8], y_pool, mod2, lw, h0_b, layer, emit_state)
    x, hb = (res[0], res[1]) if emit_state else (res[0], None)
    x = _ffn(x, mod[:, 6:9], ng[2:3], lw["ffn2_w13"], lw["ffn2_w2"], final_g)
    return x, hf, hb


def _prep_layer(l, p):
    s3 = W_MAIN_COLS
    s4 = s3 + 2 * SSD_HEADS
    w_in = p["w_in"][l]
    w_tail = jnp.concatenate([jnp.pad(w_in[:, s3:s4], ((0, 0), (0, LANES - 2 * SSD_HEADS))), w_in[:, s4:]], axis=1)
    pad_lanes = lambda v: jnp.pad(v.reshape(1, -1), ((0, 0), (0, LANES - v.size)))
    return dict(
        norm_g=p["norm_g"][l],
        ffn1_w13=p["ffn1_w13"][l].astype(BF16), ffn1_w2=p["ffn1_w2"][l].astype(BF16),
        ffn2_w13=p["ffn2_w13"][l].astype(BF16), ffn2_w2=p["ffn2_w2"][l].astype(BF16),
        w_in=w_in.astype(BF16), w_tail=w_tail.astype(BF16),
        dt_bias=pad_lanes(p["dt_bias"][l]),
        a_log=pad_lanes(p["a_log"][l]),
        d_skip=jnp.broadcast_to(pad_lanes(p["d_skip"][l]).reshape(LANES, 1), (LANES, LANES)),
        pool_w=p["pool_w"][l].astype(BF16), pool_scale=p["pool_scale"][l].reshape(1, POOL_DIM),
        conv_w=p["conv_w"][l], conv_b=p["conv_b"][l].reshape(1, CONV_DIM),
        ssd_norm=p["ssd_norm"][l].reshape(1, D_INNER),
        w_branch_pool=p["w_branch_pool"][l].astype(BF16), w_branch_ssd=p["w_branch_ssd"][l].astype(BF16),
        w_out=p["w_out"][l].astype(BF16),
    )


def kernel(x_prompt, x_sample, state_ssd_fwd, state_ssd_bwd, c, c_ctx, ada_w, ada_b, norm_g, ffn1_w13, ffn1_w2, ffn2_w13, ffn2_w2, w_in, pool_w, pool_scale, conv_w, conv_b, a_log, dt_bias, d_skip, ssd_norm, w_branch_pool, w_branch_ssd, w_out, final_norm):
    p = dict(norm_g=norm_g, ffn1_w13=ffn1_w13, ffn1_w2=ffn1_w2, ffn2_w13=ffn2_w13, ffn2_w2=ffn2_w2,
             w_in=w_in, pool_w=pool_w, pool_scale=pool_scale, conv_w=conv_w, conv_b=conv_b, a_log=a_log,
             dt_bias=dt_bias, d_skip=d_skip, ssd_norm=ssd_norm, w_branch_pool=w_branch_pool,
             w_branch_ssd=w_branch_ssd, w_out=w_out)
    nb = c.shape[0]
    rows = x_sample.shape[1] // GRID_W
    cond = jnp.concatenate([c_ctx[None, :], c], axis=0)
    cond = jnp.pad(cond, ((0, -(nb + 1) % SUBLANES), (0, 0)))
    mod = _adaln(cond, ada_w, ada_b).reshape(DEPTH, cond.shape[0], N_MOD, D_MODEL)
    fg = final_norm.reshape(1, D_MODEL)
    sd = state_ssd_fwd.shape
    h0_f = state_ssd_fwd.reshape(sd[0], DEPTH, D_INNER, SSD_STATE)
    h0_b = state_ssd_bwd.reshape(sd[0], DEPTH, D_INNER, SSD_STATE)
    xc, xl = x_prompt, x_sample
    new_f, new_b = [], []
    for l in range(DEPTH):
        lw = _prep_layer(l, p)
        last = fg if l == DEPTH - 1 else None
        xc, hf, hb = _layer(xc, mod[l, 0:1], None, None, None, l, lw, True, last)
        new_f.append(hf.reshape(-1, SSD_HEADS, SSD_HEAD_DIM, SSD_STATE))
        new_b.append(hb.reshape(-1, SSD_HEADS, SSD_HEAD_DIM, SSD_STATE))
        xl, _, _ = _layer(xl, mod[l, 1:1 + nb], rows, h0_f, h0_b, l, lw, False, last)
    return (xc, xl, jnp.stack(new_f, axis=1), jnp.stack(new_b, axis=1))
```

```python
import functools
import math

import jax
import jax.numpy as jnp
from jax import lax
from jax.experimental import pallas as pl
from jax.experimental.pallas import tpu as pltpu

F32 = jnp.float32
BF16 = jnp.bfloat16

D_MODEL = 1024
DEPTH = 2
GRID_W = 64
POOL_WINDOWS = (2, 4, 8, 16)
POOL_GROUPS = len(POOL_WINDOWS)
POOL_DIM = D_MODEL
POOL_GROUP_DIM = POOL_DIM // POOL_GROUPS
D_INNER = 2 * D_MODEL
SSD_HEAD_DIM = 64
SSD_HEADS = D_INNER // SSD_HEAD_DIM
SSD_GROUPS = 4
HEADS_PER_GROUP = SSD_HEADS // SSD_GROUPS
GROUP_DIM = HEADS_PER_GROUP * SSD_HEAD_DIM
SSD_STATE = 128
SSD_CONV = 4
SSD_CHUNK = 128
BC_DIM = SSD_GROUPS * SSD_STATE
CONV_DIM = D_INNER + 2 * BC_DIM
W_Z_AT = POOL_DIM
W_XBC_AT = W_Z_AT + D_INNER
W_MAIN_COLS = W_XBC_AT + CONV_DIM
D_FF = 2816
N_MOD = 9
EPS = 1e-6
LOG2E = math.log2(math.e)

LANES = 128
SUBLANES = 8
VMEM_LIMIT = 56 * 1024 * 1024

TM_FFN = 512
FF_CHUNK = 256
TB_MIX = 256
CHUNKS_PER_TILE = TB_MIX // SSD_CHUNK
CONV_TILE = BC_DIM
CONV_HALO = 16
POOL_BLOCK = 256
POOL_PAD_ROWS = 8
POOL_SEQ_PER_STEP = 4
POOL_UNROLL = 8
ADA_TN = 2304
PROJ_COLS = 256
SCAN_SCHEDULE = (0,) + (1,) * 7 + (3,) * 9 + (2,) * 4


def _dot(a, b):
    return jnp.dot(a, b, preferred_element_type=F32)


def _dot_nt(a, b):
    return lax.dot_general(a, b, (((1,), (1,)), ((), ())), preferred_element_type=F32)


def _rmsnorm(x, g):
    return x * lax.rsqrt(jnp.mean(x * x, axis=-1, keepdims=True) + EPS) * g


def _params(sem):
    return pltpu.CompilerParams(dimension_semantics=sem, vmem_limit_bytes=VMEM_LIMIT)


def _resident(shape, layer=None):
    nd = len(shape)
    if layer is None:
        return pl.BlockSpec(shape, lambda *_: (0,) * nd, pipeline_mode=pl.Buffered(1))
    return pl.BlockSpec((None,) + tuple(shape), lambda *_: (layer,) + (0,) * nd, pipeline_mode=pl.Buffered(1))


def _ada_kernel(cond_ref, w_ref, b_ref, o_ref):
    s = jax.nn.silu(cond_ref[...]).astype(BF16)
    o_ref[...] = _dot(s, w_ref[...].astype(BF16)) + b_ref[...]


def _adaln(cond, ada_w, ada_b):
    n = cond.shape[0]
    ncol = N_MOD * D_MODEL
    return pl.pallas_call(
        _ada_kernel,
        grid=(DEPTH, ncol // ADA_TN),
        in_specs=[
            pl.BlockSpec((n, D_MODEL), lambda l, j: (0, 0)),
            pl.BlockSpec((None, D_MODEL, ADA_TN), lambda l, j: (l, 0, j)),
            pl.BlockSpec((None, 1, ADA_TN), lambda l, j: (l, 0, j)),
        ],
        out_specs=pl.BlockSpec((None, n, ADA_TN), lambda l, j: (l, 0, j)),
        out_shape=jax.ShapeDtypeStruct((DEPTH, n, ncol), F32),
        compiler_params=_params(("parallel", "parallel")),
        name="adaln",
    )(cond, ada_w, ada_b.reshape(DEPTH, 1, ncol))


def _ffn_kernel(x_ref, mod_ref, g_ref, w13_ref, w2_ref, *rest, final):
    if final:
        fg_ref, o_ref, a_scr = rest
    else:
        o_ref, a_scr = rest
    x = x_ref[...]
    m = mod_ref[...]
    h = (_rmsnorm(x, g_ref[...]) * (1.0 + m[1:2]) + m[0:1]).astype(BF16)
    for j in range(D_FF // FF_CHUNK):
        lo, hi = j * FF_CHUNK, (j + 1) * FF_CHUNK
        gate = _dot(h, w13_ref[:, lo:hi])
        up = _dot(h, w13_ref[:, D_FF + lo:D_FF + hi])
        a_scr[:, lo:hi] = (jax.nn.silu(gate) * up).astype(BF16)
    y = _dot(a_scr[...], w2_ref[...])
    out = x + 0.5 * m[2:3] * y
    if final:
        out = _rmsnorm(out, fg_ref[...])
    o_ref[...] = out


def _ffn(x, mod, g, w13, w2, layer, final_g=None):
    b, t, _ = x.shape
    nt = b * t
    tpm = nt // mod.shape[0]
    final = final_g is not None
    in_specs = [
        pl.BlockSpec((TM_FFN, D_MODEL), lambda i: (i, 0)),
        pl.BlockSpec((None, 3, D_MODEL), lambda i: (i * TM_FFN // tpm, 0, 0)),
        _resident((1, D_MODEL)),
        _resident((D_MODEL, 2 * D_FF), layer),
        _resident((D_FF, D_MODEL), layer),
    ]
    args = [x.reshape(nt, D_MODEL), mod, g, w13, w2]
    if final:
        in_specs.append(_resident((1, D_MODEL)))
        args.append(final_g)
    out = pl.pallas_call(
        functools.partial(_ffn_kernel, final=final),
        grid=(nt // TM_FFN,),
        in_specs=in_specs,
        out_specs=pl.BlockSpec((TM_FFN, D_MODEL), lambda i: (i, 0)),
        out_shape=jax.ShapeDtypeStruct((nt, D_MODEL), F32),
        scratch_shapes=[pltpu.VMEM((TM_FFN, D_FF), BF16)],
        compiler_params=_params(("parallel",)),
        name="ffn_final" if final else "ffn",
    )(*args)
    return out.reshape(b, t, D_MODEL)


def _window_count(idx, w, n):
    lo = jnp.maximum(idx - w // 2, 0)
    hi = jnp.minimum(idx - w // 2 + w, n)
    return (hi - lo).astype(F32)


def _band_sum(band, u):
    hi = u.astype(BF16)
    lo = (u - hi.astype(F32)).astype(BF16)
    return _dot(band, hi) + _dot(band, lo)


def _pool_finish(tot, cnt, u, pw_ref, ps_ref):
    d = (tot / cnt - u).astype(BF16)
    return _dot(d, pw_ref[...]) * ps_ref[...]


def _pool_seq_body(u_ref, pw_ref, ps_ref, o_ref, *, w, t):
    ti = lax.broadcasted_iota(jnp.int32, (t, t), 0)
    si = lax.broadcasted_iota(jnp.int32, (t, t), 1)
    band = jnp.where((si >= ti - w // 2) & (si < ti - w // 2 + w), 1.0, 0.0).astype(BF16)
    cnt = _window_count(lax.broadcasted_iota(jnp.int32, (t, POOL_GROUP_DIM), 0), w, t)
    for s in range(u_ref.shape[0]):
        u = u_ref[s]
        o_ref[s] = _pool_finish(_band_sum(band, u), cnt, u, pw_ref, ps_ref)


def _pool_grid_body(u_ref, pw_ref, ps_ref, o_ref, cp_ref, *, w, rows):
    nblk = rows * GRID_W // POOL_BLOCK
    unroll = math.gcd(nblk, POOL_UNROLL)
    pad = POOL_PAD_ROWS * GRID_W
    ti = lax.broadcasted_iota(jnp.int32, (POOL_BLOCK, POOL_BLOCK), 0)
    si = lax.broadcasted_iota(jnp.int32, (POOL_BLOCK, POOL_BLOCK), 1)
    tc = ti & (GRID_W - 1)
    sc = si & (GRID_W - 1)
    same_row = (ti >> 6) == (si >> 6)
    band = jnp.where(same_row & (sc >= tc - w // 2) & (sc < tc - w // 2 + w), 1.0, 0.0).astype(BF16)
    cp_ref[0:pad + GRID_W, :] = jnp.zeros((pad + GRID_W, POOL_GROUP_DIM), F32)
    rows_per_block = POOL_BLOCK // GRID_W

    def col_pool(i, acc):
        off = pl.multiple_of(i * POOL_BLOCK, POOL_BLOCK)
        cp = _band_sum(band, u_ref[pl.ds(off, POOL_BLOCK), :])
        prefix = []
        for k in range(rows_per_block):
            acc = acc + cp[k * GRID_W:(k + 1) * GRID_W, :]
            prefix.append(acc)
        cp_ref[pl.ds(pad + GRID_W + off, POOL_BLOCK), :] = jnp.concatenate(prefix, axis=0)
        return acc

    total = lax.fori_loop(0, nblk, col_pool, jnp.zeros((GRID_W, POOL_GROUP_DIM), F32), unroll=unroll)
    for k in range(1, POOL_PAD_ROWS):
        cp_ref[pad + (rows + k) * GRID_W:pad + (rows + k + 1) * GRID_W, :] = total

    tt = lax.broadcasted_iota(jnp.int32, (POOL_BLOCK, POOL_GROUP_DIM), 0)
    ccnt = _window_count(tt & (GRID_W - 1), w, GRID_W)

    def row_pool(i, carry):
        off = pl.multiple_of(i * POOL_BLOCK, POOL_BLOCK)
        tot = (cp_ref[pl.ds(pad + off + (w // 2) * GRID_W, POOL_BLOCK), :]
               - cp_ref[pl.ds(pad + off - (w // 2) * GRID_W, POOL_BLOCK), :])
        r = i * rows_per_block + (tt >> 6)
        cnt = _window_count(r, w, rows) * ccnt
        o_ref[pl.ds(off, POOL_BLOCK), :] = _pool_finish(tot, cnt, u_ref[pl.ds(off, POOL_BLOCK), :], pw_ref, ps_ref)
        return carry

    lax.fori_loop(0, nblk, row_pool, 0, unroll=unroll)


def _pool_kernel(u_ref, pw_ref, ps_ref, o_ref, *scratch, rows, t):
    g = pl.program_id(1)
    for gi, w in enumerate(POOL_WINDOWS):
        @pl.when(g == gi)
        def _(w=w):
            if rows is None:
                _pool_seq_body(u_ref, pw_ref, ps_ref, o_ref, w=w, t=t)
            else:
                _pool_grid_body(u_ref, pw_ref, ps_ref, o_ref, scratch[0], w=w, rows=rows)


def _pool_mixer(u, pool_w, pool_scale, rows, layer):
    b, t, _ = u.shape
    if rows is None:
        per_step = math.gcd(b, POOL_SEQ_PER_STEP)
        scratch = []
        blk = pl.BlockSpec((per_step, t, POOL_GROUP_DIM), lambda i, g: (i, 0, g))
    else:
        per_step = 1
        scratch = [pltpu.VMEM(((rows + 2 * POOL_PAD_ROWS) * GRID_W, POOL_GROUP_DIM), F32)]
        blk = pl.BlockSpec((None, t, POOL_GROUP_DIM), lambda i, g: (i, 0, g))
    return pl.pallas_call(
        functools.partial(_pool_kernel, rows=rows, t=t),
        grid=(b // per_step, POOL_GROUPS),
        in_specs=[
            blk,
            pl.BlockSpec((None, None, POOL_GROUP_DIM, POOL_GROUP_DIM), lambda i, g: (layer, g, 0, 0)),
            pl.BlockSpec((1, POOL_GROUP_DIM), lambda i, g: (0, g)),
        ],
        out_specs=blk,
        out_shape=jax.ShapeDtypeStruct((b, t, POOL_DIM), F32),
        scratch_shapes=scratch,
        compiler_params=_params(("parallel", "parallel")),
        name="pool_seq" if rows is None else "pool_grid",
    )(u, pool_w, pool_scale)


def _ssd_chunk_terms(dtc, alog):
    n = SSD_CHUNK
    da = dtc * (-jnp.exp(alog) * LOG2E)
    row = lax.broadcasted_iota(jnp.int32, da.shape, 0)
    cs = da
    rcs = da
    k = 1
    while k < n:
        cs = cs + jnp.where(row >= k, pltpu.roll(cs, k, 0), 0.0)
        rcs = rcs + jnp.where(row < n - k, pltpu.roll(rcs, n - k, 0), 0.0)
        k *= 2
    logdt = jnp.log(dtc) * LOG2E
    return dict(colf=cs - logdt, colb=rcs - logdt, cst=cs.T, rcst=rcs.T, dtt=dtc.T)


def _rows64(mat, h):
    return jnp.broadcast_to(mat[h:h + 1, :], (SSD_HEAD_DIM, mat.shape[1]))


def _pair_rows(mat, h0):
    return jnp.concatenate([_rows64(mat, h0), _rows64(mat, h0 + 1)], axis=0)


def _state_update(h_scr, g, xw, bc_g, dec_t, head_off):
    rows = slice(g * GROUP_DIM, (g + 1) * GROUP_DIM)
    decay = jnp.concatenate(
        [_rows64(dec_t, head_off + g * HEADS_PER_GROUP + h) for h in range(HEADS_PER_GROUP)], axis=0)
    h_scr[rows, :] = h_scr[rows, :] * decay + _dot(xw, bc_g)


def _ssd_fwd_chunk(ci, xt_ref, bc_ref, cc_ref, dt_ref, alog_ref, y1_ref, h_scr):
    n = SSD_CHUNK
    tok = slice(ci * n, (ci + 1) * n)
    tm = _ssd_chunk_terms(dt_ref[tok, :], alog_ref[...])
    cst, rcst, dtt = tm["cst"], tm["rcst"], tm["dtt"]
    tot = cst[:, n - 1:n]
    eoff_t = jnp.exp2(cst)
    w_t = dtt * jnp.exp2(tot - cst)
    dec_t = jnp.exp2(jnp.broadcast_to(tot, (LANES, SSD_STATE)))
    si = lax.broadcasted_iota(jnp.int32, (n, n), 0)
    li = lax.broadcasted_iota(jnp.int32, (n, n), 1)
    src_before = si <= li
    src_after = si >= li
    zeros64 = jnp.zeros((SSD_HEAD_DIM, n), BF16)
    yield
    for g in range(SSD_GROUPS):
        bc_g = bc_ref[tok, g * SSD_STATE:(g + 1) * SSD_STATE]
        cc_g = cc_ref[tok, g * SSD_STATE:(g + 1) * SSD_STATE]
        sc_t = _dot_nt(bc_g, cc_g)
        y_off = _dot_nt(h_scr[g * GROUP_DIM:(g + 1) * GROUP_DIM, :].astype(BF16), cc_g)
        xw = []
        for j in range(HEADS_PER_GROUP // 2):
            h0 = g * HEADS_PER_GROUP + 2 * j
            mts = []
            for h in (h0, h0 + 1):
                seg_f = cst[h:h + 1, :] - tm["colf"][:, h:h + 1]
                hb = SSD_HEADS + h
                seg_b = rcst[hb:hb + 1, :] - tm["colb"][:, hb:hb + 1]
                dec = (jnp.exp2(jnp.where(src_before, seg_f, -jnp.inf))
                       + jnp.exp2(jnp.where(src_after, seg_b, -jnp.inf)))
                mts.append((sc_t * dec).astype(BF16))
            r0 = h0 * SSD_HEAD_DIM
            xp = xt_ref[ci, r0:r0 + 2 * SSD_HEAD_DIM, :]
            lhs = jnp.concatenate([jnp.concatenate([xp[:SSD_HEAD_DIM], zeros64], axis=0),
                                   jnp.concatenate([zeros64, xp[SSD_HEAD_DIM:]], axis=0)], axis=1)
            y_diag = _dot(lhs, jnp.concatenate(mts, axis=0))
            yo = y_off[2 * j * SSD_HEAD_DIM:(2 * j + 2) * SSD_HEAD_DIM, :]
            y1_ref[ci, r0:r0 + 2 * SSD_HEAD_DIM, :] = (y_diag + yo * _pair_rows(eoff_t, h0)).astype(y1_ref.dtype)
            xw.append((xp.astype(F32) * _pair_rows(w_t, h0)).astype(BF16))
            yield
        _state_update(h_scr, g, jnp.concatenate(xw, axis=0), bc_g, dec_t, 0)
        yield


def _ssd_bwd_chunk(ci, xt_ref, bc_ref, cc_ref, dt_ref, alog_ref, dsk, y1_ref, y_scr, h_scr):
    n = SSD_CHUNK
    tok = slice(ci * n, (ci + 1) * n)
    tm = _ssd_chunk_terms(dt_ref[tok, :], alog_ref[...])
    rcst, dtt = tm["rcst"], tm["dtt"]
    tot = rcst[:, 0:1]
    eoff_t = jnp.exp2(rcst)
    w_t = dtt * jnp.exp2(tot - rcst)
    dec_t = jnp.exp2(jnp.broadcast_to(tot, (LANES, SSD_STATE)))
    for g in range(SSD_GROUPS):
        bc_g = bc_ref[tok, g * SSD_STATE:(g + 1) * SSD_STATE]
        cc_g = cc_ref[tok, g * SSD_STATE:(g + 1) * SSD_STATE]
        y_off = _dot_nt(h_scr[g * GROUP_DIM:(g + 1) * GROUP_DIM, :].astype(BF16), cc_g)
        xw, yt = [], []
        for j in range(HEADS_PER_GROUP // 2):
            h0 = g * HEADS_PER_GROUP + 2 * j
            r0 = h0 * SSD_HEAD_DIM
            xp = xt_ref[ci, r0:r0 + 2 * SSD_HEAD_DIM, :].astype(F32)
            yo = y_off[2 * j * SSD_HEAD_DIM:(2 * j + 2) * SSD_HEAD_DIM, :]
            yt.append(y1_ref[ci, r0:r0 + 2 * SSD_HEAD_DIM, :].astype(F32)
                      + yo * _pair_rows(eoff_t, SSD_HEADS + h0) + _pair_rows(dsk, h0) * xp)
            xw.append((xp * _pair_rows(w_t, SSD_HEADS + h0)).astype(BF16))
        y_scr[tok, g * GROUP_DIM:(g + 1) * GROUP_DIM] = jnp.concatenate(yt, axis=0).T
        _state_update(h_scr, g, jnp.concatenate(xw, axis=0), bc_g, dec_t, SSD_HEADS)


def _project_and_conv(j, nblk, x_ref, xp_ref, xn_ref, mod_ref, g_ref, wm_ref, wt_ref,
                      dtb_ref, cw_ref, cb_ref, up_ref, z_ref, gt_ref, dt_ref, xt_ref, bc_ref, cc_ref,
                      hm_scr, ext_refs):
    m = mod_ref[...]
    g = g_ref[...]

    def modulated(x_rows, keep=True):
        hm = _rmsnorm(x_rows, g) * (1.0 + m[1:2]) + m[0:1]
        return jnp.where(keep, hm, 0.0).astype(BF16)

    tile = slice(CONV_HALO, CONV_HALO + TB_MIX)
    hm_scr[0:CONV_HALO, :] = modulated(xp_ref[...], j > 0)
    hm_scr[tile, :] = modulated(x_ref[...])
    hm_scr[CONV_HALO + TB_MIX:2 * CONV_HALO + TB_MIX, :] = modulated(xn_ref[...], j < nblk - 1)
    yield
    left = SSD_CONV // 2
    ct = CONV_TILE
    n_ct = CONV_DIM // ct
    n_ext = TB_MIX + 2 * CONV_HALO
    conv_base = SUBLANES + CONV_HALO

    def project_xbc(c):
        xbc = _dot(hm_scr[...], wm_ref[:, W_XBC_AT + c * ct:W_XBC_AT + (c + 1) * ct])
        for k in range(SSD_CONV):
            ext_refs[c][k, SUBLANES + left - k:SUBLANES + left - k + n_ext, :] = xbc

    def conv(c):
        cols = slice(c * ct, (c + 1) * ct)
        acc = cb_ref[:, cols] + ext_refs[c][0, conv_base:conv_base + TB_MIX, :] * cw_ref[0:1, cols]
        for k in range(1, SSD_CONV):
            acc = acc + ext_refs[c][k, conv_base:conv_base + TB_MIX, :] * cw_ref[k:k + 1, cols]
        y = jax.nn.silu(acc)
        if c < D_INNER // ct:
            for ci in range(CHUNKS_PER_TILE):
                xt_ref[ci, cols, :] = y[ci * SSD_CHUNK:(ci + 1) * SSD_CHUNK, :].T.astype(BF16)
        elif c == D_INNER // ct:
            bc_ref[...] = y.astype(BF16)
        else:
            cc_ref[...] = y.astype(BF16)

    project_xbc(0)
    yield
    for c in range(n_ct):
        if c + 1 < n_ct:
            project_xbc(c + 1)
        if c < D_INNER // ct:
            zc = slice(c * ct, (c + 1) * ct)
            z_ref[:, zc] = _dot(hm_scr[tile, :], wm_ref[:, W_Z_AT + c * ct:W_Z_AT + (c + 1) * ct]).astype(z_ref.dtype)
        conv(c)
        yield
    dt_ref[...] = jax.nn.softplus(_dot(hm_scr[tile, :], wt_ref[:, 0:LANES]) + dtb_ref[...])
    yield
    for c in range(2 * D_MODEL // PROJ_COLS):
        gc = slice(c * PROJ_COLS, (c + 1) * PROJ_COLS)
        gt_ref[:, gc] = _dot(hm_scr[tile, :],
                             wt_ref[:, LANES + c * PROJ_COLS:LANES + (c + 1) * PROJ_COLS]).astype(gt_ref.dtype)
        yield
    for c in range(POOL_DIM // PROJ_COLS):
        pc = slice(c * PROJ_COLS, (c + 1) * PROJ_COLS)
        up_ref[:, pc] = _dot(hm_scr[tile, :], wm_ref[:, pc])
        yield


def _mixf_kernel(*refs, zero_init, emit_state, nblk, ntiles):
    x_ref, xp_ref, xn_ref, mod_ref, g_ref, wm_ref, wt_ref, dtb_ref, cw_ref, cb_ref, alog_ref = refs[:11]
    refs = refs[11:]
    if not zero_init:
        h0_ref, refs = refs[0], refs[1:]
    up_ref, z_ref, gt_ref, dt_ref, xt_ref, bc_ref, cc_ref, y1_ref = refs[:8]
    refs = refs[8:]
    if emit_state:
        hf_ref, refs = refs[0], refs[1:]
    hm_scr, xt_cur, bc_cur, cc_cur, dt_cur, h_scr = refs[:6]
    ext_refs = refs[6:]
    s = pl.program_id(0)
    ja = lax.rem(jnp.minimum(s, ntiles - 1), nblk)
    jb = lax.rem(jnp.maximum(s - 1, 0), nblk)

    @pl.when(s == 0)
    def _():
        xt_cur[...] = jnp.zeros_like(xt_cur)
        bc_cur[...] = jnp.zeros_like(bc_cur)
        cc_cur[...] = jnp.zeros_like(cc_cur)
        dt_cur[...] = jnp.ones_like(dt_cur)

    @pl.when(jb == 0)
    def _():
        h_scr[...] = jnp.zeros_like(h_scr) if zero_init else h0_ref[...]

    proj = _project_and_conv(ja, nblk, x_ref, xp_ref, xn_ref, mod_ref, g_ref, wm_ref, wt_ref,
                             dtb_ref, cw_ref, cb_ref, up_ref, z_ref, gt_ref, dt_ref, xt_ref, bc_ref,
                             cc_ref, hm_scr, ext_refs)
    scan = (None for ci in range(CHUNKS_PER_TILE)
            for _ in _ssd_fwd_chunk(ci, xt_cur, bc_cur, cc_cur, dt_cur, alog_ref, y1_ref, h_scr))
    for n_scan in SCAN_SCHEDULE:
        next(proj, None)
        for _ in range(n_scan):
            next(scan, None)
    for _ in proj:
        pass
    for _ in scan:
        pass

    if emit_state:
        @pl.when(jb == nblk - 1)
        def _():
            hf_ref[...] = h_scr[...]

    xt_cur[...] = xt_ref[...]
    bc_cur[...] = bc_ref[...]
    cc_cur[...] = cc_ref[...]
    dt_cur[...] = dt_ref[...]


def _mix_fwd(x, mod, g, lw, h0_f, layer, emit_state):
    b, t, _ = x.shape
    nblk = t // TB_MIX
    ntiles = b * nblk
    nc = t // SSD_CHUNK
    per = TB_MIX // CONV_HALO
    zero_init = h0_f is None
    mod_idx = (lambda i: i) if mod.shape[0] == b else (lambda i: 0)
    ta = lambda s: jnp.minimum(s, ntiles - 1)
    tb = lambda s: jnp.maximum(s - 1, 0)
    tok = lambda n: pl.BlockSpec((None, TB_MIX, n), lambda s: (ta(s) // nblk, ta(s) % nblk, 0))
    chunked = lambda tile: pl.BlockSpec((None, CHUNKS_PER_TILE, D_INNER, SSD_CHUNK),
                                        lambda s: (tile(s) // nblk, tile(s) % nblk, 0, 0))
    in_specs = [
        tok(D_MODEL),
        pl.BlockSpec((None, CONV_HALO, D_MODEL),
                     lambda s: (ta(s) // nblk, jnp.maximum((ta(s) % nblk) * per - 1, 0), 0)),
        pl.BlockSpec((None, CONV_HALO, D_MODEL),
                     lambda s: (ta(s) // nblk, jnp.minimum((ta(s) % nblk + 1) * per, t // CONV_HALO - 1), 0)),
        pl.BlockSpec((None, 3, D_MODEL), lambda s: (mod_idx(ta(s) // nblk), 0, 0)),
        _resident((1, D_MODEL)),
        _resident((D_MODEL, W_MAIN_COLS), layer),
        _resident((D_MODEL, LANES + 2 * D_MODEL), layer),
        _resident((1, LANES)),
        _resident((SSD_CONV, CONV_DIM)),
        _resident((1, CONV_DIM)),
        _resident((1, LANES)),
    ]
    args = [x, x, x, mod, g, lw["w_in"], lw["w_tail"], lw["dt_bias"], lw["conv_w"], lw["conv_b"], lw["a_log"]]
    if not zero_init:
        in_specs.append(pl.BlockSpec((None, None, D_INNER, SSD_STATE), lambda s: (tb(s) // nblk, layer, 0, 0)))
        args.append(h0_f)
    out_specs = [tok(POOL_DIM), tok(D_INNER), tok(2 * D_MODEL), tok(LANES), chunked(ta), tok(BC_DIM), tok(BC_DIM),
                 chunked(tb)]
    out_shape = [
        jax.ShapeDtypeStruct((b, t, POOL_DIM), F32),
        jax.ShapeDtypeStruct((b, t, D_INNER), BF16),
        jax.ShapeDtypeStruct((b, t, 2 * D_MODEL), BF16),
        jax.ShapeDtypeStruct((b, t, LANES), F32),
        jax.ShapeDtypeStruct((b, nc, D_INNER, SSD_CHUNK), BF16),
        jax.ShapeDtypeStruct((b, t, BC_DIM), BF16),
        jax.ShapeDtypeStruct((b, t, BC_DIM), BF16),
        jax.ShapeDtypeStruct((b, nc, D_INNER, SSD_CHUNK), BF16),
    ]
    if emit_state:
        out_specs.append(pl.BlockSpec((None, D_INNER, SSD_STATE), lambda s: (tb(s) // nblk, 0, 0)))
        out_shape.append(jax.ShapeDtypeStruct((b, D_INNER, SSD_STATE), F32))
    scratch = [
        pltpu.VMEM((TB_MIX + 2 * CONV_HALO, D_MODEL), BF16),
        pltpu.VMEM((CHUNKS_PER_TILE, D_INNER, SSD_CHUNK), BF16),
        pltpu.VMEM((TB_MIX, BC_DIM), BF16),
        pltpu.VMEM((TB_MIX, BC_DIM), BF16),
        pltpu.VMEM((TB_MIX, LANES), F32),
        pltpu.VMEM((D_INNER, SSD_STATE), F32),
    ] + [pltpu.VMEM((SSD_CONV, TB_MIX + 2 * CONV_HALO + 2 * SUBLANES, CONV_TILE), F32)
         for _ in range(CONV_DIM // CONV_TILE)]
    return pl.pallas_call(
        functools.partial(_mixf_kernel, zero_init=zero_init, emit_state=emit_state, nblk=nblk, ntiles=ntiles),
        grid=(ntiles + 1,), in_specs=in_specs, out_specs=out_specs, out_shape=out_shape,
        scratch_shapes=scratch,
        compiler_params=_params(("arbitrary",)),
        name="mix_fwd",
    )(*args)


def _mixb_kernel(*refs, zero_init, emit_state, nblk):
    (x_ref, xt_ref, bc_ref, cc_ref, dt_ref, y1_ref, z_ref, gt_ref, yp_ref, mod_ref, alog_ref, dsk_ref,
     ng_ref, wbp_ref, wbs_ref, wo_ref) = refs[:16]
    refs = refs[16:]
    if not zero_init:
        h0_ref, refs = refs[0], refs[1:]
    o_ref, refs = refs[0], refs[1:]
    if emit_state:
        hb_ref, refs = refs[0], refs[1:]
    y_scr, h_scr = refs
    j = pl.program_id(1)

    @pl.when(j == 0)
    def _():
        h_scr[...] = jnp.zeros_like(h_scr) if zero_init else h0_ref[...]

    dsk = dsk_ref[...]
    for ci in reversed(range(CHUNKS_PER_TILE)):
        _ssd_bwd_chunk(ci, xt_ref, bc_ref, cc_ref, dt_ref, alog_ref, dsk, y1_ref, y_scr, h_scr)

    if emit_state:
        @pl.when(j == nblk - 1)
        def _():
            hb_ref[...] = h_scr[...]

    y = _rmsnorm(y_scr[...] * jax.nn.silu(z_ref[...].astype(F32)), ng_ref[...])
    gates = jax.nn.sigmoid(gt_ref[...].astype(F32))
    merged = (gates[:, :D_MODEL] * _dot(yp_ref[...].astype(BF16), wbp_ref[...])
              + gates[:, D_MODEL:] * _dot(y.astype(BF16), wbs_ref[...]))
    o_ref[...] = x_ref[...] + mod_ref[2:3, :] * _dot(merged.astype(BF16), wo_ref[...])


def _mix_bwd(x, fwd, y_pool, mod, lw, h0_b, layer, emit_state):
    b, t, _ = x.shape
    nblk = t // TB_MIX
    zero_init = h0_b is None
    z, gates, dt, xt, bcv, ccv, y1 = fwd
    mod_idx = (lambda i: i) if mod.shape[0] == b else (lambda i: 0)
    tok = lambda n: pl.BlockSpec((None, TB_MIX, n), lambda i, j: (i, nblk - 1 - j, 0))
    chunked = pl.BlockSpec((None, CHUNKS_PER_TILE, D_INNER, SSD_CHUNK), lambda i, j: (i, nblk - 1 - j, 0, 0))
    in_specs = [
        tok(D_MODEL), chunked, tok(BC_DIM), tok(BC_DIM), tok(LANES), chunked, tok(D_INNER), tok(2 * D_MODEL),
        tok(POOL_DIM),
        pl.BlockSpec((None, 3, D_MODEL), lambda i, j: (mod_idx(i), 0, 0)),
        _resident((1, LANES)),
        _resident((LANES, LANES)),
        _resident((1, D_INNER)),
        _resident((POOL_DIM, D_MODEL), layer),
        _resident((D_INNER, D_MODEL), layer),
        _resident((D_MODEL, D_MODEL), layer),
    ]
    args = [x, xt, bcv, ccv, dt, y1, z, gates, y_pool, mod, lw["a_log"], lw["d_skip"], lw["ssd_norm"],
            lw["w_branch_pool"], lw["w_branch_ssd"], lw["w_out"]]
    if not zero_init:
        in_specs.append(pl.BlockSpec((None, None, D_INNER, SSD_STATE), lambda i, j: (i, layer, 0, 0)))
        args.append(h0_b)
    out_specs = [tok(D_MODEL)]
    out_shape = [jax.ShapeDtypeStruct((b, t, D_MODEL), F32)]
    if emit_state:
        out_specs.append(pl.BlockSpec((None, D_INNER, SSD_STATE), lambda i, j: (i, 0, 0)))
        out_shape.append(jax.ShapeDtypeStruct((b, D_INNER, SSD_STATE), F32))
    return pl.pallas_call(
        functools.partial(_mixb_kernel, zero_init=zero_init, emit_state=emit_state, nblk=nblk),
        grid=(b, nblk), in_specs=in_specs, out_specs=out_specs, out_shape=out_shape,
        scratch_shapes=[pltpu.VMEM((TB_MIX, D_INNER), F32), pltpu.VMEM((D_INNER, SSD_STATE), F32)],
        compiler_params=_params(("parallel", "arbitrary")),
        name="mix_bwd",
    )(*args)


def _layer(x, mod, rows, h0_f, h0_b, layer, lw, emit_state, final_g):
    ng = lw["norm_g"]
    x = _ffn(x, mod[:, 0:3], ng[0:1], lw["ffn1_w13"], lw["ffn1_w2"], layer)
    mod2 = mod[:, 3:6]
    fwd = _mix_fwd(x, mod2, ng[1:2], lw, h0_f, layer, emit_state)
    hf = fwd[8] if emit_state else None
    y_pool = _pool_mixer(fwd[0], lw["pool_w"], lw["pool_scale"], rows, layer)
    res = _mix_bwd(x, fwd[1:8], y_pool, mod2, lw, h0_b, layer, emit_state)
    x, hb = (res[0], res[1]) if emit_state else (res[0], None)
    x = _ffn(x, mod[:, 6:9], ng[2:3], lw["ffn2_w13"], lw["ffn2_w2"], layer, final_g)
    return x, hf, hb


def _prep_weights(p):
    s3 = W_MAIN_COLS
    s4 = s3 + 2 * SSD_HEADS
    w_in = p["w_in"]
    w_tail = jnp.concatenate(
        [jnp.pad(w_in[:, :, s3:s4], ((0, 0), (0, 0), (0, LANES - 2 * SSD_HEADS))), w_in[:, :, s4:]], axis=2)
    stacked = {k: p[k].astype(BF16) for k in ("ffn1_w13", "ffn1_w2", "ffn2_w13", "ffn2_w2", "w_in", "pool_w",
                                              "w_branch_pool", "w_branch_ssd", "w_out")}
    stacked["w_tail"] = w_tail.astype(BF16)
    return stacked


def _prep_layer(l, p, stacked):
    pad_lanes = lambda v: jnp.pad(v.reshape(1, -1), ((0, 0), (0, LANES - v.size)))
    return dict(
        stacked,
        norm_g=p["norm_g"][l],
        dt_bias=pad_lanes(p["dt_bias"][l]),
        a_log=pad_lanes(p["a_log"][l]),
        d_skip=jnp.broadcast_to(pad_lanes(p["d_skip"][l]).reshape(LANES, 1), (LANES, LANES)),
        pool_scale=p["pool_scale"][l].reshape(1, POOL_DIM),
        conv_w=p["conv_w"][l], conv_b=p["conv_b"][l].reshape(1, CONV_DIM),
        ssd_norm=p["ssd_norm"][l].reshape(1, D_INNER),
    )


def kernel(x_prompt, x_sample, state_ssd_fwd, state_ssd_bwd, c, c_ctx, ada_w, ada_b, norm_g, ffn1_w13, ffn1_w2, ffn2_w13, ffn2_w2, w_in, pool_w, pool_scale, conv_w, conv_b, a_log, dt_bias, d_skip, ssd_norm, w_branch_pool, w_branch_ssd, w_out, final_norm):
    p = dict(norm_g=norm_g, ffn1_w13=ffn1_w13, ffn1_w2=ffn1_w2, ffn2_w13=ffn2_w13, ffn2_w2=ffn2_w2,
             w_in=w_in, pool_w=pool_w, pool_scale=pool_scale, conv_w=conv_w, conv_b=conv_b, a_log=a_log,
             dt_bias=dt_bias, d_skip=d_skip, ssd_norm=ssd_norm, w_branch_pool=w_branch_pool,
             w_branch_ssd=w_branch_ssd, w_out=w_out)
    nb = c.shape[0]
    rows = x_sample.shape[1] // GRID_W
    cond = jnp.concatenate([c_ctx[None, :], c], axis=0)
    cond = jnp.pad(cond, ((0, -(nb + 1) % SUBLANES), (0, 0)))
    mod = _adaln(cond, ada_w, ada_b).reshape(DEPTH, cond.shape[0], N_MOD, D_MODEL)
    fg = final_norm.reshape(1, D_MODEL)
    sd = state_ssd_fwd.shape
    h0_f = state_ssd_fwd.reshape(sd[0], DEPTH, D_INNER, SSD_STATE)
    h0_b = state_ssd_bwd.reshape(sd[0], DEPTH, D_INNER, SSD_STATE)
    xc, xl = x_prompt, x_sample
    new_f, new_b = [], []
    stacked = _prep_weights(p)
    for l in range(DEPTH):
        lw = _prep_layer(l, p, stacked)
        last = fg if l == DEPTH - 1 else None
        xc, hf, hb = _layer(xc, mod[l, 0:1], None, None, None, l, lw, True, last)
        new_f.append(hf.reshape(-1, SSD_HEADS, SSD_HEAD_DIM, SSD_STATE))
        new_b.append(hb.reshape(-1, SSD_HEADS, SSD_HEAD_DIM, SSD_STATE))
        xl, _, _ = _layer(xl, mod[l, 1:1 + nb], rows, h0_f, h0_b, l, lw, False, last)
    return (xc, xl, jnp.stack(new_f, axis=1), jnp.stack(new_b, axis=1))
```

```python
import functools
import math

import jax
import jax.numpy as jnp
from jax import lax
from jax.experimental import pallas as pl
from jax.experimental.pallas import tpu as pltpu

F32 = jnp.float32
BF16 = jnp.bfloat16

D_MODEL = 1024
DEPTH = 2
GRID_W = 64
POOL_WINDOWS = (2, 4, 8, 16)
POOL_GROUPS = len(POOL_WINDOWS)
POOL_DIM = D_MODEL
POOL_GROUP_DIM = POOL_DIM // POOL_GROUPS
D_INNER = 2 * D_MODEL
SSD_HEAD_DIM = 64
SSD_HEADS = D_INNER // SSD_HEAD_DIM
SSD_GROUPS = 4
HEADS_PER_GROUP = SSD_HEADS // SSD_GROUPS
GROUP_DIM = HEADS_PER_GROUP * SSD_HEAD_DIM
SSD_STATE = 128
SSD_CONV = 4
SSD_CHUNK = 128
BC_DIM = SSD_GROUPS * SSD_STATE
CONV_DIM = D_INNER + 2 * BC_DIM
W_Z_AT = POOL_DIM
W_XBC_AT = W_Z_AT + D_INNER
W_MAIN_COLS = W_XBC_AT + CONV_DIM
D_FF = 2816
N_MOD = 9
EPS = 1e-6
LOG2E = math.log2(math.e)

LANES = 128
SUBLANES = 8
VMEM_LIMIT = 56 * 1024 * 1024

TM_FFN = 512
FF_CHUNK = 256
TB_MIX = 256
CHUNKS_PER_TILE = TB_MIX // SSD_CHUNK
CONV_TILE = BC_DIM
CONV_HALO = 16
POOL_BLOCK = 256
POOL_PAD_ROWS = 8
POOL_SEQ_PER_STEP = 4
POOL_UNROLL = 8
ADA_TN = 2304
PROJ_COLS = 256
SCAN_SCHEDULE = (0,) + (1,) * 7 + (3,) * 9 + (2,) * 4


def _dot(a, b):
    return jnp.dot(a, b, preferred_element_type=F32)


def _dot_nt(a, b):
    return lax.dot_general(a, b, (((1,), (1,)), ((), ())), preferred_element_type=F32)


def _rmsnorm(x, g):
    return x * lax.rsqrt(jnp.mean(x * x, axis=-1, keepdims=True) + EPS) * g


def _params(sem):
    return pltpu.CompilerParams(dimension_semantics=sem, vmem_limit_bytes=VMEM_LIMIT)


def _resident(shape, layer=None):
    nd = len(shape)
    if layer is None:
        return pl.BlockSpec(shape, lambda *_: (0,) * nd, pipeline_mode=pl.Buffered(1))
    return pl.BlockSpec((None,) + tuple(shape), lambda *_: (layer,) + (0,) * nd, pipeline_mode=pl.Buffered(1))


def _ada_kernel(cond_ref, w_ref, b_ref, o_ref):
    s = jax.nn.silu(cond_ref[...]).astype(BF16)
    o_ref[...] = _dot(s, w_ref[...].astype(BF16)) + b_ref[...]


def _adaln(cond, ada_w, ada_b):
    n = cond.shape[0]
    ncol = N_MOD * D_MODEL
    return pl.pallas_call(
        _ada_kernel,
        grid=(DEPTH, ncol // ADA_TN),
        in_specs=[
            pl.BlockSpec((n, D_MODEL), lambda l, j: (0, 0)),
            pl.BlockSpec((None, D_MODEL, ADA_TN), lambda l, j: (l, 0, j)),
            pl.BlockSpec((None, 1, ADA_TN), lambda l, j: (l, 0, j)),
        ],
        out_specs=pl.BlockSpec((None, n, ADA_TN), lambda l, j: (l, 0, j)),
        out_shape=jax.ShapeDtypeStruct((DEPTH, n, ncol), F32),
        compiler_params=_params(("parallel", "parallel")),
        name="adaln",
    )(cond, ada_w, ada_b.reshape(DEPTH, 1, ncol))


def _ffn_kernel(x_ref, mod_ref, g_ref, w13_ref, w2_ref, *rest, final):
    if final:
        fg_ref, o_ref, a_scr = rest
    else:
        o_ref, a_scr = rest
    x = x_ref[...]
    m = mod_ref[...]
    h = (_rmsnorm(x, g_ref[...]) * (1.0 + m[1:2]) + m[0:1]).astype(BF16)
    for j in range(D_FF // FF_CHUNK):
        lo, hi = j * FF_CHUNK, (j + 1) * FF_CHUNK
        gate = _dot(h, w13_ref[:, lo:hi])
        up = _dot(h, w13_ref[:, D_FF + lo:D_FF + hi])
        a_scr[:, lo:hi] = (jax.nn.silu(gate) * up).astype(BF16)
    y = _dot(a_scr[...], w2_ref[...])
    out = x + 0.5 * m[2:3] * y
    if final:
        out = _rmsnorm(out, fg_ref[...])
    o_ref[...] = out


def _ffn(x, mod, g, w13, w2, layer, final_g=None):
    b, t, _ = x.shape
    nt = b * t
    tpm = nt // mod.shape[0]
    final = final_g is not None
    in_specs = [
        pl.BlockSpec((TM_FFN, D_MODEL), lambda i: (i, 0)),
        pl.BlockSpec((None, 3, D_MODEL), lambda i: (i * TM_FFN // tpm, 0, 0)),
        _resident((1, D_MODEL)),
        _resident((D_MODEL, 2 * D_FF), layer),
        _resident((D_FF, D_MODEL), layer),
    ]
    args = [x.reshape(nt, D_MODEL), mod, g, w13, w2]
    if final:
        in_specs.append(_resident((1, D_MODEL)))
        args.append(final_g)
    out = pl.pallas_call(
        functools.partial(_ffn_kernel, final=final),
        grid=(nt // TM_FFN,),
        in_specs=in_specs,
        out_specs=pl.BlockSpec((TM_FFN, D_MODEL), lambda i: (i, 0)),
        out_shape=jax.ShapeDtypeStruct((nt, D_MODEL), F32),
        scratch_shapes=[pltpu.VMEM((TM_FFN, D_FF), BF16)],
        compiler_params=_params(("parallel",)),
        name="ffn_final" if final else "ffn",
    )(*args)
    return out.reshape(b, t, D_MODEL)


def _window_count(idx, w, n):
    lo = jnp.maximum(idx - w // 2, 0)
    hi = jnp.minimum(idx - w // 2 + w, n)
    return (hi - lo).astype(F32)


def _band_sum(band, u):
    hi = u.astype(BF16)
    lo = (u - hi.astype(F32)).astype(BF16)
    return _dot(band, hi) + _dot(band, lo)


def _pool_finish(tot, cnt, u, pw_ref, ps_ref):
    d = (tot / cnt - u).astype(BF16)
    return _dot(d, pw_ref[...]) * ps_ref[...]


def _pool_seq_body(u_ref, pw_ref, ps_ref, o_ref, *, w, t):
    ti = lax.broadcasted_iota(jnp.int32, (t, t), 0)
    si = lax.broadcasted_iota(jnp.int32, (t, t), 1)
    band = jnp.where((si >= ti - w // 2) & (si < ti - w // 2 + w), 1.0, 0.0).astype(BF16)
    cnt = _window_count(lax.broadcasted_iota(jnp.int32, (t, POOL_GROUP_DIM), 0), w, t)
    for s in range(u_ref.shape[0]):
        u = u_ref[s]
        o_ref[s] = _pool_finish(_band_sum(band, u), cnt, u, pw_ref, ps_ref)


def _pool_grid_body(u_ref, pw_ref, ps_ref, o_ref, cp_ref, *, w, rows):
    nblk = rows * GRID_W // POOL_BLOCK
    unroll = math.gcd(nblk, POOL_UNROLL)
    pad = POOL_PAD_ROWS * GRID_W
    ti = lax.broadcasted_iota(jnp.int32, (POOL_BLOCK, POOL_BLOCK), 0)
    si = lax.broadcasted_iota(jnp.int32, (POOL_BLOCK, POOL_BLOCK), 1)
    tc = ti & (GRID_W - 1)
    sc = si & (GRID_W - 1)
    same_row = (ti >> 6) == (si >> 6)
    band = jnp.where(same_row & (sc >= tc - w // 2) & (sc < tc - w // 2 + w), 1.0, 0.0).astype(BF16)
    cp_ref[0:pad + GRID_W, :] = jnp.zeros((pad + GRID_W, POOL_GROUP_DIM), F32)
    rows_per_block = POOL_BLOCK // GRID_W

    def col_pool(i, acc):
        off = pl.multiple_of(i * POOL_BLOCK, POOL_BLOCK)
        cp = _band_sum(band, u_ref[pl.ds(off, POOL_BLOCK), :])
        prefix = []
        for k in range(rows_per_block):
            acc = acc + cp[k * GRID_W:(k + 1) * GRID_W, :]
            prefix.append(acc)
        cp_ref[pl.ds(pad + GRID_W + off, POOL_BLOCK), :] = jnp.concatenate(prefix, axis=0)
        return acc

    total = lax.fori_loop(0, nblk, col_pool, jnp.zeros((GRID_W, POOL_GROUP_DIM), F32), unroll=unroll)
    for k in range(1, POOL_PAD_ROWS):
        cp_ref[pad + (rows + k) * GRID_W:pad + (rows + k + 1) * GRID_W, :] = total

    tt = lax.broadcasted_iota(jnp.int32, (POOL_BLOCK, POOL_GROUP_DIM), 0)
    ccnt = _window_count(tt & (GRID_W - 1), w, GRID_W)

    def row_pool(i, carry):
        off = pl.multiple_of(i * POOL_BLOCK, POOL_BLOCK)
        tot = (cp_ref[pl.ds(pad + off + (w // 2) * GRID_W, POOL_BLOCK), :]
               - cp_ref[pl.ds(pad + off - (w // 2) * GRID_W, POOL_BLOCK), :])
        r = i * rows_per_block + (tt >> 6)
        cnt = _window_count(r, w, rows) * ccnt
        o_ref[pl.ds(off, POOL_BLOCK), :] = _pool_finish(tot, cnt, u_ref[pl.ds(off, POOL_BLOCK), :], pw_ref, ps_ref)
        return carry

    lax.fori_loop(0, nblk, row_pool, 0, unroll=unroll)


def _pool_kernel(u_ref, pw_ref, ps_ref, o_ref, *scratch, rows, t):
    g = pl.program_id(1)
    for gi, w in enumerate(POOL_WINDOWS):
        @pl.when(g == gi)
        def _(w=w):
            if rows is None:
                _pool_seq_body(u_ref, pw_ref, ps_ref, o_ref, w=w, t=t)
            else:
                _pool_grid_body(u_ref, pw_ref, ps_ref, o_ref, scratch[0], w=w, rows=rows)


def _pool_mixer(u, pool_w, pool_scale, rows, layer):
    b, t, _ = u.shape
    if rows is None:
        per_step = math.gcd(b, POOL_SEQ_PER_STEP)
        scratch = []
        blk = pl.BlockSpec((per_step, t, POOL_GROUP_DIM), lambda i, g: (i, 0, g))
    else:
        per_step = 1
        scratch = [pltpu.VMEM(((rows + 2 * POOL_PAD_ROWS) * GRID_W, POOL_GROUP_DIM), F32)]
        blk = pl.BlockSpec((None, t, POOL_GROUP_DIM), lambda i, g: (i, 0, g))
    return pl.pallas_call(
        functools.partial(_pool_kernel, rows=rows, t=t),
        grid=(b // per_step, POOL_GROUPS),
        in_specs=[
            blk,
            pl.BlockSpec((None, None, POOL_GROUP_DIM, POOL_GROUP_DIM), lambda i, g: (layer, g, 0, 0)),
            pl.BlockSpec((1, POOL_GROUP_DIM), lambda i, g: (0, g)),
        ],
        out_specs=blk,
        out_shape=jax.ShapeDtypeStruct((b, t, POOL_DIM), F32),
        scratch_shapes=scratch,
        compiler_params=_params(("parallel", "parallel")),
        name="pool_seq" if rows is None else "pool_grid",
    )(u, pool_w, pool_scale)


def _ssd_chunk_terms(dtc, alog):
    n = SSD_CHUNK
    da = dtc * (-jnp.exp(alog) * LOG2E)
    row = lax.broadcasted_iota(jnp.int32, da.shape, 0)
    cs = da
    rcs = da
    k = 1
    while k < n:
        cs = cs + jnp.where(row >= k, pltpu.roll(cs, k, 0), 0.0)
        rcs = rcs + jnp.where(row < n - k, pltpu.roll(rcs, n - k, 0), 0.0)
        k *= 2
    logdt = jnp.log(dtc) * LOG2E
    return dict(colf=cs - logdt, colb=rcs - logdt, cst=cs.T, rcst=rcs.T, dtt=dtc.T)


def _rows64(mat, h):
    return jnp.broadcast_to(mat[h:h + 1, :], (SSD_HEAD_DIM, mat.shape[1]))


def _pair_rows(mat, h0):
    return jnp.concatenate([_rows64(mat, h0), _rows64(mat, h0 + 1)], axis=0)


def _state_update(h_scr, g, xw, bc_g, dec_t, head_off):
    rows = slice(g * GROUP_DIM, (g + 1) * GROUP_DIM)
    decay = jnp.concatenate(
        [_rows64(dec_t, head_off + g * HEADS_PER_GROUP + h) for h in range(HEADS_PER_GROUP)], axis=0)
    h_scr[rows, :] = h_scr[rows, :] * decay + _dot(xw, bc_g)


def _ssd_fwd_chunk(ci, xt_ref, bc_ref, cc_ref, dt_ref, alog_ref, y1_ref, h_scr):
    n = SSD_CHUNK
    tok = slice(ci * n, (ci + 1) * n)
    tm = _ssd_chunk_terms(dt_ref[tok, :], alog_ref[...])
    cst, rcst, dtt = tm["cst"], tm["rcst"], tm["dtt"]
    tot = cst[:, n - 1:n]
    eoff_t = jnp.exp2(cst)
    w_t = dtt * jnp.exp2(tot - cst)
    dec_t = jnp.exp2(jnp.broadcast_to(tot, (LANES, SSD_STATE)))
    si = lax.broadcasted_iota(jnp.int32, (n, n), 0)
    li = lax.broadcasted_iota(jnp.int32, (n, n), 1)
    src_before = si <= li
    src_after = si >= li
    zeros64 = jnp.zeros((SSD_HEAD_DIM, n), BF16)
    yield
    for g in range(SSD_GROUPS):
        bc_g = bc_ref[tok, g * SSD_STATE:(g + 1) * SSD_STATE]
        cc_g = cc_ref[tok, g * SSD_STATE:(g + 1) * SSD_STATE]
        sc_t = _dot_nt(bc_g, cc_g)
        y_off = _dot_nt(h_scr[g * GROUP_DIM:(g + 1) * GROUP_DIM, :].astype(BF16), cc_g)
        xw = []
        for j in range(HEADS_PER_GROUP // 2):
            h0 = g * HEADS_PER_GROUP + 2 * j
            mts = []
            for h in (h0, h0 + 1):
                seg_f = cst[h:h + 1, :] - tm["colf"][:, h:h + 1]
                hb = SSD_HEADS + h
                seg_b = rcst[hb:hb + 1, :] - tm["colb"][:, hb:hb + 1]
                dec = (jnp.exp2(jnp.where(src_before, seg_f, -jnp.inf))
                       + jnp.exp2(jnp.where(src_after, seg_b, -jnp.inf)))
                mts.append((sc_t * dec).astype(BF16))
            r0 = h0 * SSD_HEAD_DIM
            xp = xt_ref[ci, r0:r0 + 2 * SSD_HEAD_DIM, :]
            lhs = jnp.concatenate([jnp.concatenate([xp[:SSD_HEAD_DIM], zeros64], axis=0),
                                   jnp.concatenate([zeros64, xp[SSD_HEAD_DIM:]], axis=0)], axis=1)
            y_diag = _dot(lhs, jnp.concatenate(mts, axis=0))
            yo = y_off[2 * j * SSD_HEAD_DIM:(2 * j + 2) * SSD_HEAD_DIM, :]
            y1_ref[ci, r0:r0 + 2 * SSD_HEAD_DIM, :] = (y_diag + yo * _pair_rows(eoff_t, h0)).astype(y1_ref.dtype)
            xw.append((xp.astype(F32) * _pair_rows(w_t, h0)).astype(BF16))
            yield
        _state_update(h_scr, g, jnp.concatenate(xw, axis=0), bc_g, dec_t, 0)
        yield


def _ssd_bwd_chunk(ci, xt_ref, bc_ref, cc_ref, dt_ref, alog_ref, dsk, y1_ref, y_scr, h_scr):
    n = SSD_CHUNK
    tok = slice(ci * n, (ci + 1) * n)
    tm = _ssd_chunk_terms(dt_ref[tok, :], alog_ref[...])
    rcst, dtt = tm["rcst"], tm["dtt"]
    tot = rcst[:, 0:1]
    eoff_t = jnp.exp2(rcst)
    w_t = dtt * jnp.exp2(tot - rcst)
    dec_t = jnp.exp2(jnp.broadcast_to(tot, (LANES, SSD_STATE)))
    for g in range(SSD_GROUPS):
        bc_g = bc_ref[tok, g * SSD_STATE:(g + 1) * SSD_STATE]
        cc_g = cc_ref[tok, g * SSD_STATE:(g + 1) * SSD_STATE]
        y_off = _dot_nt(h_scr[g * GROUP_DIM:(g + 1) * GROUP_DIM, :].astype(BF16), cc_g)
        xw, yt = [], []
        for j in range(HEADS_PER_GROUP // 2):
            h0 = g * HEADS_PER_GROUP + 2 * j
            r0 = h0 * SSD_HEAD_DIM
            xp = xt_ref[ci, r0:r0 + 2 * SSD_HEAD_DIM, :].astype(F32)
            yo = y_off[2 * j * SSD_HEAD_DIM:(2 * j + 2) * SSD_HEAD_DIM, :]
            yt.append(y1_ref[ci, r0:r0 + 2 * SSD_HEAD_DIM, :].astype(F32)
                      + yo * _pair_rows(eoff_t, SSD_HEADS + h0) + _pair_rows(dsk, h0) * xp)
            xw.append((xp * _pair_rows(w_t, SSD_HEADS + h0)).astype(BF16))
        y_scr[tok, g * GROUP_DIM:(g + 1) * GROUP_DIM] = jnp.concatenate(yt, axis=0).T
        _state_update(h_scr, g, jnp.concatenate(xw, axis=0), bc_g, dec_t, SSD_HEADS)


def _project_and_conv(j, nblk, x_ref, xp_ref, xn_ref, mod_ref, g_ref, wm_ref, wt_ref,
                      dtb_ref, cw_ref, cb_ref, up_ref, z_ref, gt_ref, dt_ref, xt_ref, bc_ref, cc_ref,
                      hm_scr, ext_refs):
    m = mod_ref[...]
    g = g_ref[...]

    def modulated(x_rows, keep=True):
        hm = _rmsnorm(x_rows, g) * (1.0 + m[1:2]) + m[0:1]
        return jnp.where(keep, hm, 0.0).astype(BF16)

    tile = slice(CONV_HALO, CONV_HALO + TB_MIX)
    hm_scr[0:CONV_HALO, :] = modulated(xp_ref[...], j > 0)
    hm_scr[tile, :] = modulated(x_ref[...])
    hm_scr[CONV_HALO + TB_MIX:2 * CONV_HALO + TB_MIX, :] = modulated(xn_ref[...], j < nblk - 1)
    yield
    left = SSD_CONV // 2
    ct = CONV_TILE
    n_ct = CONV_DIM // ct
    n_ext = TB_MIX + 2 * CONV_HALO
    conv_base = SUBLANES + CONV_HALO

    def project_xbc(c):
        xbc = _dot(hm_scr[...], wm_ref[:, W_XBC_AT + c * ct:W_XBC_AT + (c + 1) * ct])
        for k in range(SSD_CONV):
            ext_refs[c][k, SUBLANES + left - k:SUBLANES + left - k + n_ext, :] = xbc

    def conv(c):
        cols = slice(c * ct, (c + 1) * ct)
        acc = cb_ref[:, cols] + ext_refs[c][0, conv_base:conv_base + TB_MIX, :] * cw_ref[0:1, cols]
        for k in range(1, SSD_CONV):
            acc = acc + ext_refs[c][k, conv_base:conv_base + TB_MIX, :] * cw_ref[k:k + 1, cols]
        y = jax.nn.silu(acc)
        if c < D_INNER // ct:
            for ci in range(CHUNKS_PER_TILE):
                xt_ref[ci, cols, :] = y[ci * SSD_CHUNK:(ci + 1) * SSD_CHUNK, :].T.astype(BF16)
        elif c == D_INNER // ct:
            bc_ref[...] = y.astype(BF16)
        else:
            cc_ref[...] = y.astype(BF16)

    project_xbc(0)
    yield
    for c in range(n_ct):
        if c + 1 < n_ct:
            project_xbc(c + 1)
        if c < D_INNER // ct:
            zc = slice(c * ct, (c + 1) * ct)
            z_ref[:, zc] = _dot(hm_scr[tile, :], wm_ref[:, W_Z_AT + c * ct:W_Z_AT + (c + 1) * ct]).astype(z_ref.dtype)
        conv(c)
        yield
    dt_ref[...] = jax.nn.softplus(_dot(hm_scr[tile, :], wt_ref[:, 0:LANES]) + dtb_ref[...])
    yield
    for c in range(2 * D_MODEL // PROJ_COLS):
        gc = slice(c * PROJ_COLS, (c + 1) * PROJ_COLS)
        gt_ref[:, gc] = _dot(hm_scr[tile, :],
                             wt_ref[:, LANES + c * PROJ_COLS:LANES + (c + 1) * PROJ_COLS]).astype(gt_ref.dtype)
        yield
    for c in range(POOL_DIM // PROJ_COLS):
        pc = slice(c * PROJ_COLS, (c + 1) * PROJ_COLS)
        up_ref[:, pc] = _dot(hm_scr[tile, :], wm_ref[:, pc])
        yield


def _mixf_kernel(*refs, zero_init, emit_state, nblk, ntiles):
    x_ref, xp_ref, xn_ref, mod_ref, g_ref, wm_ref, wt_ref, dtb_ref, cw_ref, cb_ref, alog_ref = refs[:11]
    refs = refs[11:]
    if not zero_init:
        h0_ref, refs = refs[0], refs[1:]
    up_ref, z_ref, gt_ref, dt_ref, xt_ref, bc_ref, cc_ref, y1_ref = refs[:8]
    refs = refs[8:]
    if emit_state:
        hf_ref, refs = refs[0], refs[1:]
    hm_scr, xt_cur, bc_cur, cc_cur, dt_cur, h_scr = refs[:6]
    ext_refs = refs[6:]
    s = pl.program_id(0)
    ja = lax.rem(jnp.minimum(s, ntiles - 1), nblk)
    jb = lax.rem(jnp.maximum(s - 1, 0), nblk)

    @pl.when(s == 0)
    def _():
        xt_cur[...] = jnp.zeros_like(xt_cur)
        bc_cur[...] = jnp.zeros_like(bc_cur)
        cc_cur[...] = jnp.zeros_like(cc_cur)
        dt_cur[...] = jnp.ones_like(dt_cur)

    @pl.when(jb == 0)
    def _():
        h_scr[...] = jnp.zeros_like(h_scr) if zero_init else h0_ref[...]

    proj = _project_and_conv(ja, nblk, x_ref, xp_ref, xn_ref, mod_ref, g_ref, wm_ref, wt_ref,
                             dtb_ref, cw_ref, cb_ref, up_ref, z_ref, gt_ref, dt_ref, xt_ref, bc_ref,
                             cc_ref, hm_scr, ext_refs)
    scan = (None for ci in range(CHUNKS_PER_TILE)
            for _ in _ssd_fwd_chunk(ci, xt_cur, bc_cur, cc_cur, dt_cur, alog_ref, y1_ref, h_scr))
    for n_scan in SCAN_SCHEDULE:
        next(proj, None)
        for _ in range(n_scan):
            next(scan, None)
    for _ in proj:
        pass
    for _ in scan:
        pass

    if emit_state:
        @pl.when(jb == nblk - 1)
        def _():
            hf_ref[...] = h_scr[...]

    xt_cur[...] = xt_ref[...]
    bc_cur[...] = bc_ref[...]
    cc_cur[...] = cc_ref[...]
    dt_cur[...] = dt_ref[...]


def _mix_fwd(x, mod, g, lw, h0_f, layer, emit_state):
    b, t, _ = x.shape
    nblk = t // TB_MIX
    ntiles = b * nblk
    nc = t // SSD_CHUNK
    per = TB_MIX // CONV_HALO
    zero_init = h0_f is None
    mod_idx = (lambda i: i) if mod.shape[0] == b else (lambda i: 0)
    ta = lambda s: jnp.minimum(s, ntiles - 1)
    tb = lambda s: jnp.maximum(s - 1, 0)
    tok = lambda n: pl.BlockSpec((None, TB_MIX, n), lambda s: (ta(s) // nblk, ta(s) % nblk, 0))
    chunked = lambda tile: pl.BlockSpec((None, CHUNKS_PER_TILE, D_INNER, SSD_CHUNK),
                                        lambda s: (tile(s) // nblk, tile(s) % nblk, 0, 0))
    in_specs = [
        tok(D_MODEL),
        pl.BlockSpec((None, CONV_HALO, D_MODEL),
                     lambda s: (ta(s) // nblk, jnp.maximum((ta(s) % nblk) * per - 1, 0), 0)),
        pl.BlockSpec((None, CONV_HALO, D_MODEL),
                     lambda s: (ta(s) // nblk, jnp.minimum((ta(s) % nblk + 1) * per, t // CONV_HALO - 1), 0)),
        pl.BlockSpec((None, 3, D_MODEL), lambda s: (mod_idx(ta(s) // nblk), 0, 0)),
        _resident((1, D_MODEL)),
        _resident((D_MODEL, W_MAIN_COLS), layer),
        _resident((D_MODEL, LANES + 2 * D_MODEL), layer),
        _resident((1, LANES)),
        _resident((SSD_CONV, CONV_DIM)),
        _resident((1, CONV_DIM)),
        _resident((1, LANES)),
    ]
    args = [x, x, x, mod, g, lw["w_in"], lw["w_tail"], lw["dt_bias"], lw["conv_w"], lw["conv_b"], lw["a_log"]]
    if not zero_init:
        in_specs.append(pl.BlockSpec((None, None, D_INNER, SSD_STATE), lambda s: (tb(s) // nblk, layer, 0, 0)))
        args.append(h0_f)
    out_specs = [tok(POOL_DIM), tok(D_INNER), tok(2 * D_MODEL), tok(LANES), chunked(ta), tok(BC_DIM), tok(BC_DIM),
                 chunked(tb)]
    out_shape = [
        jax.ShapeDtypeStruct((b, t, POOL_DIM), F32),
        jax.ShapeDtypeStruct((b, t, D_INNER), BF16),
        jax.ShapeDtypeStruct((b, t, 2 * D_MODEL), BF16),
        jax.ShapeDtypeStruct((b, t, LANES), F32),
        jax.ShapeDtypeStruct((b, nc, D_INNER, SSD_CHUNK), BF16),
        jax.ShapeDtypeStruct((b, t, BC_DIM), BF16),
        jax.ShapeDtypeStruct((b, t, BC_DIM), BF16),
        jax.ShapeDtypeStruct((b, nc, D_INNER, SSD_CHUNK), BF16),
    ]
    if emit_state:
        out_specs.append(pl.BlockSpec((None, D_INNER, SSD_STATE), lambda s: (tb(s) // nblk, 0, 0)))
        out_shape.append(jax.ShapeDtypeStruct((b, D_INNER, SSD_STATE), F32))
    scratch = [
        pltpu.VMEM((TB_MIX + 2 * CONV_HALO, D_MODEL), BF16),
        pltpu.VMEM((CHUNKS_PER_TILE, D_INNER, SSD_CHUNK), BF16),
        pltpu.VMEM((TB_MIX, BC_DIM), BF16),
        pltpu.VMEM((TB_MIX, BC_DIM), BF16),
        pltpu.VMEM((TB_MIX, LANES), F32),
        pltpu.VMEM((D_INNER, SSD_STATE), F32),
    ] + [pltpu.VMEM((SSD_CONV, TB_MIX + 2 * CONV_HALO + 2 * SUBLANES, CONV_TILE), F32)
         for _ in range(CONV_DIM // CONV_TILE)]
    return pl.pallas_call(
        functools.partial(_mixf_kernel, zero_init=zero_init, emit_state=emit_state, nblk=nblk, ntiles=ntiles),
        grid=(ntiles + 1,), in_specs=in_specs, out_specs=out_specs, out_shape=out_shape,
        scratch_shapes=scratch,
        compiler_params=_params(("arbitrary",)),
        name="mix_fwd",
    )(*args)


def _mixb_kernel(*refs, zero_init, emit_state, nblk):
    (x_ref, xt_ref, bc_ref, cc_ref, dt_ref, y1_ref, z_ref, gt_ref, yp_ref, mod_ref, alog_ref, dsk_ref,
     ng_ref, wbp_ref, wbs_ref, wo_ref) = refs[:16]
    refs = refs[16:]
    if not zero_init:
        h0_ref, refs = refs[0], refs[1:]
    o_ref, refs = refs[0], refs[1:]
    if emit_state:
        hb_ref, refs = refs[0], refs[1:]
    y_scr, h_scr = refs
    j = pl.program_id(1)

    @pl.when(j == 0)
    def _():
        h_scr[...] = jnp.zeros_like(h_scr) if zero_init else h0_ref[...]

    dsk = dsk_ref[...]
    for ci in reversed(range(CHUNKS_PER_TILE)):
        _ssd_bwd_chunk(ci, xt_ref, bc_ref, cc_ref, dt_ref, alog_ref, dsk, y1_ref, y_scr, h_scr)

    if emit_state:
        @pl.when(j == nblk - 1)
        def _():
            hb_ref[...] = h_scr[...]

    y = _rmsnorm(y_scr[...] * jax.nn.silu(z_ref[...].astype(F32)), ng_ref[...])
    gates = jax.nn.sigmoid(gt_ref[...].astype(F32))
    merged = (gates[:, :D_MODEL] * _dot(yp_ref[...].astype(BF16), wbp_ref[...])
              + gates[:, D_MODEL:] * _dot(y.astype(BF16), wbs_ref[...]))
    o_ref[...] = x_ref[...] + mod_ref[2:3, :] * _dot(merged.astype(BF16), wo_ref[...])


def _mix_bwd(x, fwd, y_pool, mod, lw, h0_b, layer, emit_state):
    b, t, _ = x.shape
    nblk = t // TB_MIX
    zero_init = h0_b is None
    z, gates, dt, xt, bcv, ccv, y1 = fwd
    mod_idx = (lambda i: i) if mod.shape[0] == b else (lambda i: 0)
    tok = lambda n: pl.BlockSpec((None, TB_MIX, n), lambda i, j: (i, nblk - 1 - j, 0))
    chunked = pl.BlockSpec((None, CHUNKS_PER_TILE, D_INNER, SSD_CHUNK), lambda i, j: (i, nblk - 1 - j, 0, 0))
    in_specs = [
        tok(D_MODEL), chunked, tok(BC_DIM), tok(BC_DIM), tok(LANES), chunked, tok(D_INNER), tok(2 * D_MODEL),
        tok(POOL_DIM),
        pl.BlockSpec((None, 3, D_MODEL), lambda i, j: (mod_idx(i), 0, 0)),
        _resident((1, LANES)),
        _resident((LANES, LANES)),
        _resident((1, D_INNER)),
        _resident((POOL_DIM, D_MODEL), layer),
        _resident((D_INNER, D_MODEL), layer),
        _resident((D_MODEL, D_MODEL), layer),
    ]
    args = [x, xt, bcv, ccv, dt, y1, z, gates, y_pool, mod, lw["a_log"], lw["d_skip"], lw["ssd_norm"],
            lw["w_branch_pool"], lw["w_branch_ssd"], lw["w_out"]]
    if not zero_init:
        in_specs.append(pl.BlockSpec((None, None, D_INNER, SSD_STATE), lambda i, j: (i, layer, 0, 0)))
        args.append(h0_b)
    out_specs = [tok(D_MODEL)]
    out_shape = [jax.ShapeDtypeStruct((b, t, D_MODEL), F32)]
    if emit_state:
        out_specs.append(pl.BlockSpec((None, D_INNER, SSD_STATE), lambda i, j: (i, 0, 0)))
        out_shape.append(jax.ShapeDtypeStruct((b, D_INNER, SSD_STATE), F32))
    return pl.pallas_call(
        functools.partial(_mixb_kernel, zero_init=zero_init, emit_state=emit_state, nblk=nblk),
        grid=(b, nblk), in_specs=in_specs, out_specs=out_specs, out_shape=out_shape,
        scratch_shapes=[pltpu.VMEM((TB_MIX, D_INNER), F32), pltpu.VMEM((D_INNER, SSD_STATE), F32)],
        compiler_params=_params(("parallel", "arbitrary")),
        name="mix_bwd",
    )(*args)


def _layer(x, mod, rows, h0_f, h0_b, layer, lw, emit_state, final_g):
    ng = lw["norm_g"]
    x = _ffn(x, mod[:, 0:3], ng[0:1], lw["ffn1_w13"], lw["ffn1_w2"], layer)
    mod2 = mod[:, 3:6]
    fwd = _mix_fwd(x, mod2, ng[1:2], lw, h0_f, layer, emit_state)
    hf = fwd[8] if emit_state else None
    y_pool = _pool_mixer(fwd[0], lw["pool_w"], lw["pool_scale"], rows, layer)
    res = _mix_bwd(x, fwd[1:8], y_pool, mod2, lw, h0_b, layer, emit_state)
    x, hb = (res[0], res[1]) if emit_state else (res[0], None)
    x = _ffn(x, mod[:, 6:9], ng[2:3], lw["ffn2_w13"], lw["ffn2_w2"], layer, final_g)
    return x, hf, hb


def _prep_weights(p):
    s3 = W_MAIN_COLS
    s4 = s3 + 2 * SSD_HEADS
    w_in = p["w_in"]
    w_tail = jnp.concatenate(
        [jnp.pad(w_in[:, :, s3:s4], ((0, 0), (0, 0), (0, LANES - 2 * SSD_HEADS))), w_in[:, :, s4:]], axis=2)
    stacked = {k: p[k].astype(BF16) for k in ("ffn1_w13", "ffn1_w2", "ffn2_w13", "ffn2_w2", "pool_w",
                                              "w_branch_pool", "w_branch_ssd", "w_out")}
    stacked["w_in"] = w_in[:, :, :s3].astype(BF16)
    stacked["w_tail"] = w_tail.astype(BF16)
    return stacked


def _prep_layer(l, p, stacked):
    pad_lanes = lambda v: jnp.pad(v.reshape(1, -1), ((0, 0), (0, LANES - v.size)))
    return dict(
        stacked,
        norm_g=p["norm_g"][l],
        dt_bias=pad_lanes(p["dt_bias"][l]),
        a_log=pad_lanes(p["a_log"][l]),
        d_skip=jnp.broadcast_to(pad_lanes(p["d_skip"][l]).reshape(LANES, 1), (LANES, LANES)),
        pool_scale=p["pool_scale"][l].reshape(1, POOL_DIM),
        conv_w=p["conv_w"][l], conv_b=p["conv_b"][l].reshape(1, CONV_DIM),
        ssd_norm=p["ssd_norm"][l].reshape(1, D_INNER),
    )


def kernel(x_prompt, x_sample, state_ssd_fwd, state_ssd_bwd, c, c_ctx, ada_w, ada_b, norm_g, ffn1_w13, ffn1_w2, ffn2_w13, ffn2_w2, w_in, pool_w, pool_scale, conv_w, conv_b, a_log, dt_bias, d_skip, ssd_norm, w_branch_pool, w_branch_ssd, w_out, final_norm):
    p = dict(norm_g=norm_g, ffn1_w13=ffn1_w13, ffn1_w2=ffn1_w2, ffn2_w13=ffn2_w13, ffn2_w2=ffn2_w2,
             w_in=w_in, pool_w=pool_w, pool_scale=pool_scale, conv_w=conv_w, conv_b=conv_b, a_log=a_log,
             dt_bias=dt_bias, d_skip=d_skip, ssd_norm=ssd_norm, w_branch_pool=w_branch_pool,
             w_branch_ssd=w_branch_ssd, w_out=w_out)
    nb = c.shape[0]
    rows = x_sample.shape[1] // GRID_W
    cond = jnp.concatenate([c_ctx[None, :], c], axis=0)
    cond = jnp.pad(cond, ((0, -(nb + 1) % SUBLANES), (0, 0)))
    mod = _adaln(cond, ada_w, ada_b).reshape(DEPTH, cond.shape[0], N_MOD, D_MODEL)
    fg = final_norm.reshape(1, D_MODEL)
    sd = state_ssd_fwd.shape
    h0_f = state_ssd_fwd.reshape(sd[0], DEPTH, D_INNER, SSD_STATE)
    h0_b = state_ssd_bwd.reshape(sd[0], DEPTH, D_INNER, SSD_STATE)
    xc, xl = x_prompt, x_sample
    new_f, new_b = [], []
    stacked = _prep_weights(p)
    for l in range(DEPTH):
        lw = _prep_layer(l, p, stacked)
        last = fg if l == DEPTH - 1 else None
        xc, hf, hb = _layer(xc, mod[l, 0:1], None, None, None, l, lw, True, last)
        new_f.append(hf.reshape(-1, SSD_HEADS, SSD_HEAD_DIM, SSD_STATE))
        new_b.append(hb.reshape(-1, SSD_HEADS, SSD_HEAD_DIM, SSD_STATE))
        xl, _, _ = _layer(xl, mod[l, 1:1 + nb], rows, h0_f, h0_b, l, lw, False, last)
    return (xc, xl, jnp.stack(new_f, axis=1), jnp.stack(new_b, axis=1))
```

```python
import functools
import math

import jax
import jax.numpy as jnp
from jax import lax
from jax.experimental import pallas as pl
from jax.experimental.pallas import tpu as pltpu

F32 = jnp.float32
BF16 = jnp.bfloat16

D_MODEL = 1024
DEPTH = 2
GRID_W = 64
POOL_WINDOWS = (2, 4, 8, 16)
POOL_GROUPS = len(POOL_WINDOWS)
POOL_DIM = D_MODEL
POOL_GROUP_DIM = POOL_DIM // POOL_GROUPS
D_INNER = 2 * D_MODEL
SSD_HEAD_DIM = 64
SSD_HEADS = D_INNER // SSD_HEAD_DIM
SSD_GROUPS = 4
HEADS_PER_GROUP = SSD_HEADS // SSD_GROUPS
GROUP_DIM = HEADS_PER_GROUP * SSD_HEAD_DIM
SSD_STATE = 128
SSD_CONV = 4
SSD_CHUNK = 128
BC_DIM = SSD_GROUPS * SSD_STATE
CONV_DIM = D_INNER + 2 * BC_DIM
W_Z_AT = POOL_DIM
W_XBC_AT = W_Z_AT + D_INNER
W_MAIN_COLS = W_XBC_AT + CONV_DIM
D_FF = 2816
N_MOD = 9
EPS = 1e-6
LOG2E = math.log2(math.e)

LANES = 128
SUBLANES = 8
VMEM_LIMIT = 56 * 1024 * 1024

TM_FFN = 512
FF_CHUNK = 256
TB_MIX = 256
CHUNKS_PER_TILE = TB_MIX // SSD_CHUNK
CONV_TILE = BC_DIM
CONV_HALO = 16
POOL_BLOCK = 256
POOL_PAD_ROWS = 8
POOL_SEQ_PER_STEP = 4
POOL_UNROLL = 8
ADA_TN = 2304
PROJ_COLS = 256
SCAN_SCHEDULE = (0,) + (1,) * 7 + (3,) * 9 + (2,) * 4


def _dot(a, b):
    return jnp.dot(a, b, preferred_element_type=F32)


def _dot_nt(a, b):
    return lax.dot_general(a, b, (((1,), (1,)), ((), ())), preferred_element_type=F32)


def _rmsnorm(x, g):
    return x * lax.rsqrt(jnp.mean(x * x, axis=-1, keepdims=True) + EPS) * g


def _params(sem):
    return pltpu.CompilerParams(dimension_semantics=sem, vmem_limit_bytes=VMEM_LIMIT)


def _resident(shape, layer=None):
    nd = len(shape)
    if layer is None:
        return pl.BlockSpec(shape, lambda *_: (0,) * nd, pipeline_mode=pl.Buffered(1))
    return pl.BlockSpec((None,) + tuple(shape), lambda *_: (layer,) + (0,) * nd, pipeline_mode=pl.Buffered(1))


def _ada_kernel(cond_ref, w_ref, b_ref, o_ref):
    s = jax.nn.silu(cond_ref[...]).astype(BF16)
    o_ref[...] = _dot(s, w_ref[...].astype(BF16)) + b_ref[...]


def _adaln(cond, ada_w, ada_b):
    n = cond.shape[0]
    ncol = N_MOD * D_MODEL
    return pl.pallas_call(
        _ada_kernel,
        grid=(DEPTH, ncol // ADA_TN),
        in_specs=[
            pl.BlockSpec((n, D_MODEL), lambda l, j: (0, 0)),
            pl.BlockSpec((None, D_MODEL, ADA_TN), lambda l, j: (l, 0, j)),
            pl.BlockSpec((None, 1, ADA_TN), lambda l, j: (l, 0, j)),
        ],
        out_specs=pl.BlockSpec((None, n, ADA_TN), lambda l, j: (l, 0, j)),
        out_shape=jax.ShapeDtypeStruct((DEPTH, n, ncol), F32),
        compiler_params=_params(("parallel", "parallel")),
        name="adaln",
    )(cond, ada_w, ada_b.reshape(DEPTH, 1, ncol))


def _ffn_kernel(x_ref, mod_ref, g_ref, w13_ref, w2_ref, *rest, final):
    if final:
        fg_ref, o_ref, a_scr = rest
    else:
        o_ref, a_scr = rest
    x = x_ref[...]
    m = mod_ref[...]
    h = (_rmsnorm(x, g_ref[...]) * (1.0 + m[1:2]) + m[0:1]).astype(BF16)
    for j in range(D_FF // FF_CHUNK):
        lo, hi = j * FF_CHUNK, (j + 1) * FF_CHUNK
        gate = _dot(h, w13_ref[:, lo:hi])
        up = _dot(h, w13_ref[:, D_FF + lo:D_FF + hi])
        a_scr[:, lo:hi] = (jax.nn.silu(gate) * up).astype(BF16)
    y = _dot(a_scr[...], w2_ref[...])
    out = x + 0.5 * m[2:3] * y
    if final:
        out = _rmsnorm(out, fg_ref[...])
    o_ref[...] = out


def _ffn(x, mod, g, w13, w2, layer, final_g=None):
    b, t, _ = x.shape
    nt = b * t
    tpm = nt // mod.shape[0]
    final = final_g is not None
    in_specs = [
        pl.BlockSpec((TM_FFN, D_MODEL), lambda i: (i, 0)),
        pl.BlockSpec((None, 3, D_MODEL), lambda i: (i * TM_FFN // tpm, 0, 0)),
        _resident((1, D_MODEL)),
        _resident((D_MODEL, 2 * D_FF), layer),
        _resident((D_FF, D_MODEL), layer),
    ]
    args = [x.reshape(nt, D_MODEL), mod, g, w13, w2]
    if final:
        in_specs.append(_resident((1, D_MODEL)))
        args.append(final_g)
    out = pl.pallas_call(
        functools.partial(_ffn_kernel, final=final),
        grid=(nt // TM_FFN,),
        in_specs=in_specs,
        out_specs=pl.BlockSpec((TM_FFN, D_MODEL), lambda i: (i, 0)),
        out_shape=jax.ShapeDtypeStruct((nt, D_MODEL), F32),
        scratch_shapes=[pltpu.VMEM((TM_FFN, D_FF), BF16)],
        compiler_params=_params(("parallel",)),
        name="ffn_final" if final else "ffn",
    )(*args)
    return out.reshape(b, t, D_MODEL)


def _window_count(idx, w, n):
    lo = jnp.maximum(idx - w // 2, 0)
    hi = jnp.minimum(idx - w // 2 + w, n)
    return (hi - lo).astype(F32)


def _band_sum(band, u):
    hi = u.astype(BF16)
    lo = (u - hi.astype(F32)).astype(BF16)
    return _dot(band, hi) + _dot(band, lo)


def _pool_finish(tot, cnt, u, pw_ref, ps_ref):
    d = (tot / cnt - u).astype(BF16)
    return _dot(d, pw_ref[...]) * ps_ref[...]


def _pool_seq_body(u_ref, pw_ref, ps_ref, o_ref, *, w, t):
    ti = lax.broadcasted_iota(jnp.int32, (t, t), 0)
    si = lax.broadcasted_iota(jnp.int32, (t, t), 1)
    band = jnp.where((si >= ti - w // 2) & (si < ti - w // 2 + w), 1.0, 0.0).astype(BF16)
    cnt = _window_count(lax.broadcasted_iota(jnp.int32, (t, POOL_GROUP_DIM), 0), w, t)
    for s in range(u_ref.shape[0]):
        u = u_ref[s]
        o_ref[s] = _pool_finish(_band_sum(band, u), cnt, u, pw_ref, ps_ref)


def _pool_grid_body(u_ref, pw_ref, ps_ref, o_ref, cp_ref, *, w, rows):
    nblk = rows * GRID_W // POOL_BLOCK
    unroll = math.gcd(nblk, POOL_UNROLL)
    pad = POOL_PAD_ROWS * GRID_W
    ti = lax.broadcasted_iota(jnp.int32, (POOL_BLOCK, POOL_BLOCK), 0)
    si = lax.broadcasted_iota(jnp.int32, (POOL_BLOCK, POOL_BLOCK), 1)
    tc = ti & (GRID_W - 1)
    sc = si & (GRID_W - 1)
    same_row = (ti >> 6) == (si >> 6)
    band = jnp.where(same_row & (sc >= tc - w // 2) & (sc < tc - w // 2 + w), 1.0, 0.0).astype(BF16)
    cp_ref[0:pad + GRID_W, :] = jnp.zeros((pad + GRID_W, POOL_GROUP_DIM), F32)
    rows_per_block = POOL_BLOCK // GRID_W

    def col_pool(i, acc):
        off = pl.multiple_of(i * POOL_BLOCK, POOL_BLOCK)
        cp = _band_sum(band, u_ref[pl.ds(off, POOL_BLOCK), :])
        prefix = []
        for k in range(rows_per_block):
            acc = acc + cp[k * GRID_W:(k + 1) * GRID_W, :]
            prefix.append(acc)
        cp_ref[pl.ds(pad + GRID_W + off, POOL_BLOCK), :] = jnp.concatenate(prefix, axis=0)
        return acc

    total = lax.fori_loop(0, nblk, col_pool, jnp.zeros((GRID_W, POOL_GROUP_DIM), F32), unroll=unroll)
    for k in range(1, POOL_PAD_ROWS):
        cp_ref[pad + (rows + k) * GRID_W:pad + (rows + k + 1) * GRID_W, :] = total

    tt = lax.broadcasted_iota(jnp.int32, (POOL_BLOCK, POOL_GROUP_DIM), 0)
    ccnt = _window_count(tt & (GRID_W - 1), w, GRID_W)

    def row_pool(i, carry):
        off = pl.multiple_of(i * POOL_BLOCK, POOL_BLOCK)
        tot = (cp_ref[pl.ds(pad + off + (w // 2) * GRID_W, POOL_BLOCK), :]
               - cp_ref[pl.ds(pad + off - (w // 2) * GRID_W, POOL_BLOCK), :])
        r = i * rows_per_block + (tt >> 6)
        cnt = _window_count(r, w, rows) * ccnt
        o_ref[pl.ds(off, POOL_BLOCK), :] = _pool_finish(tot, cnt, u_ref[pl.ds(off, POOL_BLOCK), :], pw_ref, ps_ref)
        return carry

    lax.fori_loop(0, nblk, row_pool, 0, unroll=unroll)


def _pool_kernel(u_ref, pw_ref, ps_ref, o_ref, *scratch, rows, t):
    g = pl.program_id(1)
    for gi, w in enumerate(POOL_WINDOWS):
        @pl.when(g == gi)
        def _(w=w):
            if rows is None:
                _pool_seq_body(u_ref, pw_ref, ps_ref, o_ref, w=w, t=t)
            else:
                _pool_grid_body(u_ref, pw_ref, ps_ref, o_ref, scratch[0], w=w, rows=rows)


def _pool_mixer(u, pool_w, pool_scale, rows, layer):
    b, t, _ = u.shape
    if rows is None:
        per_step = math.gcd(b, POOL_SEQ_PER_STEP)
        scratch = []
        blk = pl.BlockSpec((per_step, t, POOL_GROUP_DIM), lambda i, g: (i, 0, g))
    else:
        per_step = 1
        scratch = [pltpu.VMEM(((rows + 2 * POOL_PAD_ROWS) * GRID_W, POOL_GROUP_DIM), F32)]
        blk = pl.BlockSpec((None, t, POOL_GROUP_DIM), lambda i, g: (i, 0, g))
    return pl.pallas_call(
        functools.partial(_pool_kernel, rows=rows, t=t),
        grid=(b // per_step, POOL_GROUPS),
        in_specs=[
            blk,
            pl.BlockSpec((None, None, POOL_GROUP_DIM, POOL_GROUP_DIM), lambda i, g: (layer, g, 0, 0)),
            pl.BlockSpec((1, POOL_GROUP_DIM), lambda i, g: (0, g)),
        ],
        out_specs=blk,
        out_shape=jax.ShapeDtypeStruct((b, t, POOL_DIM), F32),
        scratch_shapes=scratch,
        compiler_params=_params(("parallel", "parallel")),
        name="pool_seq" if rows is None else "pool_grid",
    )(u, pool_w, pool_scale)


def _ssd_chunk_terms(dtc, alog):
    n = SSD_CHUNK
    da = dtc * (-jnp.exp(alog) * LOG2E)
    row = lax.broadcasted_iota(jnp.int32, da.shape, 0)
    cs = da
    rcs = da
    k = 1
    while k < n:
        cs = cs + jnp.where(row >= k, pltpu.roll(cs, k, 0), 0.0)
        rcs = rcs + jnp.where(row < n - k, pltpu.roll(rcs, n - k, 0), 0.0)
        k *= 2
    logdt = jnp.log(dtc) * LOG2E
    return dict(colf=cs - logdt, colb=rcs - logdt, cst=cs.T, rcst=rcs.T, dtt=dtc.T)


def _rows64(mat, h):
    return jnp.broadcast_to(mat[h:h + 1, :], (SSD_HEAD_DIM, mat.shape[1]))


def _pair_rows(mat, h0):
    return jnp.concatenate([_rows64(mat, h0), _rows64(mat, h0 + 1)], axis=0)


def _state_update(h_scr, g, xw, bc_g, dec_t, head_off):
    rows = slice(g * GROUP_DIM, (g + 1) * GROUP_DIM)
    decay = jnp.concatenate(
        [_rows64(dec_t, head_off + g * HEADS_PER_GROUP + h) for h in range(HEADS_PER_GROUP)], axis=0)
    h_scr[rows, :] = h_scr[rows, :] * decay + _dot(xw, bc_g)


def _ssd_fwd_chunk(ci, xt_ref, bc_ref, cc_ref, dt_ref, alog_ref, y1_ref, h_scr):
    n = SSD_CHUNK
    tok = slice(ci * n, (ci + 1) * n)
    tm = _ssd_chunk_terms(dt_ref[tok, :], alog_ref[...])
    cst, rcst, dtt = tm["cst"], tm["rcst"], tm["dtt"]
    tot = cst[:, n - 1:n]
    eoff_t = jnp.exp2(cst)
    w_t = dtt * jnp.exp2(tot - cst)
    dec_t = jnp.exp2(jnp.broadcast_to(tot, (LANES, SSD_STATE)))
    si = lax.broadcasted_iota(jnp.int32, (n, n), 0)
    li = lax.broadcasted_iota(jnp.int32, (n, n), 1)
    src_before = si <= li
    src_after = si >= li
    zeros64 = jnp.zeros((SSD_HEAD_DIM, n), BF16)
    yield
    for g in range(SSD_GROUPS):
        bc_g = bc_ref[tok, g * SSD_STATE:(g + 1) * SSD_STATE]
        cc_g = cc_ref[tok, g * SSD_STATE:(g + 1) * SSD_STATE]
        sc_t = _dot_nt(bc_g, cc_g)
        y_off = _dot_nt(h_scr[g * GROUP_DIM:(g + 1) * GROUP_DIM, :].astype(BF16), cc_g)
        xw = []
        for j in range(HEADS_PER_GROUP // 2):
            h0 = g * HEADS_PER_GROUP + 2 * j
            mts = []
            for h in (h0, h0 + 1):
                seg_f = cst[h:h + 1, :] - tm["colf"][:, h:h + 1]
                hb = SSD_HEADS + h
                seg_b = rcst[hb:hb + 1, :] - tm["colb"][:, hb:hb + 1]
                dec = (jnp.exp2(jnp.where(src_before, seg_f, -jnp.inf))
                       + jnp.exp2(jnp.where(src_after, seg_b, -jnp.inf)))
                mts.append((sc_t * dec).astype(BF16))
            r0 = h0 * SSD_HEAD_DIM
            xp = xt_ref[ci, r0:r0 + 2 * SSD_HEAD_DIM, :]
            lhs = jnp.concatenate([jnp.concatenate([xp[:SSD_HEAD_DIM], zeros64], axis=0),
                                   jnp.concatenate([zeros64, xp[SSD_HEAD_DIM:]], axis=0)], axis=1)
            y_diag = _dot(lhs, jnp.concatenate(mts, axis=0))
            yo = y_off[2 * j * SSD_HEAD_DIM:(2 * j + 2) * SSD_HEAD_DIM, :]
            y1_ref[ci, r0:r0 + 2 * SSD_HEAD_DIM, :] = (y_diag + yo * _pair_rows(eoff_t, h0)).astype(y1_ref.dtype)
            xw.append((xp.astype(F32) * _pair_rows(w_t, h0)).astype(BF16))
            yield
        _state_update(h_scr, g, jnp.concatenate(xw, axis=0), bc_g, dec_t, 0)
        yield


def _ssd_bwd_chunk(ci, xt_ref, bc_ref, cc_ref, dt_ref, alog_ref, dsk, y1_ref, y_scr, h_scr):
    n = SSD_CHUNK
    tok = slice(ci * n, (ci + 1) * n)
    tm = _ssd_chunk_terms(dt_ref[tok, :], alog_ref[...])
    rcst, dtt = tm["rcst"], tm["dtt"]
    tot = rcst[:, 0:1]
    eoff_t = jnp.exp2(rcst)
    w_t = dtt * jnp.exp2(tot - rcst)
    dec_t = jnp.exp2(jnp.broadcast_to(tot, (LANES, SSD_STATE)))
    for g in range(SSD_GROUPS):
        bc_g = bc_ref[tok, g * SSD_STATE:(g + 1) * SSD_STATE]
        cc_g = cc_ref[tok, g * SSD_STATE:(g + 1) * SSD_STATE]
        y_off = _dot_nt(h_scr[g * GROUP_DIM:(g + 1) * GROUP_DIM, :].astype(BF16), cc_g)
        xw, yt = [], []
        for j in range(HEADS_PER_GROUP // 2):
            h0 = g * HEADS_PER_GROUP + 2 * j
            r0 = h0 * SSD_HEAD_DIM
            xp = xt_ref[ci, r0:r0 + 2 * SSD_HEAD_DIM, :].astype(F32)
            yo = y_off[2 * j * SSD_HEAD_DIM:(2 * j + 2) * SSD_HEAD_DIM, :]
            yt.append(y1_ref[ci, r0:r0 + 2 * SSD_HEAD_DIM, :].astype(F32)
                      + yo * _pair_rows(eoff_t, SSD_HEADS + h0) + _pair_rows(dsk, h0) * xp)
            xw.append((xp * _pair_rows(w_t, SSD_HEADS + h0)).astype(BF16))
        y_scr[tok, g * GROUP_DIM:(g + 1) * GROUP_DIM] = jnp.concatenate(yt, axis=0).T
        _state_update(h_scr, g, jnp.concatenate(xw, axis=0), bc_g, dec_t, SSD_HEADS)


def _project_and_conv(j, nblk, x_ref, xp_ref, xn_ref, mod_ref, g_ref, wm_ref, wt_ref,
                      dtb_ref, cw_ref, cb_ref, up_ref, z_ref, gt_ref, dt_ref, xt_ref, bc_ref, cc_ref,
                      hm_scr, ext_refs):
    m = mod_ref[...]
    g = g_ref[...]

    def modulated(x_rows, keep=True):
        hm = _rmsnorm(x_rows, g) * (1.0 + m[1:2]) + m[0:1]
        return jnp.where(keep, hm, 0.0).astype(BF16)

    tile = slice(CONV_HALO, CONV_HALO + TB_MIX)
    hm_scr[0:CONV_HALO, :] = modulated(xp_ref[...], j > 0)
    hm_scr[tile, :] = modulated(x_ref[...])
    hm_scr[CONV_HALO + TB_MIX:2 * CONV_HALO + TB_MIX, :] = modulated(xn_ref[...], j < nblk - 1)
    yield
    left = SSD_CONV // 2
    ct = CONV_TILE
    n_ct = CONV_DIM // ct
    n_ext = TB_MIX + 2 * CONV_HALO
    conv_base = SUBLANES + CONV_HALO

    def project_xbc(c):
        xbc = _dot(hm_scr[...], wm_ref[:, W_XBC_AT + c * ct:W_XBC_AT + (c + 1) * ct])
        for k in range(SSD_CONV):
            ext_refs[c][k, SUBLANES + left - k:SUBLANES + left - k + n_ext, :] = xbc

    def conv(c):
        cols = slice(c * ct, (c + 1) * ct)
        acc = cb_ref[:, cols] + ext_refs[c][0, conv_base:conv_base + TB_MIX, :] * cw_ref[0:1, cols]
        for k in range(1, SSD_CONV):
            acc = acc + ext_refs[c][k, conv_base:conv_base + TB_MIX, :] * cw_ref[k:k + 1, cols]
        y = jax.nn.silu(acc)
        if c < D_INNER // ct:
            for ci in range(CHUNKS_PER_TILE):
                xt_ref[ci, cols, :] = y[ci * SSD_CHUNK:(ci + 1) * SSD_CHUNK, :].T.astype(BF16)
        elif c == D_INNER // ct:
            bc_ref[...] = y.astype(BF16)
        else:
            cc_ref[...] = y.astype(BF16)

    project_xbc(0)
    yield
    for c in range(n_ct):
        if c + 1 < n_ct:
            project_xbc(c + 1)
        if c < D_INNER // ct:
            zc = slice(c * ct, (c + 1) * ct)
            z_ref[:, zc] = _dot(hm_scr[tile, :], wm_ref[:, W_Z_AT + c * ct:W_Z_AT + (c + 1) * ct]).astype(z_ref.dtype)
        conv(c)
        yield
    dt_ref[...] = jax.nn.softplus(_dot(hm_scr[tile, :], wt_ref[:, 0:LANES]) + dtb_ref[...])
    yield
    for c in range(2 * D_MODEL // PROJ_COLS):
        gc = slice(c * PROJ_COLS, (c + 1) * PROJ_COLS)
        gt_ref[:, gc] = _dot(hm_scr[tile, :],
                             wt_ref[:, LANES + c * PROJ_COLS:LANES + (c + 1) * PROJ_COLS]).astype(gt_ref.dtype)
        yield
    for c in range(POOL_DIM // PROJ_COLS):
        pc = slice(c * PROJ_COLS, (c + 1) * PROJ_COLS)
        up_ref[:, pc] = _dot(hm_scr[tile, :], wm_ref[:, pc])
        yield


def _mixf_kernel(*refs, zero_init, emit_state, nblk, ntiles):
    x_ref, xp_ref, xn_ref, mod_ref, g_ref, wm_ref, wt_ref, dtb_ref, cw_ref, cb_ref, alog_ref = refs[:11]
    refs = refs[11:]
    if not zero_init:
        h0_ref, refs = refs[0], refs[1:]
    up_ref, z_ref, gt_ref, dt_ref, xt_ref, bc_ref, cc_ref, y1_ref = refs[:8]
    refs = refs[8:]
    if emit_state:
        hf_ref, refs = refs[0], refs[1:]
    hm_scr, xt_cur, bc_cur, cc_cur, dt_cur, h_scr = refs[:6]
    ext_refs = refs[6:]
    s = pl.program_id(0)
    ja = lax.rem(jnp.minimum(s, ntiles - 1), nblk)
    jb = lax.rem(jnp.maximum(s - 1, 0), nblk)

    @pl.when(s == 0)
    def _():
        xt_cur[...] = jnp.zeros_like(xt_cur)
        bc_cur[...] = jnp.zeros_like(bc_cur)
        cc_cur[...] = jnp.zeros_like(cc_cur)
        dt_cur[...] = jnp.ones_like(dt_cur)

    @pl.when(jb == 0)
    def _():
        h_scr[...] = jnp.zeros_like(h_scr) if zero_init else h0_ref[...]

    proj = _project_and_conv(ja, nblk, x_ref, xp_ref, xn_ref, mod_ref, g_ref, wm_ref, wt_ref,
                             dtb_ref, cw_ref, cb_ref, up_ref, z_ref, gt_ref, dt_ref, xt_ref, bc_ref,
                             cc_ref, hm_scr, ext_refs)
    scan = (None for ci in range(CHUNKS_PER_TILE)
            for _ in _ssd_fwd_chunk(ci, xt_cur, bc_cur, cc_cur, dt_cur, alog_ref, y1_ref, h_scr))
    for n_scan in SCAN_SCHEDULE:
        next(proj, None)
        for _ in range(n_scan):
            next(scan, None)
    for _ in proj:
        pass
    for _ in scan:
        pass

    if emit_state:
        @pl.when(jb == nblk - 1)
        def _():
            hf_ref[...] = h_scr[...]

    xt_cur[...] = xt_ref[...]
    bc_cur[...] = bc_ref[...]
    cc_cur[...] = cc_ref[...]
    dt_cur[...] = dt_ref[...]


def _mix_fwd(x, mod, g, lw, h0_f, layer, emit_state):
    b, t, _ = x.shape
    nblk = t // TB_MIX
    ntiles = b * nblk
    nc = t // SSD_CHUNK
    per = TB_MIX // CONV_HALO
    zero_init = h0_f is None
    mod_idx = (lambda i: i) if mod.shape[0] == b else (lambda i: 0)
    ta = lambda s: jnp.minimum(s, ntiles - 1)
    tb = lambda s: jnp.maximum(s - 1, 0)
    tok = lambda n: pl.BlockSpec((None, TB_MIX, n), lambda s: (ta(s) // nblk, ta(s) % nblk, 0))
    chunked = lambda tile: pl.BlockSpec((None, CHUNKS_PER_TILE, D_INNER, SSD_CHUNK),
                                        lambda s: (tile(s) // nblk, tile(s) % nblk, 0, 0))
    in_specs = [
        tok(D_MODEL),
        pl.BlockSpec((None, CONV_HALO, D_MODEL),
                     lambda s: (ta(s) // nblk, jnp.maximum((ta(s) % nblk) * per - 1, 0), 0)),
        pl.BlockSpec((None, CONV_HALO, D_MODEL),
                     lambda s: (ta(s) // nblk, jnp.minimum((ta(s) % nblk + 1) * per, t // CONV_HALO - 1), 0)),
        pl.BlockSpec((None, 3, D_MODEL), lambda s: (mod_idx(ta(s) // nblk), 0, 0)),
        _resident((1, D_MODEL)),
        _resident((D_MODEL, W_MAIN_COLS), layer),
        _resident((D_MODEL, LANES + 2 * D_MODEL), layer),
        _resident((1, LANES)),
        _resident((SSD_CONV, CONV_DIM)),
        _resident((1, CONV_DIM)),
        _resident((1, LANES)),
    ]
    args = [x, x, x, mod, g, lw["w_in"], lw["w_tail"], lw["dt_bias"], lw["conv_w"], lw["conv_b"], lw["a_log"]]
    if not zero_init:
        in_specs.append(pl.BlockSpec((None, None, D_INNER, SSD_STATE), lambda s: (tb(s) // nblk, layer, 0, 0)))
        args.append(h0_f)
    out_specs = [tok(POOL_DIM), tok(D_INNER), tok(2 * D_MODEL), tok(LANES), chunked(ta), tok(BC_DIM), tok(BC_DIM),
                 chunked(tb)]
    out_shape = [
        jax.ShapeDtypeStruct((b, t, POOL_DIM), F32),
        jax.ShapeDtypeStruct((b, t, D_INNER), BF16),
        jax.ShapeDtypeStruct((b, t, 2 * D_MODEL), BF16),
        jax.ShapeDtypeStruct((b, t, LANES), F32),
        jax.ShapeDtypeStruct((b, nc, D_INNER, SSD_CHUNK), BF16),
        jax.ShapeDtypeStruct((b, t, BC_DIM), BF16),
        jax.ShapeDtypeStruct((b, t, BC_DIM), BF16),
        jax.ShapeDtypeStruct((b, nc, D_INNER, SSD_CHUNK), BF16),
    ]
    if emit_state:
        out_specs.append(pl.BlockSpec((None, D_INNER, SSD_STATE), lambda s: (tb(s) // nblk, 0, 0)))
        out_shape.append(jax.ShapeDtypeStruct((b, D_INNER, SSD_STATE), F32))
    scratch = [
        pltpu.VMEM((TB_MIX + 2 * CONV_HALO, D_MODEL), BF16),
        pltpu.VMEM((CHUNKS_PER_TILE, D_INNER, SSD_CHUNK), BF16),
        pltpu.VMEM((TB_MIX, BC_DIM), BF16),
        pltpu.VMEM((TB_MIX, BC_DIM), BF16),
        pltpu.VMEM((TB_MIX, LANES), F32),
        pltpu.VMEM((D_INNER, SSD_STATE), F32),
    ] + [pltpu.VMEM((SSD_CONV, TB_MIX + 2 * CONV_HALO + 2 * SUBLANES, CONV_TILE), F32)
         for _ in range(CONV_DIM // CONV_TILE)]
    return pl.pallas_call(
        functools.partial(_mixf_kernel, zero_init=zero_init, emit_state=emit_state, nblk=nblk, ntiles=ntiles),
        grid=(ntiles + 1,), in_specs=in_specs, out_specs=out_specs, out_shape=out_shape,
        scratch_shapes=scratch,
        compiler_params=_params(("arbitrary",)),
        name="mix_fwd",
    )(*args)


def _mixb_kernel(*refs, zero_init, emit_state, nblk):
    (x_ref, xt_ref, bc_ref, cc_ref, dt_ref, y1_ref, z_ref, gt_ref, yp_ref, mod_ref, alog_ref, dsk_ref,
     ng_ref, wbp_ref, wbs_ref, wo_ref) = refs[:16]
    refs = refs[16:]
    if not zero_init:
        h0_ref, refs = refs[0], refs[1:]
    o_ref, refs = refs[0], refs[1:]
    if emit_state:
        hb_ref, refs = refs[0], refs[1:]
    y_scr, h_scr = refs
    j = pl.program_id(1)

    @pl.when(j == 0)
    def _():
        h_scr[...] = jnp.zeros_like(h_scr) if zero_init else h0_ref[...]

    dsk = dsk_ref[...]
    for ci in reversed(range(CHUNKS_PER_TILE)):
        _ssd_bwd_chunk(ci, xt_ref, bc_ref, cc_ref, dt_ref, alog_ref, dsk, y1_ref, y_scr, h_scr)

    if emit_state:
        @pl.when(j == nblk - 1)
        def _():
            hb_ref[...] = h_scr[...]

    y = _rmsnorm(y_scr[...] * jax.nn.silu(z_ref[...].astype(F32)), ng_ref[...])
    gates = jax.nn.sigmoid(gt_ref[...].astype(F32))
    merged = (gates[:, :D_MODEL] * _dot(yp_ref[...].astype(BF16), wbp_ref[...])
              + gates[:, D_MODEL:] * _dot(y.astype(BF16), wbs_ref[...]))
    o_ref[...] = x_ref[...] + mod_ref[2:3, :] * _dot(merged.astype(BF16), wo_ref[...])


def _mix_bwd(x, fwd, y_pool, mod, lw, h0_b, layer, emit_state):
    b, t, _ = x.shape
    nblk = t // TB_MIX
    zero_init = h0_b is None
    z, gates, dt, xt, bcv, ccv, y1 = fwd
    mod_idx = (lambda i: i) if mod.shape[0] == b else (lambda i: 0)
    tok = lambda n: pl.BlockSpec((None, TB_MIX, n), lambda i, j: (i, nblk - 1 - j, 0))
    chunked = pl.BlockSpec((None, CHUNKS_PER_TILE, D_INNER, SSD_CHUNK), lambda i, j: (i, nblk - 1 - j, 0, 0))
    in_specs = [
        tok(D_MODEL), chunked, tok(BC_DIM), tok(BC_DIM), tok(LANES), chunked, tok(D_INNER), tok(2 * D_MODEL),
        tok(POOL_DIM),
        pl.BlockSpec((None, 3, D_MODEL), lambda i, j: (mod_idx(i), 0, 0)),
        _resident((1, LANES)),
        _resident((LANES, LANES)),
        _resident((1, D_INNER)),
        _resident((POOL_DIM, D_MODEL), layer),
        _resident((D_INNER, D_MODEL), layer),
        _resident((D_MODEL, D_MODEL), layer),
    ]
    args = [x, xt, bcv, ccv, dt, y1, z, gates, y_pool, mod, lw["a_log"], lw["d_skip"], lw["ssd_norm"],
            lw["w_branch_pool"], lw["w_branch_ssd"], lw["w_out"]]
    if not zero_init:
        in_specs.append(pl.BlockSpec((None, None, D_INNER, SSD_STATE), lambda i, j: (i, layer, 0, 0)))
        args.append(h0_b)
    out_specs = [tok(D_MODEL)]
    out_shape = [jax.ShapeDtypeStruct((b, t, D_MODEL), F32)]
    if emit_state:
        out_specs.append(pl.BlockSpec((None, D_INNER, SSD_STATE), lambda i, j: (i, 0, 0)))
        out_shape.append(jax.ShapeDtypeStruct((b, D_INNER, SSD_STATE), F32))
    return pl.pallas_call(
        functools.partial(_mixb_kernel, zero_init=zero_init, emit_state=emit_state, nblk=nblk),
        grid=(b, nblk), in_specs=in_specs, out_specs=out_specs, out_shape=out_shape,
        scratch_shapes=[pltpu.VMEM((TB_MIX, D_INNER), F32), pltpu.VMEM((D_INNER, SSD_STATE), F32)],
        compiler_params=_params(("parallel", "arbitrary")),
        name="mix_bwd",
    )(*args)


def _layer(x, mod, rows, h0_f, h0_b, layer, lw, emit_state, final_g):
    ng = lw["norm_g"]
    x = _ffn(x, mod[:, 0:3], ng[0:1], lw["ffn1_w13"], lw["ffn1_w2"], layer)
    mod2 = mod[:, 3:6]
    fwd = _mix_fwd(x, mod2, ng[1:2], lw, h0_f, layer, emit_state)
    hf = fwd[8] if emit_state else None
    y_pool = _pool_mixer(fwd[0], lw["pool_w"], lw["pool_scale"], rows, layer)
    res = _mix_bwd(x, fwd[1:8], y_pool, mod2, lw, h0_b, layer, emit_state)
    x, hb = (res[0], res[1]) if emit_state else (res[0], None)
    x = _ffn(x, mod[:, 6:9], ng[2:3], lw["ffn2_w13"], lw["ffn2_w2"], layer, final_g)
    return x, hf, hb


def _prep_weights(p):
    s3 = W_MAIN_COLS
    s4 = s3 + 2 * SSD_HEADS
    w_in = p["w_in"]
    w_tail = jnp.concatenate(
        [jnp.pad(w_in[:, :, s3:s4], ((0, 0), (0, 0), (0, LANES - 2 * SSD_HEADS))), w_in[:, :, s4:]], axis=2)
    stacked = {k: p[k].astype(BF16) for k in ("ffn1_w13", "ffn1_w2", "ffn2_w13", "ffn2_w2", "w_in", "pool_w",
                                              "w_branch_pool", "w_branch_ssd", "w_out")}
    stacked["w_tail"] = w_tail.astype(BF16)
    return stacked


def _prep_layer(l, p, stacked):
    pad_lanes = lambda v: jnp.pad(v.reshape(1, -1), ((0, 0), (0, LANES - v.size)))
    return dict(
        stacked,
        norm_g=p["norm_g"][l],
        dt_bias=pad_lanes(p["dt_bias"][l]),
        a_log=pad_lanes(p["a_log"][l]),
        d_skip=jnp.broadcast_to(pad_lanes(p["d_skip"][l]).reshape(LANES, 1), (LANES, LANES)),
        pool_scale=p["pool_scale"][l].reshape(1, POOL_DIM),
        conv_w=p["conv_w"][l], conv_b=p["conv_b"][l].reshape(1, CONV_DIM),
        ssd_norm=p["ssd_norm"][l].reshape(1, D_INNER),
    )


def kernel(x_prompt, x_sample, state_ssd_fwd, state_ssd_bwd, c, c_ctx, ada_w, ada_b, norm_g, ffn1_w13, ffn1_w2, ffn2_w13, ffn2_w2, w_in, pool_w, pool_scale, conv_w, conv_b, a_log, dt_bias, d_skip, ssd_norm, w_branch_pool, w_branch_ssd, w_out, final_norm):
    p = dict(norm_g=norm_g, ffn1_w13=ffn1_w13, ffn1_w2=ffn1_w2, ffn2_w13=ffn2_w13, ffn2_w2=ffn2_w2,
             w_in=w_in, pool_w=pool_w, pool_scale=pool_scale, conv_w=conv_w, conv_b=conv_b, a_log=a_log,
             dt_bias=dt_bias, d_skip=d_skip, ssd_norm=ssd_norm, w_branch_pool=w_branch_pool,
             w_branch_ssd=w_branch_ssd, w_out=w_out)
    nb = c.shape[0]
    rows = x_sample.shape[1] // GRID_W
    cond = jnp.concatenate([c_ctx[None, :], c], axis=0)
    cond = jnp.pad(cond, ((0, -(nb + 1) % SUBLANES), (0, 0)))
    mod = _adaln(cond, ada_w, ada_b).reshape(DEPTH, cond.shape[0], N_MOD, D_MODEL)
    fg = final_norm.reshape(1, D_MODEL)
    sd = state_ssd_fwd.shape
    h0_f = state_ssd_fwd.reshape(sd[0], DEPTH, D_INNER, SSD_STATE)
    h0_b = state_ssd_bwd.reshape(sd[0], DEPTH, D_INNER, SSD_STATE)
    xc, xl = x_prompt, x_sample
    new_f, new_b = [], []
    stacked = _prep_weights(p)
    for l in range(DEPTH):
        lw = _prep_layer(l, p, stacked)
        last = fg if l == DEPTH - 1 else None
        xc, hf, hb = _layer(xc, mod[l, 0:1], None, None, None, l, lw, True, last)
        new_f.append(hf.reshape(-1, SSD_HEADS, SSD_HEAD_DIM, SSD_STATE))
        new_b.append(hb.reshape(-1, SSD_HEADS, SSD_HEAD_DIM, SSD_STATE))
        xl, _, _ = _layer(xl, mod[l, 1:1 + nb], rows, h0_f, h0_b, l, lw, False, last)
    return (xc, xl, jnp.stack(new_f, axis=1), jnp.stack(new_b, axis=1))
```

```python
import functools
import math

import jax
import jax.numpy as jnp
from jax import lax
from jax.experimental import pallas as pl
from jax.experimental.pallas import tpu as pltpu

F32 = jnp.float32
BF16 = jnp.bfloat16

D_MODEL = 1024
DEPTH = 2
GRID_W = 64
POOL_WINDOWS = (2, 4, 8, 16)
POOL_GROUPS = len(POOL_WINDOWS)
POOL_DIM = D_MODEL
POOL_GROUP_DIM = POOL_DIM // POOL_GROUPS
D_INNER = 2 * D_MODEL
SSD_HEAD_DIM = 64
SSD_HEADS = D_INNER // SSD_HEAD_DIM
SSD_GROUPS = 4
HEADS_PER_GROUP = SSD_HEADS // SSD_GROUPS
GROUP_DIM = HEADS_PER_GROUP * SSD_HEAD_DIM
SSD_STATE = 128
SSD_CONV = 4
SSD_CHUNK = 128
BC_DIM = SSD_GROUPS * SSD_STATE
CONV_DIM = D_INNER + 2 * BC_DIM
W_Z_AT = POOL_DIM
W_XBC_AT = W_Z_AT + D_INNER
W_MAIN_COLS = W_XBC_AT + CONV_DIM
D_FF = 2816
N_MOD = 9
EPS = 1e-6
LOG2E = math.log2(math.e)

LANES = 128
SUBLANES = 8
VMEM_LIMIT = 56 * 1024 * 1024

TM_FFN = 512
FF_CHUNK = 256
TB_MIX = 256
CHUNKS_PER_TILE = TB_MIX // SSD_CHUNK
CONV_TILE = BC_DIM
CONV_HALO = 16
POOL_BLOCK = 256
POOL_PAD_ROWS = 8
POOL_SEQ_PER_STEP = 4
POOL_UNROLL = 16
ADA_TN = 2304
PROJ_COLS = 256
SCAN_SCHEDULE = (0,) + (1,) * 7 + (3,) * 9 + (2,) * 4


def _dot(a, b):
    return jnp.dot(a, b, preferred_element_type=F32)


def _dot_nt(a, b):
    return lax.dot_general(a, b, (((1,), (1,)), ((), ())), preferred_element_type=F32)


def _rmsnorm(x, g):
    return x * lax.rsqrt(jnp.mean(x * x, axis=-1, keepdims=True) + EPS) * g


def _params(sem):
    return pltpu.CompilerParams(dimension_semantics=sem, vmem_limit_bytes=VMEM_LIMIT)


def _resident(shape, layer=None):
    nd = len(shape)
    if layer is None:
        return pl.BlockSpec(shape, lambda *_: (0,) * nd, pipeline_mode=pl.Buffered(1))
    return pl.BlockSpec((None,) + tuple(shape), lambda *_: (layer,) + (0,) * nd, pipeline_mode=pl.Buffered(1))


def _ada_kernel(cond_ref, w_ref, b_ref, o_ref):
    s = jax.nn.silu(cond_ref[...]).astype(BF16)
    o_ref[...] = _dot(s, w_ref[...].astype(BF16)) + b_ref[...]


def _adaln(cond, ada_w, ada_b):
    n = cond.shape[0]
    ncol = N_MOD * D_MODEL
    return pl.pallas_call(
        _ada_kernel,
        grid=(DEPTH, ncol // ADA_TN),
        in_specs=[
            pl.BlockSpec((n, D_MODEL), lambda l, j: (0, 0)),
            pl.BlockSpec((None, D_MODEL, ADA_TN), lambda l, j: (l, 0, j)),
            pl.BlockSpec((None, 1, ADA_TN), lambda l, j: (l, 0, j)),
        ],
        out_specs=pl.BlockSpec((None, n, ADA_TN), lambda l, j: (l, 0, j)),
        out_shape=jax.ShapeDtypeStruct((DEPTH, n, ncol), F32),
        compiler_params=_params(("parallel", "parallel")),
        name="adaln",
    )(cond, ada_w, ada_b.reshape(DEPTH, 1, ncol))


def _ffn_kernel(x_ref, mod_ref, g_ref, w13_ref, w2_ref, *rest, final):
    if final:
        fg_ref, o_ref, a_scr = rest
    else:
        o_ref, a_scr = rest
    x = x_ref[...]
    m = mod_ref[...]
    h = (_rmsnorm(x, g_ref[...]) * (1.0 + m[1:2]) + m[0:1]).astype(BF16)
    for j in range(D_FF // FF_CHUNK):
        lo, hi = j * FF_CHUNK, (j + 1) * FF_CHUNK
        gate = _dot(h, w13_ref[:, lo:hi])
        up = _dot(h, w13_ref[:, D_FF + lo:D_FF + hi])
        a_scr[:, lo:hi] = (jax.nn.silu(gate) * up).astype(BF16)
    y = _dot(a_scr[...], w2_ref[...])
    out = x + 0.5 * m[2:3] * y
    if final:
        out = _rmsnorm(out, fg_ref[...])
    o_ref[...] = out


def _ffn(x, mod, g, w13, w2, layer, final_g=None):
    b, t, _ = x.shape
    nt = b * t
    tpm = nt // mod.shape[0]
    final = final_g is not None
    in_specs = [
        pl.BlockSpec((TM_FFN, D_MODEL), lambda i: (i, 0)),
        pl.BlockSpec((None, 3, D_MODEL), lambda i: (i * TM_FFN // tpm, 0, 0)),
        _resident((1, D_MODEL)),
        _resident((D_MODEL, 2 * D_FF), layer),
        _resident((D_FF, D_MODEL), layer),
    ]
    args = [x.reshape(nt, D_MODEL), mod, g, w13, w2]
    if final:
        in_specs.append(_resident((1, D_MODEL)))
        args.append(final_g)
    out = pl.pallas_call(
        functools.partial(_ffn_kernel, final=final),
        grid=(nt // TM_FFN,),
        in_specs=in_specs,
        out_specs=pl.BlockSpec((TM_FFN, D_MODEL), lambda i: (i, 0)),
        out_shape=jax.ShapeDtypeStruct((nt, D_MODEL), F32),
        scratch_shapes=[pltpu.VMEM((TM_FFN, D_FF), BF16)],
        compiler_params=_params(("parallel",)),
        name="ffn_final" if final else "ffn",
    )(*args)
    return out.reshape(b, t, D_MODEL)


def _window_count(idx, w, n):
    lo = jnp.maximum(idx - w // 2, 0)
    hi = jnp.minimum(idx - w // 2 + w, n)
    return (hi - lo).astype(F32)


def _band_sum(band, u):
    hi = u.astype(BF16)
    lo = (u - hi.astype(F32)).astype(BF16)
    return _dot(band, hi) + _dot(band, lo)


def _pool_finish(tot, cnt, u, pw_ref, ps_ref):
    d = (tot / cnt - u).astype(BF16)
    return _dot(d, pw_ref[...]) * ps_ref[...]


def _pool_seq_body(u_ref, pw_ref, ps_ref, o_ref, *, w, t):
    ti = lax.broadcasted_iota(jnp.int32, (t, t), 0)
    si = lax.broadcasted_iota(jnp.int32, (t, t), 1)
    band = jnp.where((si >= ti - w // 2) & (si < ti - w // 2 + w), 1.0, 0.0).astype(BF16)
    cnt = _window_count(lax.broadcasted_iota(jnp.int32, (t, POOL_GROUP_DIM), 0), w, t)
    for s in range(u_ref.shape[0]):
        u = u_ref[s]
        o_ref[s] = _pool_finish(_band_sum(band, u), cnt, u, pw_ref, ps_ref)


def _pool_grid_body(u_ref, pw_ref, ps_ref, o_ref, cp_ref, *, w, rows):
    nblk = rows * GRID_W // POOL_BLOCK
    unroll = math.gcd(nblk, POOL_UNROLL)
    pad = POOL_PAD_ROWS * GRID_W
    ti = lax.broadcasted_iota(jnp.int32, (POOL_BLOCK, POOL_BLOCK), 0)
    si = lax.broadcasted_iota(jnp.int32, (POOL_BLOCK, POOL_BLOCK), 1)
    tc = ti & (GRID_W - 1)
    sc = si & (GRID_W - 1)
    same_row = (ti >> 6) == (si >> 6)
    band = jnp.where(same_row & (sc >= tc - w // 2) & (sc < tc - w // 2 + w), 1.0, 0.0).astype(BF16)
    cp_ref[0:pad + GRID_W, :] = jnp.zeros((pad + GRID_W, POOL_GROUP_DIM), F32)
    rows_per_block = POOL_BLOCK // GRID_W

    def col_pool(i, acc):
        off = pl.multiple_of(i * POOL_BLOCK, POOL_BLOCK)
        cp = _band_sum(band, u_ref[pl.ds(off, POOL_BLOCK), :])
        prefix = []
        for k in range(rows_per_block):
            acc = acc + cp[k * GRID_W:(k + 1) * GRID_W, :]
            prefix.append(acc)
        cp_ref[pl.ds(pad + GRID_W + off, POOL_BLOCK), :] = jnp.concatenate(prefix, axis=0)
        return acc

    total = lax.fori_loop(0, nblk, col_pool, jnp.zeros((GRID_W, POOL_GROUP_DIM), F32), unroll=unroll)
    for k in range(1, POOL_PAD_ROWS):
        cp_ref[pad + (rows + k) * GRID_W:pad + (rows + k + 1) * GRID_W, :] = total

    tt = lax.broadcasted_iota(jnp.int32, (POOL_BLOCK, POOL_GROUP_DIM), 0)
    ccnt = _window_count(tt & (GRID_W - 1), w, GRID_W)

    def row_pool(i, carry):
        off = pl.multiple_of(i * POOL_BLOCK, POOL_BLOCK)
        tot = (cp_ref[pl.ds(pad + off + (w // 2) * GRID_W, POOL_BLOCK), :]
               - cp_ref[pl.ds(pad + off - (w // 2) * GRID_W, POOL_BLOCK), :])
        r = i * rows_per_block + (tt >> 6)
        cnt = _window_count(r, w, rows) * ccnt
        o_ref[pl.ds(off, POOL_BLOCK), :] = _pool_finish(tot, cnt, u_ref[pl.ds(off, POOL_BLOCK), :], pw_ref, ps_ref)
        return carry

    lax.fori_loop(0, nblk, row_pool, 0, unroll=unroll)


def _pool_kernel(u_ref, pw_ref, ps_ref, o_ref, *scratch, rows, t):
    g = pl.program_id(1)
    for gi, w in enumerate(POOL_WINDOWS):
        @pl.when(g == gi)
        def _(w=w):
            if rows is None:
                _pool_seq_body(u_ref, pw_ref, ps_ref, o_ref, w=w, t=t)
            else:
                _pool_grid_body(u_ref, pw_ref, ps_ref, o_ref, scratch[0], w=w, rows=rows)


def _pool_mixer(u, pool_w, pool_scale, rows, layer):
    b, t, _ = u.shape
    if rows is None:
        per_step = math.gcd(b, POOL_SEQ_PER_STEP)
        scratch = []
        blk = pl.BlockSpec((per_step, t, POOL_GROUP_DIM), lambda i, g: (i, 0, g))
    else:
        per_step = 1
        scratch = [pltpu.VMEM(((rows + 2 * POOL_PAD_ROWS) * GRID_W, POOL_GROUP_DIM), F32)]
        blk = pl.BlockSpec((None, t, POOL_GROUP_DIM), lambda i, g: (i, 0, g))
    return pl.pallas_call(
        functools.partial(_pool_kernel, rows=rows, t=t),
        grid=(b // per_step, POOL_GROUPS),
        in_specs=[
            blk,
            pl.BlockSpec((None, None, POOL_GROUP_DIM, POOL_GROUP_DIM), lambda i, g: (layer, g, 0, 0)),
            pl.BlockSpec((1, POOL_GROUP_DIM), lambda i, g: (0, g)),
        ],
        out_specs=blk,
        out_shape=jax.ShapeDtypeStruct((b, t, POOL_DIM), F32),
        scratch_shapes=scratch,
        compiler_params=_params(("parallel", "parallel")),
        name="pool_seq" if rows is None else "pool_grid",
    )(u, pool_w, pool_scale)


def _ssd_chunk_terms(dtc, alog):
    n = SSD_CHUNK
    da = dtc * (-jnp.exp(alog) * LOG2E)
    row = lax.broadcasted_iota(jnp.int32, da.shape, 0)
    cs = da
    rcs = da
    k = 1
    while k < n:
        cs = cs + jnp.where(row >= k, pltpu.roll(cs, k, 0), 0.0)
        rcs = rcs + jnp.where(row < n - k, pltpu.roll(rcs, n - k, 0), 0.0)
        k *= 2
    logdt = jnp.log(dtc) * LOG2E
    return dict(colf=cs - logdt, colb=rcs - logdt, cst=cs.T, rcst=rcs.T, dtt=dtc.T)


def _rows64(mat, h):
    return jnp.broadcast_to(mat[h:h + 1, :], (SSD_HEAD_DIM, mat.shape[1]))


def _pair_rows(mat, h0):
    return jnp.concatenate([_rows64(mat, h0), _rows64(mat, h0 + 1)], axis=0)


def _state_update(h_scr, g, xw, bc_g, dec_t, head_off):
    rows = slice(g * GROUP_DIM, (g + 1) * GROUP_DIM)
    decay = jnp.concatenate(
        [_rows64(dec_t, head_off + g * HEADS_PER_GROUP + h) for h in range(HEADS_PER_GROUP)], axis=0)
    h_scr[rows, :] = h_scr[rows, :] * decay + _dot(xw, bc_g)


def _ssd_fwd_chunk(ci, xt_ref, bc_ref, cc_ref, dt_ref, alog_ref, y1_ref, h_scr):
    n = SSD_CHUNK
    tok = slice(ci * n, (ci + 1) * n)
    tm = _ssd_chunk_terms(dt_ref[tok, :], alog_ref[...])
    cst, rcst, dtt = tm["cst"], tm["rcst"], tm["dtt"]
    tot = cst[:, n - 1:n]
    eoff_t = jnp.exp2(cst)
    w_t = dtt * jnp.exp2(tot - cst)
    dec_t = jnp.exp2(jnp.broadcast_to(tot, (LANES, SSD_STATE)))
    si = lax.broadcasted_iota(jnp.int32, (n, n), 0)
    li = lax.broadcasted_iota(jnp.int32, (n, n), 1)
    src_before = si <= li
    src_after = si >= li
    zeros64 = jnp.zeros((SSD_HEAD_DIM, n), BF16)
    yield
    for g in range(SSD_GROUPS):
        bc_g = bc_ref[tok, g * SSD_STATE:(g + 1) * SSD_STATE]
        cc_g = cc_ref[tok, g * SSD_STATE:(g + 1) * SSD_STATE]
        sc_t = _dot_nt(bc_g, cc_g)
        y_off = _dot_nt(h_scr[g * GROUP_DIM:(g + 1) * GROUP_DIM, :].astype(BF16), cc_g)
        xw = []
        for j in range(HEADS_PER_GROUP // 2):
            h0 = g * HEADS_PER_GROUP + 2 * j
            mts = []
            for h in (h0, h0 + 1):
                seg_f = cst[h:h + 1, :] - tm["colf"][:, h:h + 1]
                hb = SSD_HEADS + h
                seg_b = rcst[hb:hb + 1, :] - tm["colb"][:, hb:hb + 1]
                dec = (jnp.exp2(jnp.where(src_before, seg_f, -jnp.inf))
                       + jnp.exp2(jnp.where(src_after, seg_b, -jnp.inf)))
                mts.append((sc_t * dec).astype(BF16))
            r0 = h0 * SSD_HEAD_DIM
            xp = xt_ref[ci, r0:r0 + 2 * SSD_HEAD_DIM, :]
            lhs = jnp.concatenate([jnp.concatenate([xp[:SSD_HEAD_DIM], zeros64], axis=0),
                                   jnp.concatenate([zeros64, xp[SSD_HEAD_DIM:]], axis=0)], axis=1)
            y_diag = _dot(lhs, jnp.concatenate(mts, axis=0))
            yo = y_off[2 * j * SSD_HEAD_DIM:(2 * j + 2) * SSD_HEAD_DIM, :]
            y1_ref[ci, r0:r0 + 2 * SSD_HEAD_DIM, :] = (y_diag + yo * _pair_rows(eoff_t, h0)).astype(y1_ref.dtype)
            xw.append((xp.astype(F32) * _pair_rows(w_t, h0)).astype(BF16))
            yield
        _state_update(h_scr, g, jnp.concatenate(xw, axis=0), bc_g, dec_t, 0)
        yield


def _ssd_bwd_chunk(ci, xt_ref, bc_ref, cc_ref, dt_ref, alog_ref, dsk, y1_ref, y_scr, h_scr):
    n = SSD_CHUNK
    tok = slice(ci * n, (ci + 1) * n)
    tm = _ssd_chunk_terms(dt_ref[tok, :], alog_ref[...])
    rcst, dtt = tm["rcst"], tm["dtt"]
    tot = rcst[:, 0:1]
    eoff_t = jnp.exp2(rcst)
    w_t = dtt * jnp.exp2(tot - rcst)
    dec_t = jnp.exp2(jnp.broadcast_to(tot, (LANES, SSD_STATE)))
    for g in range(SSD_GROUPS):
        bc_g = bc_ref[tok, g * SSD_STATE:(g + 1) * SSD_STATE]
        cc_g = cc_ref[tok, g * SSD_STATE:(g + 1) * SSD_STATE]
        y_off = _dot_nt(h_scr[g * GROUP_DIM:(g + 1) * GROUP_DIM, :].astype(BF16), cc_g)
        xw, yt = [], []
        for j in range(HEADS_PER_GROUP // 2):
            h0 = g * HEADS_PER_GROUP + 2 * j
            r0 = h0 * SSD_HEAD_DIM
            xp = xt_ref[ci, r0:r0 + 2 * SSD_HEAD_DIM, :].astype(F32)
            yo = y_off[2 * j * SSD_HEAD_DIM:(2 * j + 2) * SSD_HEAD_DIM, :]
            yt.append(y1_ref[ci, r0:r0 + 2 * SSD_HEAD_DIM, :].astype(F32)
                      + yo * _pair_rows(eoff_t, SSD_HEADS + h0) + _pair_rows(dsk, h0) * xp)
            xw.append((xp * _pair_rows(w_t, SSD_HEADS + h0)).astype(BF16))
        y_scr[tok, g * GROUP_DIM:(g + 1) * GROUP_DIM] = jnp.concatenate(yt, axis=0).T
        _state_update(h_scr, g, jnp.concatenate(xw, axis=0), bc_g, dec_t, SSD_HEADS)


def _project_and_conv(j, nblk, x_ref, xp_ref, xn_ref, mod_ref, g_ref, wm_ref, wt_ref,
                      dtb_ref, cw_ref, cb_ref, up_ref, z_ref, gt_ref, dt_ref, xt_ref, bc_ref, cc_ref,
                      hm_scr, ext_refs):
    m = mod_ref[...]
    g = g_ref[...]

    def modulated(x_rows, keep=True):
        hm = _rmsnorm(x_rows, g) * (1.0 + m[1:2]) + m[0:1]
        return jnp.where(keep, hm, 0.0).astype(BF16)

    tile = slice(CONV_HALO, CONV_HALO + TB_MIX)
    hm_scr[0:CONV_HALO, :] = modulated(xp_ref[...], j > 0)
    hm_scr[tile, :] = modulated(x_ref[...])
    hm_scr[CONV_HALO + TB_MIX:2 * CONV_HALO + TB_MIX, :] = modulated(xn_ref[...], j < nblk - 1)
    yield
    left = SSD_CONV // 2
    ct = CONV_TILE
    n_ct = CONV_DIM // ct
    n_ext = TB_MIX + 2 * CONV_HALO
    conv_base = SUBLANES + CONV_HALO

    def project_xbc(c):
        xbc = _dot(hm_scr[...], wm_ref[:, W_XBC_AT + c * ct:W_XBC_AT + (c + 1) * ct])
        for k in range(SSD_CONV):
            ext_refs[c][k, SUBLANES + left - k:SUBLANES + left - k + n_ext, :] = xbc

    def conv(c):
        cols = slice(c * ct, (c + 1) * ct)
        acc = cb_ref[:, cols] + ext_refs[c][0, conv_base:conv_base + TB_MIX, :] * cw_ref[0:1, cols]
        for k in range(1, SSD_CONV):
            acc = acc + ext_refs[c][k, conv_base:conv_base + TB_MIX, :] * cw_ref[k:k + 1, cols]
        y = jax.nn.silu(acc)
        if c < D_INNER // ct:
            for ci in range(CHUNKS_PER_TILE):
                xt_ref[ci, cols, :] = y[ci * SSD_CHUNK:(ci + 1) * SSD_CHUNK, :].T.astype(BF16)
        elif c == D_INNER // ct:
            bc_ref[...] = y.astype(BF16)
        else:
            cc_ref[...] = y.astype(BF16)

    project_xbc(0)
    yield
    for c in range(n_ct):
        if c + 1 < n_ct:
            project_xbc(c + 1)
        if c < D_INNER // ct:
            zc = slice(c * ct, (c + 1) * ct)
            z_ref[:, zc] = _dot(hm_scr[tile, :], wm_ref[:, W_Z_AT + c * ct:W_Z_AT + (c + 1) * ct]).astype(z_ref.dtype)
        conv(c)
        yield
    dt_ref[...] = jax.nn.softplus(_dot(hm_scr[tile, :], wt_ref[:, 0:LANES]) + dtb_ref[...])
    yield
    for c in range(2 * D_MODEL // PROJ_COLS):
        gc = slice(c * PROJ_COLS, (c + 1) * PROJ_COLS)
        gt_ref[:, gc] = _dot(hm_scr[tile, :],
                             wt_ref[:, LANES + c * PROJ_COLS:LANES + (c + 1) * PROJ_COLS]).astype(gt_ref.dtype)
        yield
    for c in range(POOL_DIM // PROJ_COLS):
        pc = slice(c * PROJ_COLS, (c + 1) * PROJ_COLS)
        up_ref[:, pc] = _dot(hm_scr[tile, :], wm_ref[:, pc])
        yield


def _mixf_kernel(*refs, zero_init, emit_state, nblk, ntiles):
    x_ref, xp_ref, xn_ref, mod_ref, g_ref, wm_ref, wt_ref, dtb_ref, cw_ref, cb_ref, alog_ref = refs[:11]
    refs = refs[11:]
    if not zero_init:
        h0_ref, refs = refs[0], refs[1:]
    up_ref, z_ref, gt_ref, dt_ref, xt_ref, bc_ref, cc_ref, y1_ref = refs[:8]
    refs = refs[8:]
    if emit_state:
        hf_ref, refs = refs[0], refs[1:]
    hm_scr, xt_cur, bc_cur, cc_cur, dt_cur, h_scr = refs[:6]
    ext_refs = refs[6:]
    s = pl.program_id(0)
    ja = lax.rem(jnp.minimum(s, ntiles - 1), nblk)
    jb = lax.rem(jnp.maximum(s - 1, 0), nblk)

    @pl.when(s == 0)
    def _():
        xt_cur[...] = jnp.zeros_like(xt_cur)
        bc_cur[...] = jnp.zeros_like(bc_cur)
        cc_cur[...] = jnp.zeros_like(cc_cur)
        dt_cur[...] = jnp.ones_like(dt_cur)

    @pl.when(jb == 0)
    def _():
        h_scr[...] = jnp.zeros_like(h_scr) if zero_init else h0_ref[...]

    proj = _project_and_conv(ja, nblk, x_ref, xp_ref, xn_ref, mod_ref, g_ref, wm_ref, wt_ref,
                             dtb_ref, cw_ref, cb_ref, up_ref, z_ref, gt_ref, dt_ref, xt_ref, bc_ref,
                             cc_ref, hm_scr, ext_refs)
    scan = (None for ci in range(CHUNKS_PER_TILE)
            for _ in _ssd_fwd_chunk(ci, xt_cur, bc_cur, cc_cur, dt_cur, alog_ref, y1_ref, h_scr))
    for n_scan in SCAN_SCHEDULE:
        next(proj, None)
        for _ in range(n_scan):
            next(scan, None)
    for _ in proj:
        pass
    for _ in scan:
        pass

    if emit_state:
        @pl.when(jb == nblk - 1)
        def _():
            hf_ref[...] = h_scr[...]

    xt_cur[...] = xt_ref[...]
    bc_cur[...] = bc_ref[...]
    cc_cur[...] = cc_ref[...]
    dt_cur[...] = dt_ref[...]


def _mix_fwd(x, mod, g, lw, h0_f, layer, emit_state):
    b, t, _ = x.shape
    nblk = t // TB_MIX
    ntiles = b * nblk
    nc = t // SSD_CHUNK
    per = TB_MIX // CONV_HALO
    zero_init = h0_f is None
    mod_idx = (lambda i: i) if mod.shape[0] == b else (lambda i: 0)
    ta = lambda s: jnp.minimum(s, ntiles - 1)
    tb = lambda s: jnp.maximum(s - 1, 0)
    tok = lambda n: pl.BlockSpec((None, TB_MIX, n), lambda s: (ta(s) // nblk, ta(s) % nblk, 0))
    chunked = lambda tile: pl.BlockSpec((None, CHUNKS_PER_TILE, D_INNER, SSD_CHUNK),
                                        lambda s: (tile(s) // nblk, tile(s) % nblk, 0, 0))
    in_specs = [
        tok(D_MODEL),
        pl.BlockSpec((None, CONV_HALO, D_MODEL),
                     lambda s: (ta(s) // nblk, jnp.maximum((ta(s) % nblk) * per - 1, 0), 0)),
        pl.BlockSpec((None, CONV_HALO, D_MODEL),
                     lambda s: (ta(s) // nblk, jnp.minimum((ta(s) % nblk + 1) * per, t // CONV_HALO - 1), 0)),
        pl.BlockSpec((None, 3, D_MODEL), lambda s: (mod_idx(ta(s) // nblk), 0, 0)),
        _resident((1, D_MODEL)),
        _resident((D_MODEL, W_MAIN_COLS), layer),
        _resident((D_MODEL, LANES + 2 * D_MODEL), layer),
        _resident((1, LANES)),
        _resident((SSD_CONV, CONV_DIM)),
        _resident((1, CONV_DIM)),
        _resident((1, LANES)),
    ]
    args = [x, x, x, mod, g, lw["w_in"], lw["w_tail"], lw["dt_bias"], lw["conv_w"], lw["conv_b"], lw["a_log"]]
    if not zero_init:
        in_specs.append(pl.BlockSpec((None, None, D_INNER, SSD_STATE), lambda s: (tb(s) // nblk, layer, 0, 0)))
        args.append(h0_f)
    out_specs = [tok(POOL_DIM), tok(D_INNER), tok(2 * D_MODEL), tok(LANES), chunked(ta), tok(BC_DIM), tok(BC_DIM),
                 chunked(tb)]
    out_shape = [
        jax.ShapeDtypeStruct((b, t, POOL_DIM), F32),
        jax.ShapeDtypeStruct((b, t, D_INNER), BF16),
        jax.ShapeDtypeStruct((b, t, 2 * D_MODEL), BF16),
        jax.ShapeDtypeStruct((b, t, LANES), F32),
        jax.ShapeDtypeStruct((b, nc, D_INNER, SSD_CHUNK), BF16),
        jax.ShapeDtypeStruct((b, t, BC_DIM), BF16),
        jax.ShapeDtypeStruct((b, t, BC_DIM), BF16),
        jax.ShapeDtypeStruct((b, nc, D_INNER, SSD_CHUNK), BF16),
    ]
    if emit_state:
        out_specs.append(pl.BlockSpec((None, D_INNER, SSD_STATE), lambda s: (tb(s) // nblk, 0, 0)))
        out_shape.append(jax.ShapeDtypeStruct((b, D_INNER, SSD_STATE), F32))
    scratch = [
        pltpu.VMEM((TB_MIX + 2 * CONV_HALO, D_MODEL), BF16),
        pltpu.VMEM((CHUNKS_PER_TILE, D_INNER, SSD_CHUNK), BF16),
        pltpu.VMEM((TB_MIX, BC_DIM), BF16),
        pltpu.VMEM((TB_MIX, BC_DIM), BF16),
        pltpu.VMEM((TB_MIX, LANES), F32),
        pltpu.VMEM((D_INNER, SSD_STATE), F32),
    ] + [pltpu.VMEM((SSD_CONV, TB_MIX + 2 * CONV_HALO + 2 * SUBLANES, CONV_TILE), F32)
         for _ in range(CONV_DIM // CONV_TILE)]
    return pl.pallas_call(
        functools.partial(_mixf_kernel, zero_init=zero_init, emit_state=emit_state, nblk=nblk, ntiles=ntiles),
        grid=(ntiles + 1,), in_specs=in_specs, out_specs=out_specs, out_shape=out_shape,
        scratch_shapes=scratch,
        compiler_params=_params(("arbitrary",)),
        name="mix_fwd",
    )(*args)


def _mixb_kernel(*refs, zero_init, emit_state, nblk):
    (x_ref, xt_ref, bc_ref, cc_ref, dt_ref, y1_ref, z_ref, gt_ref, yp_ref, mod_ref, alog_ref, dsk_ref,
     ng_ref, wbp_ref, wbs_ref, wo_ref) = refs[:16]
    refs = refs[16:]
    if not zero_init:
        h0_ref, refs = refs[0], refs[1:]
    o_ref, refs = refs[0], refs[1:]
    if emit_state:
        hb_ref, refs = refs[0], refs[1:]
    y_scr, h_scr = refs
    j = pl.program_id(1)

    @pl.when(j == 0)
    def _():
        h_scr[...] = jnp.zeros_like(h_scr) if zero_init else h0_ref[...]

    dsk = dsk_ref[...]
    for ci in reversed(range(CHUNKS_PER_TILE)):
        _ssd_bwd_chunk(ci, xt_ref, bc_ref, cc_ref, dt_ref, alog_ref, dsk, y1_ref, y_scr, h_scr)

    if emit_state:
        @pl.when(j == nblk - 1)
        def _():
            hb_ref[...] = h_scr[...]

    y = _rmsnorm(y_scr[...] * jax.nn.silu(z_ref[...].astype(F32)), ng_ref[...])
    gates = jax.nn.sigmoid(gt_ref[...].astype(F32))
    merged = (gates[:, :D_MODEL] * _dot(yp_ref[...].astype(BF16), wbp_ref[...])
              + gates[:, D_MODEL:] * _dot(y.astype(BF16), wbs_ref[...]))
    o_ref[...] = x_ref[...] + mod_ref[2:3, :] * _dot(merged.astype(BF16), wo_ref[...])


def _mix_bwd(x, fwd, y_pool, mod, lw, h0_b, layer, emit_state):
    b, t, _ = x.shape
    nblk = t // TB_MIX
    zero_init = h0_b is None
    z, gates, dt, xt, bcv, ccv, y1 = fwd
    mod_idx = (lambda i: i) if mod.shape[0] == b else (lambda i: 0)
    tok = lambda n: pl.BlockSpec((None, TB_MIX, n), lambda i, j: (i, nblk - 1 - j, 0))
    chunked = pl.BlockSpec((None, CHUNKS_PER_TILE, D_INNER, SSD_CHUNK), lambda i, j: (i, nblk - 1 - j, 0, 0))
    in_specs = [
        tok(D_MODEL), chunked, tok(BC_DIM), tok(BC_DIM), tok(LANES), chunked, tok(D_INNER), tok(2 * D_MODEL),
        tok(POOL_DIM),
        pl.BlockSpec((None, 3, D_MODEL), lambda i, j: (mod_idx(i), 0, 0)),
        _resident((1, LANES)),
        _resident((LANES, LANES)),
        _resident((1, D_INNER)),
        _resident((POOL_DIM, D_MODEL), layer),
        _resident((D_INNER, D_MODEL), layer),
        _resident((D_MODEL, D_MODEL), layer),
    ]
    args = [x, xt, bcv, ccv, dt, y1, z, gates, y_pool, mod, lw["a_log"], lw["d_skip"], lw["ssd_norm"],
            lw["w_branch_pool"], lw["w_branch_ssd"], lw["w_out"]]
    if not zero_init:
        in_specs.append(pl.BlockSpec((None, None, D_INNER, SSD_STATE), lambda i, j: (i, layer, 0, 0)))
        args.append(h0_b)
    out_specs = [tok(D_MODEL)]
    out_shape = [jax.ShapeDtypeStruct((b, t, D_MODEL), F32)]
    if emit_state:
        out_specs.append(pl.BlockSpec((None, D_INNER, SSD_STATE), lambda i, j: (i, 0, 0)))
        out_shape.append(jax.ShapeDtypeStruct((b, D_INNER, SSD_STATE), F32))
    return pl.pallas_call(
        functools.partial(_mixb_kernel, zero_init=zero_init, emit_state=emit_state, nblk=nblk),
        grid=(b, nblk), in_specs=in_specs, out_specs=out_specs, out_shape=out_shape,
        scratch_shapes=[pltpu.VMEM((TB_MIX, D_INNER), F32), pltpu.VMEM((D_INNER, SSD_STATE), F32)],
        compiler_params=_params(("parallel", "arbitrary")),
        name="mix_bwd",
    )(*args)


def _layer(x, mod, rows, h0_f, h0_b, layer, lw, emit_state, final_g):
    ng = lw["norm_g"]
    x = _ffn(x, mod[:, 0:3], ng[0:1], lw["ffn1_w13"], lw["ffn1_w2"], layer)
    mod2 = mod[:, 3:6]
    fwd = _mix_fwd(x, mod2, ng[1:2], lw, h0_f, layer, emit_state)
    hf = fwd[8] if emit_state else None
    y_pool = _pool_mixer(fwd[0], lw["pool_w"], lw["pool_scale"], rows, layer)
    res = _mix_bwd(x, fwd[1:8], y_pool, mod2, lw, h0_b, layer, emit_state)
    x, hb = (res[0], res[1]) if emit_state else (res[0], None)
    x = _ffn(x, mod[:, 6:9], ng[2:3], lw["ffn2_w13"], lw["ffn2_w2"], layer, final_g)
    return x, hf, hb


def _prep_weights(p):
    s3 = W_MAIN_COLS
    s4 = s3 + 2 * SSD_HEADS
    w_in = p["w_in"]
    w_tail = jnp.concatenate(
        [jnp.pad(w_in[:, :, s3:s4], ((0, 0), (0, 0), (0, LANES - 2 * SSD_HEADS))), w_in[:, :, s4:]], axis=2)
    stacked = {k: p[k].astype(BF16) for k in ("ffn1_w13", "ffn1_w2", "ffn2_w13", "ffn2_w2", "w_in", "pool_w",
                                              "w_branch_pool", "w_branch_ssd", "w_out")}
    stacked["w_tail"] = w_tail.astype(BF16)
    return stacked


def _prep_layer(l, p, stacked):
    pad_lanes = lambda v: jnp.pad(v.reshape(1, -1), ((0, 0), (0, LANES - v.size)))
    return dict(
        stacked,
        norm_g=p["norm_g"][l],
        dt_bias=pad_lanes(p["dt_bias"][l]),
        a_log=pad_lanes(p["a_log"][l]),
        d_skip=jnp.broadcast_to(pad_lanes(p["d_skip"][l]).reshape(LANES, 1), (LANES, LANES)),
        pool_scale=p["pool_scale"][l].reshape(1, POOL_DIM),
        conv_w=p["conv_w"][l], conv_b=p["conv_b"][l].reshape(1, CONV_DIM),
        ssd_norm=p["ssd_norm"][l].reshape(1, D_INNER),
    )


def kernel(x_prompt, x_sample, state_ssd_fwd, state_ssd_bwd, c, c_ctx, ada_w, ada_b, norm_g, ffn1_w13, ffn1_w2, ffn2_w13, ffn2_w2, w_in, pool_w, pool_scale, conv_w, conv_b, a_log, dt_bias, d_skip, ssd_norm, w_branch_pool, w_branch_ssd, w_out, final_norm):
    p = dict(norm_g=norm_g, ffn1_w13=ffn1_w13, ffn1_w2=ffn1_w2, ffn2_w13=ffn2_w13, ffn2_w2=ffn2_w2,
             w_in=w_in, pool_w=pool_w, pool_scale=pool_scale, conv_w=conv_w, conv_b=conv_b, a_log=a_log,
             dt_bias=dt_bias, d_skip=d_skip, ssd_norm=ssd_norm, w_branch_pool=w_branch_pool,
             w_branch_ssd=w_branch_ssd, w_out=w_out)
    nb = c.shape[0]
    rows = x_sample.shape[1] // GRID_W
    cond = jnp.concatenate([c_ctx[None, :], c], axis=0)
    cond = jnp.pad(cond, ((0, -(nb + 1) % SUBLANES), (0, 0)))
    mod = _adaln(cond, ada_w, ada_b).reshape(DEPTH, cond.shape[0], N_MOD, D_MODEL)
    fg = final_norm.reshape(1, D_MODEL)
    sd = state_ssd_fwd.shape
    h0_f = state_ssd_fwd.reshape(sd[0], DEPTH, D_INNER, SSD_STATE)
    h0_b = state_ssd_bwd.reshape(sd[0], DEPTH, D_INNER, SSD_STATE)
    xc, xl = x_prompt, x_sample
    new_f, new_b = [], []
    stacked = _prep_weights(p)
    for l in range(DEPTH):
        lw = _prep_layer(l, p, stacked)
        last = fg if l == DEPTH - 1 else None
        xc, hf, hb = _layer(xc, mod[l, 0:1], None, None, None, l, lw, True, last)
        new_f.append(hf.reshape(-1, SSD_HEADS, SSD_HEAD_DIM, SSD_STATE))
        new_b.append(hb.reshape(-1, SSD_HEADS, SSD_HEAD_DIM, SSD_STATE))
        xl, _, _ = _layer(xl, mod[l, 1:1 + nb], rows, h0_f, h0_b, l, lw, False, last)
    return (xc, xl, jnp.stack(new_f, axis=1), jnp.stack(new_b, axis=1))
```

```python
import functools
import math

import jax
import jax.numpy as jnp
from jax import lax
from jax.experimental import pallas as pl
from jax.experimental.pallas import tpu as pltpu

F32 = jnp.float32
BF16 = jnp.bfloat16

D_MODEL = 1024
DEPTH = 2
GRID_W = 64
POOL_WINDOWS = (2, 4, 8, 16)
POOL_GROUPS = len(POOL_WINDOWS)
POOL_DIM = D_MODEL
POOL_GROUP_DIM = POOL_DIM // POOL_GROUPS
D_INNER = 2 * D_MODEL
SSD_HEAD_DIM = 64
SSD_HEADS = D_INNER // SSD_HEAD_DIM
SSD_GROUPS = 4
HEADS_PER_GROUP = SSD_HEADS // SSD_GROUPS
GROUP_DIM = HEADS_PER_GROUP * SSD_HEAD_DIM
SSD_STATE = 128
SSD_CONV = 4
SSD_CHUNK = 128
BC_DIM = SSD_GROUPS * SSD_STATE
CONV_DIM = D_INNER + 2 * BC_DIM
W_Z_AT = POOL_DIM
W_XBC_AT = W_Z_AT + D_INNER
W_MAIN_COLS = W_XBC_AT + CONV_DIM
D_FF = 2816
N_MOD = 9
EPS = 1e-6
LOG2E = math.log2(math.e)

LANES = 128
SUBLANES = 8
VMEM_LIMIT = 56 * 1024 * 1024

TM_FFN = 512
FF_CHUNK = 256
TB_MIX = 256
CHUNKS_PER_TILE = TB_MIX // SSD_CHUNK
CONV_TILE = BC_DIM
CONV_HALO = 16
POOL_BLOCK = 256
POOL_PAD_ROWS = 8
POOL_SEQ_PER_STEP = 16
POOL_UNROLL = 16
ADA_TN = 2304
PROJ_COLS = 256
SCAN_SCHEDULE = (0,) + (1,) * 7 + (3,) * 9 + (2,) * 4


def _dot(a, b):
    return jnp.dot(a, b, preferred_element_type=F32)


def _dot_nt(a, b):
    return lax.dot_general(a, b, (((1,), (1,)), ((), ())), preferred_element_type=F32)


def _rmsnorm(x, g):
    return x * lax.rsqrt(jnp.mean(x * x, axis=-1, keepdims=True) + EPS) * g


def _params(sem):
    return pltpu.CompilerParams(dimension_semantics=sem, vmem_limit_bytes=VMEM_LIMIT)


def _resident(shape, layer=None):
    nd = len(shape)
    if layer is None:
        return pl.BlockSpec(shape, lambda *_: (0,) * nd, pipeline_mode=pl.Buffered(1))
    return pl.BlockSpec((None,) + tuple(shape), lambda *_: (layer,) + (0,) * nd, pipeline_mode=pl.Buffered(1))


def _ada_kernel(cond_ref, w_ref, b_ref, o_ref):
    s = jax.nn.silu(cond_ref[...]).astype(BF16)
    o_ref[...] = _dot(s, w_ref[...].astype(BF16)) + b_ref[...]


def _adaln(cond, ada_w, ada_b):
    n = cond.shape[0]
    ncol = N_MOD * D_MODEL
    return pl.pallas_call(
        _ada_kernel,
        grid=(DEPTH, ncol // ADA_TN),
        in_specs=[
            pl.BlockSpec((n, D_MODEL), lambda l, j: (0, 0)),
            pl.BlockSpec((None, D_MODEL, ADA_TN), lambda l, j: (l, 0, j)),
            pl.BlockSpec((None, 1, ADA_TN), lambda l, j: (l, 0, j)),
        ],
        out_specs=pl.BlockSpec((None, n, ADA_TN), lambda l, j: (l, 0, j)),
        out_shape=jax.ShapeDtypeStruct((DEPTH, n, ncol), F32),
        compiler_params=_params(("parallel", "parallel")),
        name="adaln",
    )(cond, ada_w, ada_b.reshape(DEPTH, 1, ncol))


def _ffn_kernel(x_ref, mod_ref, g_ref, w13_ref, w2_ref, *rest, final):
    if final:
        fg_ref, o_ref, a_scr = rest
    else:
        o_ref, a_scr = rest
    x = x_ref[...]
    m = mod_ref[...]
    h = (_rmsnorm(x, g_ref[...]) * (1.0 + m[1:2]) + m[0:1]).astype(BF16)
    for j in range(D_FF // FF_CHUNK):
        lo, hi = j * FF_CHUNK, (j + 1) * FF_CHUNK
        gate = _dot(h, w13_ref[:, lo:hi])
        up = _dot(h, w13_ref[:, D_FF + lo:D_FF + hi])
        a_scr[:, lo:hi] = (jax.nn.silu(gate) * up).astype(BF16)
    y = _dot(a_scr[...], w2_ref[...])
    out = x + 0.5 * m[2:3] * y
    if final:
        out = _rmsnorm(out, fg_ref[...])
    o_ref[...] = out


def _ffn(x, mod, g, w13, w2, layer, final_g=None):
    b, t, _ = x.shape
    nt = b * t
    tpm = nt // mod.shape[0]
    final = final_g is not None
    in_specs = [
        pl.BlockSpec((TM_FFN, D_MODEL), lambda i: (i, 0)),
        pl.BlockSpec((None, 3, D_MODEL), lambda i: (i * TM_FFN // tpm, 0, 0)),
        _resident((1, D_MODEL)),
        _resident((D_MODEL, 2 * D_FF), layer),
        _resident((D_FF, D_MODEL), layer),
    ]
    args = [x.reshape(nt, D_MODEL), mod, g, w13, w2]
    if final:
        in_specs.append(_resident((1, D_MODEL)))
        args.append(final_g)
    out = pl.pallas_call(
        functools.partial(_ffn_kernel, final=final),
        grid=(nt // TM_FFN,),
        in_specs=in_specs,
        out_specs=pl.BlockSpec((TM_FFN, D_MODEL), lambda i: (i, 0)),
        out_shape=jax.ShapeDtypeStruct((nt, D_MODEL), F32),
        scratch_shapes=[pltpu.VMEM((TM_FFN, D_FF), BF16)],
        compiler_params=_params(("parallel",)),
        name="ffn_final" if final else "ffn",
    )(*args)
    return out.reshape(b, t, D_MODEL)


def _window_count(idx, w, n):
    lo = jnp.maximum(idx - w // 2, 0)
    hi = jnp.minimum(idx - w // 2 + w, n)
    return (hi - lo).astype(F32)


def _band_sum(band, u):
    hi = u.astype(BF16)
    lo = (u - hi.astype(F32)).astype(BF16)
    return _dot(band, hi) + _dot(band, lo)


def _pool_finish(tot, cnt, u, pw_ref, ps_ref):
    d = (tot / cnt - u).astype(BF16)
    return (_dot(d, pw_ref[...]) * ps_ref[...]).astype(BF16)


def _pool_seq_body(u_ref, pw_ref, ps_ref, o_ref, *, w, t):
    ti = lax.broadcasted_iota(jnp.int32, (t, t), 0)
    si = lax.broadcasted_iota(jnp.int32, (t, t), 1)
    band = jnp.where((si >= ti - w // 2) & (si < ti - w // 2 + w), 1.0, 0.0).astype(BF16)
    cnt = _window_count(lax.broadcasted_iota(jnp.int32, (t, POOL_GROUP_DIM), 0), w, t)
    for s in range(u_ref.shape[0]):
        u = u_ref[s]
        o_ref[s] = _pool_finish(_band_sum(band, u), cnt, u, pw_ref, ps_ref)


def _pool_grid_body(u_ref, pw_ref, ps_ref, o_ref, cp_ref, *, w, rows):
    nblk = rows * GRID_W // POOL_BLOCK
    unroll = math.gcd(nblk, POOL_UNROLL)
    pad = POOL_PAD_ROWS * GRID_W
    ti = lax.broadcasted_iota(jnp.int32, (POOL_BLOCK, POOL_BLOCK), 0)
    si = lax.broadcasted_iota(jnp.int32, (POOL_BLOCK, POOL_BLOCK), 1)
    tc = ti & (GRID_W - 1)
    sc = si & (GRID_W - 1)
    same_row = (ti >> 6) == (si >> 6)
    band = jnp.where(same_row & (sc >= tc - w // 2) & (sc < tc - w // 2 + w), 1.0, 0.0).astype(BF16)
    cp_ref[0:pad + GRID_W, :] = jnp.zeros((pad + GRID_W, POOL_GROUP_DIM), F32)
    rows_per_block = POOL_BLOCK // GRID_W

    def col_pool(i, acc):
        off = pl.multiple_of(i * POOL_BLOCK, POOL_BLOCK)
        cp = _band_sum(band, u_ref[pl.ds(off, POOL_BLOCK), :])
        prefix = []
        for k in range(rows_per_block):
            acc = acc + cp[k * GRID_W:(k + 1) * GRID_W, :]
            prefix.append(acc)
        cp_ref[pl.ds(pad + GRID_W + off, POOL_BLOCK), :] = jnp.concatenate(prefix, axis=0)
        return acc

    total = lax.fori_loop(0, nblk, col_pool, jnp.zeros((GRID_W, POOL_GROUP_DIM), F32), unroll=unroll)
    for k in range(1, POOL_PAD_ROWS):
        cp_ref[pad + (rows + k) * GRID_W:pad + (rows + k + 1) * GRID_W, :] = total

    tt = lax.broadcasted_iota(jnp.int32, (POOL_BLOCK, POOL_GROUP_DIM), 0)
    ccnt = _window_count(tt & (GRID_W - 1), w, GRID_W)

    def row_pool(i, carry):
        off = pl.multiple_of(i * POOL_BLOCK, POOL_BLOCK)
        tot = (cp_ref[pl.ds(pad + off + (w // 2) * GRID_W, POOL_BLOCK), :]
               - cp_ref[pl.ds(pad + off - (w // 2) * GRID_W, POOL_BLOCK), :])
        r = i * rows_per_block + (tt >> 6)
        cnt = _window_count(r, w, rows) * ccnt
        o_ref[pl.ds(off, POOL_BLOCK), :] = _pool_finish(tot, cnt, u_ref[pl.ds(off, POOL_BLOCK), :], pw_ref, ps_ref)
        return carry

    lax.fori_loop(0, nblk, row_pool, 0, unroll=unroll)


def _pool_kernel(u_ref, pw_ref, ps_ref, o_ref, *scratch, rows, t):
    g = pl.program_id(1)
    for gi, w in enumerate(POOL_WINDOWS):
        @pl.when(g == gi)
        def _(w=w):
            if rows is None:
                _pool_seq_body(u_ref, pw_ref, ps_ref, o_ref, w=w, t=t)
            else:
                _pool_grid_body(u_ref, pw_ref, ps_ref, o_ref, scratch[0], w=w, rows=rows)


def _pool_mixer(u, pool_w, pool_scale, rows, layer):
    b, t, _ = u.shape
    if rows is None:
        per_step = math.gcd(b, POOL_SEQ_PER_STEP)
        scratch = []
        blk = pl.BlockSpec((per_step, t, POOL_GROUP_DIM), lambda i, g: (i, 0, g))
    else:
        per_step = 1
        scratch = [pltpu.VMEM(((rows + 2 * POOL_PAD_ROWS) * GRID_W, POOL_GROUP_DIM), F32)]
        blk = pl.BlockSpec((None, t, POOL_GROUP_DIM), lambda i, g: (i, 0, g))
    return pl.pallas_call(
        functools.partial(_pool_kernel, rows=rows, t=t),
        grid=(b // per_step, POOL_GROUPS),
        in_specs=[
            blk,
            pl.BlockSpec((None, None, POOL_GROUP_DIM, POOL_GROUP_DIM), lambda i, g: (layer, g, 0, 0)),
            pl.BlockSpec((1, POOL_GROUP_DIM), lambda i, g: (0, g)),
        ],
        out_specs=blk,
        out_shape=jax.ShapeDtypeStruct((b, t, POOL_DIM), BF16),
        scratch_shapes=scratch,
        compiler_params=_params(("parallel", "parallel")),
        name="pool_seq" if rows is None else "pool_grid",
    )(u, pool_w, pool_scale)


def _ssd_chunk_terms(dtc, alog):
    n = SSD_CHUNK
    da = dtc * (-jnp.exp(alog) * LOG2E)
    row = lax.broadcasted_iota(jnp.int32, da.shape, 0)
    cs = da
    rcs = da
    k = 1
    while k < n:
        cs = cs + jnp.where(row >= k, pltpu.roll(cs, k, 0), 0.0)
        rcs = rcs + jnp.where(row < n - k, pltpu.roll(rcs, n - k, 0), 0.0)
        k *= 2
    logdt = jnp.log(dtc) * LOG2E
    return dict(colf=cs - logdt, colb=rcs - logdt, cst=cs.T, rcst=rcs.T, dtt=dtc.T)


def _rows64(mat, h):
    return jnp.broadcast_to(mat[h:h + 1, :], (SSD_HEAD_DIM, mat.shape[1]))


def _pair_rows(mat, h0):
    return jnp.concatenate([_rows64(mat, h0), _rows64(mat, h0 + 1)], axis=0)


def _state_update(h_scr, g, xw, bc_g, dec_t, head_off):
    rows = slice(g * GROUP_DIM, (g + 1) * GROUP_DIM)
    decay = jnp.concatenate(
        [_rows64(dec_t, head_off + g * HEADS_PER_GROUP + h) for h in range(HEADS_PER_GROUP)], axis=0)
    h_scr[rows, :] = h_scr[rows, :] * decay + _dot(xw, bc_g)


def _ssd_fwd_chunk(ci, xt_ref, bc_ref, cc_ref, dt_ref, alog_ref, y1_ref, h_scr):
    n = SSD_CHUNK
    tok = slice(ci * n, (ci + 1) * n)
    tm = _ssd_chunk_terms(dt_ref[tok, :], alog_ref[...])
    cst, rcst, dtt = tm["cst"], tm["rcst"], tm["dtt"]
    tot = cst[:, n - 1:n]
    eoff_t = jnp.exp2(cst)
    w_t = dtt * jnp.exp2(tot - cst)
    dec_t = jnp.exp2(jnp.broadcast_to(tot, (LANES, SSD_STATE)))
    si = lax.broadcasted_iota(jnp.int32, (n, n), 0)
    li = lax.broadcasted_iota(jnp.int32, (n, n), 1)
    src_before = si <= li
    src_after = si >= li
    zeros64 = jnp.zeros((SSD_HEAD_DIM, n), BF16)
    yield
    for g in range(SSD_GROUPS):
        bc_g = bc_ref[tok, g * SSD_STATE:(g + 1) * SSD_STATE]
        cc_g = cc_ref[tok, g * SSD_STATE:(g + 1) * SSD_STATE]
        sc_t = _dot_nt(bc_g, cc_g)
        y_off = _dot_nt(h_scr[g * GROUP_DIM:(g + 1) * GROUP_DIM, :].astype(BF16), cc_g)
        xw = []
        for j in range(HEADS_PER_GROUP // 2):
            h0 = g * HEADS_PER_GROUP + 2 * j
            mts = []
            for h in (h0, h0 + 1):
                seg_f = cst[h:h + 1, :] - tm["colf"][:, h:h + 1]
                hb = SSD_HEADS + h
                seg_b = rcst[hb:hb + 1, :] - tm["colb"][:, hb:hb + 1]
                dec = (jnp.exp2(jnp.where(src_before, seg_f, -jnp.inf))
                       + jnp.exp2(jnp.where(src_after, seg_b, -jnp.inf)))
                mts.append((sc_t * dec).astype(BF16))
            r0 = h0 * SSD_HEAD_DIM
            xp = xt_ref[ci, r0:r0 + 2 * SSD_HEAD_DIM, :]
            lhs = jnp.concatenate([jnp.concatenate([xp[:SSD_HEAD_DIM], zeros64], axis=0),
                                   jnp.concatenate([zeros64, xp[SSD_HEAD_DIM:]], axis=0)], axis=1)
            y_diag = _dot(lhs, jnp.concatenate(mts, axis=0))
            yo = y_off[2 * j * SSD_HEAD_DIM:(2 * j + 2) * SSD_HEAD_DIM, :]
            y1_ref[ci, r0:r0 + 2 * SSD_HEAD_DIM, :] = (y_diag + yo * _pair_rows(eoff_t, h0)).astype(y1_ref.dtype)
            xw.append((xp.astype(F32) * _pair_rows(w_t, h0)).astype(BF16))
            yield
        _state_update(h_scr, g, jnp.concatenate(xw, axis=0), bc_g, dec_t, 0)
        yield


def _ssd_bwd_chunk(ci, xt_ref, bc_ref, cc_ref, dt_ref, alog_ref, dsk, y1_ref, y_scr, h_scr):
    n = SSD_CHUNK
    tok = slice(ci * n, (ci + 1) * n)
    tm = _ssd_chunk_terms(dt_ref[tok, :], alog_ref[...])
    rcst, dtt = tm["rcst"], tm["dtt"]
    tot = rcst[:, 0:1]
    eoff_t = jnp.exp2(rcst)
    w_t = dtt * jnp.exp2(tot - rcst)
    dec_t = jnp.exp2(jnp.broadcast_to(tot, (LANES, SSD_STATE)))
    for g in range(SSD_GROUPS):
        bc_g = bc_ref[tok, g * SSD_STATE:(g + 1) * SSD_STATE]
        cc_g = cc_ref[tok, g * SSD_STATE:(g + 1) * SSD_STATE]
        y_off = _dot_nt(h_scr[g * GROUP_DIM:(g + 1) * GROUP_DIM, :].astype(BF16), cc_g)
        xw, yt = [], []
        for j in range(HEADS_PER_GROUP // 2):
            h0 = g * HEADS_PER_GROUP + 2 * j
            r0 = h0 * SSD_HEAD_DIM
            xp = xt_ref[ci, r0:r0 + 2 * SSD_HEAD_DIM, :].astype(F32)
            yo = y_off[2 * j * SSD_HEAD_DIM:(2 * j + 2) * SSD_HEAD_DIM, :]
            yt.append(y1_ref[ci, r0:r0 + 2 * SSD_HEAD_DIM, :].astype(F32)
                      + yo * _pair_rows(eoff_t, SSD_HEADS + h0) + _pair_rows(dsk, h0) * xp)
            xw.append((xp * _pair_rows(w_t, SSD_HEADS + h0)).astype(BF16))
        y_scr[tok, g * GROUP_DIM:(g + 1) * GROUP_DIM] = jnp.concatenate(yt, axis=0).T
        _state_update(h_scr, g, jnp.concatenate(xw, axis=0), bc_g, dec_t, SSD_HEADS)


def _project_and_conv(j, nblk, x_ref, xp_ref, xn_ref, mod_ref, g_ref, wm_ref, wt_ref,
                      dtb_ref, cw_ref, cb_ref, up_ref, z_ref, gt_ref, dt_ref, xt_ref, bc_ref, cc_ref,
                      hm_scr, ext_refs):
    m = mod_ref[...]
    g = g_ref[...]

    def modulated(x_rows, keep=True):
        hm = _rmsnorm(x_rows, g) * (1.0 + m[1:2]) + m[0:1]
        return jnp.where(keep, hm, 0.0).astype(BF16)

    tile = slice(CONV_HALO, CONV_HALO + TB_MIX)
    hm_scr[0:CONV_HALO, :] = modulated(xp_ref[...], j > 0)
    hm_scr[tile, :] = modulated(x_ref[...])
    hm_scr[CONV_HALO + TB_MIX:2 * CONV_HALO + TB_MIX, :] = modulated(xn_ref[...], j < nblk - 1)
    yield
    left = SSD_CONV // 2
    ct = CONV_TILE
    n_ct = CONV_DIM // ct
    n_ext = TB_MIX + 2 * CONV_HALO
    conv_base = SUBLANES + CONV_HALO

    def project_xbc(c):
        xbc = _dot(hm_scr[...], wm_ref[:, W_XBC_AT + c * ct:W_XBC_AT + (c + 1) * ct])
        for k in range(SSD_CONV):
            ext_refs[c][k, SUBLANES + left - k:SUBLANES + left - k + n_ext, :] = xbc

    def conv(c):
        cols = slice(c * ct, (c + 1) * ct)
        acc = cb_ref[:, cols] + ext_refs[c][0, conv_base:conv_base + TB_MIX, :] * cw_ref[0:1, cols]
        for k in range(1, SSD_CONV):
            acc = acc + ext_refs[c][k, conv_base:conv_base + TB_MIX, :] * cw_ref[k:k + 1, cols]
        y = jax.nn.silu(acc)
        if c < D_INNER // ct:
            for ci in range(CHUNKS_PER_TILE):
                xt_ref[ci, cols, :] = y[ci * SSD_CHUNK:(ci + 1) * SSD_CHUNK, :].T.astype(BF16)
        elif c == D_INNER // ct:
            bc_ref[...] = y.astype(BF16)
        else:
            cc_ref[...] = y.astype(BF16)

    project_xbc(0)
    yield
    for c in range(n_ct):
        if c + 1 < n_ct:
            project_xbc(c + 1)
        if c < D_INNER // ct:
            zc = slice(c * ct, (c + 1) * ct)
            z_ref[:, zc] = _dot(hm_scr[tile, :], wm_ref[:, W_Z_AT + c * ct:W_Z_AT + (c + 1) * ct]).astype(z_ref.dtype)
        conv(c)
        yield
    dt_ref[...] = jax.nn.softplus(_dot(hm_scr[tile, :], wt_ref[:, 0:LANES]) + dtb_ref[...])
    yield
    for c in range(2 * D_MODEL // PROJ_COLS):
        gc = slice(c * PROJ_COLS, (c + 1) * PROJ_COLS)
        gt_ref[:, gc] = _dot(hm_scr[tile, :],
                             wt_ref[:, LANES + c * PROJ_COLS:LANES + (c + 1) * PROJ_COLS]).astype(gt_ref.dtype)
        yield
    for c in range(POOL_DIM // PROJ_COLS):
        pc = slice(c * PROJ_COLS, (c + 1) * PROJ_COLS)
        up_ref[:, pc] = _dot(hm_scr[tile, :], wm_ref[:, pc])
        yield


def _mixf_kernel(*refs, zero_init, emit_state, nblk, ntiles):
    x_ref, xp_ref, xn_ref, mod_ref, g_ref, wm_ref, wt_ref, dtb_ref, cw_ref, cb_ref, alog_ref = refs[:11]
    refs = refs[11:]
    if not zero_init:
        h0_ref, refs = refs[0], refs[1:]
    up_ref, z_ref, gt_ref, dt_ref, xt_ref, bc_ref, cc_ref, y1_ref = refs[:8]
    refs = refs[8:]
    if emit_state:
        hf_ref, refs = refs[0], refs[1:]
    hm_scr, xt_cur, bc_cur, cc_cur, dt_cur, h_scr = refs[:6]
    ext_refs = refs[6:]
    s = pl.program_id(0)
    ja = lax.rem(jnp.minimum(s, ntiles - 1), nblk)
    jb = lax.rem(jnp.maximum(s - 1, 0), nblk)

    @pl.when(s == 0)
    def _():
        xt_cur[...] = jnp.zeros_like(xt_cur)
        bc_cur[...] = jnp.zeros_like(bc_cur)
        cc_cur[...] = jnp.zeros_like(cc_cur)
        dt_cur[...] = jnp.ones_like(dt_cur)

    @pl.when(jb == 0)
    def _():
        h_scr[...] = jnp.zeros_like(h_scr) if zero_init else h0_ref[...]

    proj = _project_and_conv(ja, nblk, x_ref, xp_ref, xn_ref, mod_ref, g_ref, wm_ref, wt_ref,
                             dtb_ref, cw_ref, cb_ref, up_ref, z_ref, gt_ref, dt_ref, xt_ref, bc_ref,
                             cc_ref, hm_scr, ext_refs)
    scan = (None for ci in range(CHUNKS_PER_TILE)
            for _ in _ssd_fwd_chunk(ci, xt_cur, bc_cur, cc_cur, dt_cur, alog_ref, y1_ref, h_scr))
    for n_scan in SCAN_SCHEDULE:
        next(proj, None)
        for _ in range(n_scan):
            next(scan, None)
    for _ in proj:
        pass
    for _ in scan:
        pass

    if emit_state:
        @pl.when(jb == nblk - 1)
        def _():
            hf_ref[...] = h_scr[...]

    xt_cur[...] = xt_ref[...]
    bc_cur[...] = bc_ref[...]
    cc_cur[...] = cc_ref[...]
    dt_cur[...] = dt_ref[...]


def _mix_fwd(x, mod, g, lw, h0_f, layer, emit_state):
    b, t, _ = x.shape
    nblk = t // TB_MIX
    ntiles = b * nblk
    nc = t // SSD_CHUNK
    per = TB_MIX // CONV_HALO
    zero_init = h0_f is None
    mod_idx = (lambda i: i) if mod.shape[0] == b else (lambda i: 0)
    ta = lambda s: jnp.minimum(s, ntiles - 1)
    tb = lambda s: jnp.maximum(s - 1, 0)
    tok = lambda n: pl.BlockSpec((None, TB_MIX, n), lambda s: (ta(s) // nblk, ta(s) % nblk, 0))
    chunked = lambda tile: pl.BlockSpec((None, CHUNKS_PER_TILE, D_INNER, SSD_CHUNK),
                                        lambda s: (tile(s) // nblk, tile(s) % nblk, 0, 0))
    in_specs = [
        tok(D_MODEL),
        pl.BlockSpec((None, CONV_HALO, D_MODEL),
                     lambda s: (ta(s) // nblk, jnp.maximum((ta(s) % nblk) * per - 1, 0), 0)),
        pl.BlockSpec((None, CONV_HALO, D_MODEL),
                     lambda s: (ta(s) // nblk, jnp.minimum((ta(s) % nblk + 1) * per, t // CONV_HALO - 1), 0)),
        pl.BlockSpec((None, 3, D_MODEL), lambda s: (mod_idx(ta(s) // nblk), 0, 0)),
        _resident((1, D_MODEL)),
        _resident((D_MODEL, W_MAIN_COLS), layer),
        _resident((D_MODEL, LANES + 2 * D_MODEL), layer),
        _resident((1, LANES)),
        _resident((SSD_CONV, CONV_DIM)),
        _resident((1, CONV_DIM)),
        _resident((1, LANES)),
    ]
    args = [x, x, x, mod, g, lw["w_in"], lw["w_tail"], lw["dt_bias"], lw["conv_w"], lw["conv_b"], lw["a_log"]]
    if not zero_init:
        in_specs.append(pl.BlockSpec((None, None, D_INNER, SSD_STATE), lambda s: (tb(s) // nblk, layer, 0, 0)))
        args.append(h0_f)
    out_specs = [tok(POOL_DIM), tok(D_INNER), tok(2 * D_MODEL), tok(LANES), chunked(ta), tok(BC_DIM), tok(BC_DIM),
                 chunked(tb)]
    out_shape = [
        jax.ShapeDtypeStruct((b, t, POOL_DIM), F32),
        jax.ShapeDtypeStruct((b, t, D_INNER), BF16),
        jax.ShapeDtypeStruct((b, t, 2 * D_MODEL), BF16),
        jax.ShapeDtypeStruct((b, t, LANES), F32),
        jax.ShapeDtypeStruct((b, nc, D_INNER, SSD_CHUNK), BF16),
        jax.ShapeDtypeStruct((b, t, BC_DIM), BF16),
        jax.ShapeDtypeStruct((b, t, BC_DIM), BF16),
        jax.ShapeDtypeStruct((b, nc, D_INNER, SSD_CHUNK), BF16),
    ]
    if emit_state:
        out_specs.append(pl.BlockSpec((None, D_INNER, SSD_STATE), lambda s: (tb(s) // nblk, 0, 0)))
        out_shape.append(jax.ShapeDtypeStruct((b, D_INNER, SSD_STATE), F32))
    scratch = [
        pltpu.VMEM((TB_MIX + 2 * CONV_HALO, D_MODEL), BF16),
        pltpu.VMEM((CHUNKS_PER_TILE, D_INNER, SSD_CHUNK), BF16),
        pltpu.VMEM((TB_MIX, BC_DIM), BF16),
        pltpu.VMEM((TB_MIX, BC_DIM), BF16),
        pltpu.VMEM((TB_MIX, LANES), F32),
        pltpu.VMEM((D_INNER, SSD_STATE), F32),
    ] + [pltpu.VMEM((SSD_CONV, TB_MIX + 2 * CONV_HALO + 2 * SUBLANES, CONV_TILE), F32)
         for _ in range(CONV_DIM // CONV_TILE)]
    return pl.pallas_call(
        functools.partial(_mixf_kernel, zero_init=zero_init, emit_state=emit_state, nblk=nblk, ntiles=ntiles),
        grid=(ntiles + 1,), in_specs=in_specs, out_specs=out_specs, out_shape=out_shape,
        scratch_shapes=scratch,
        compiler_params=_params(("arbitrary",)),
        name="mix_fwd",
    )(*args)


def _mixb_kernel(*refs, zero_init, emit_state, nblk):
    (x_ref, xt_ref, bc_ref, cc_ref, dt_ref, y1_ref, z_ref, gt_ref, yp_ref, mod_ref, alog_ref, dsk_ref,
     ng_ref, wbp_ref, wbs_ref, wo_ref) = refs[:16]
    refs = refs[16:]
    if not zero_init:
        h0_ref, refs = refs[0], refs[1:]
    o_ref, refs = refs[0], refs[1:]
    if emit_state:
        hb_ref, refs = refs[0], refs[1:]
    y_scr, h_scr = refs
    j = pl.program_id(1)

    @pl.when(j == 0)
    def _():
        h_scr[...] = jnp.zeros_like(h_scr) if zero_init else h0_ref[...]

    dsk = dsk_ref[...]
    for ci in reversed(range(CHUNKS_PER_TILE)):
        _ssd_bwd_chunk(ci, xt_ref, bc_ref, cc_ref, dt_ref, alog_ref, dsk, y1_ref, y_scr, h_scr)

    if emit_state:
        @pl.when(j == nblk - 1)
        def _():
            hb_ref[...] = h_scr[...]

    y = _rmsnorm(y_scr[...] * jax.nn.silu(z_ref[...].astype(F32)), ng_ref[...])
    gates = jax.nn.sigmoid(gt_ref[...].astype(F32))
    merged = (gates[:, :D_MODEL] * _dot(yp_ref[...], wbp_ref[...])
              + gates[:, D_MODEL:] * _dot(y.astype(BF16), wbs_ref[...]))
    o_ref[...] = x_ref[...] + mod_ref[2:3, :] * _dot(merged.astype(BF16), wo_ref[...])


def _mix_bwd(x, fwd, y_pool, mod, lw, h0_b, layer, emit_state):
    b, t, _ = x.shape
    nblk = t // TB_MIX
    zero_init = h0_b is None
    z, gates, dt, xt, bcv, ccv, y1 = fwd
    mod_idx = (lambda i: i) if mod.shape[0] == b else (lambda i: 0)
    tok = lambda n: pl.BlockSpec((None, TB_MIX, n), lambda i, j: (i, nblk - 1 - j, 0))
    chunked = pl.BlockSpec((None, CHUNKS_PER_TILE, D_INNER, SSD_CHUNK), lambda i, j: (i, nblk - 1 - j, 0, 0))
    in_specs = [
        tok(D_MODEL), chunked, tok(BC_DIM), tok(BC_DIM), tok(LANES), chunked, tok(D_INNER), tok(2 * D_MODEL),
        tok(POOL_DIM),
        pl.BlockSpec((None, 3, D_MODEL), lambda i, j: (mod_idx(i), 0, 0)),
        _resident((1, LANES)),
        _resident((LANES, LANES)),
        _resident((1, D_INNER)),
        _resident((POOL_DIM, D_MODEL), layer),
        _resident((D_INNER, D_MODEL), layer),
        _resident((D_MODEL, D_MODEL), layer),
    ]
    args = [x, xt, bcv, ccv, dt, y1, z, gates, y_pool, mod, lw["a_log"], lw["d_skip"], lw["ssd_norm"],
            lw["w_branch_pool"], lw["w_branch_ssd"], lw["w_out"]]
    if not zero_init:
        in_specs.append(pl.BlockSpec((None, None, D_INNER, SSD_STATE), lambda i, j: (i, layer, 0, 0)))
        args.append(h0_b)
    out_specs = [tok(D_MODEL)]
    out_shape = [jax.ShapeDtypeStruct((b, t, D_MODEL), F32)]
    if emit_state:
        out_specs.append(pl.BlockSpec((None, D_INNER, SSD_STATE), lambda i, j: (i, 0, 0)))
        out_shape.append(jax.ShapeDtypeStruct((b, D_INNER, SSD_STATE), F32))
    return pl.pallas_call(
        functools.partial(_mixb_kernel, zero_init=zero_init, emit_state=emit_state, nblk=nblk),
        grid=(b, nblk), in_specs=in_specs, out_specs=out_specs, out_shape=out_shape,
        scratch_shapes=[pltpu.VMEM((TB_MIX, D_INNER), F32), pltpu.VMEM((D_INNER, SSD_STATE), F32)],
        compiler_params=_params(("parallel", "arbitrary")),
        name="mix_bwd",
    )(*args)


def _layer(x, mod, rows, h0_f, h0_b, layer, lw, emit_state, final_g):
    ng = lw["norm_g"]
    x = _ffn(x, mod[:, 0:3], ng[0:1], lw["ffn1_w13"], lw["ffn1_w2"], layer)
    mod2 = mod[:, 3:6]
    fwd = _mix_fwd(x, mod2, ng[1:2], lw, h0_f, layer, emit_state)
    hf = fwd[8] if emit_state else None
    y_pool = _pool_mixer(fwd[0], lw["pool_w"], lw["pool_scale"], rows, layer)
    res = _mix_bwd(x, fwd[1:8], y_pool, mod2, lw, h0_b, layer, emit_state)
    x, hb = (res[0], res[1]) if emit_state else (res[0], None)
    x = _ffn(x, mod[:, 6:9], ng[2:3], lw["ffn2_w13"], lw["ffn2_w2"], layer, final_g)
    return x, hf, hb


def _prep_weights(p):
    s3 = W_MAIN_COLS
    s4 = s3 + 2 * SSD_HEADS
    w_in = p["w_in"]
    w_tail = jnp.concatenate(
        [jnp.pad(w_in[:, :, s3:s4], ((0, 0), (0, 0), (0, LANES - 2 * SSD_HEADS))), w_in[:, :, s4:]], axis=2)
    stacked = {k: p[k].astype(BF16) for k in ("ffn1_w13", "ffn1_w2", "ffn2_w13", "ffn2_w2", "w_in", "pool_w",
                                              "w_branch_pool", "w_branch_ssd", "w_out")}
    stacked["w_tail"] = w_tail.astype(BF16)
    return stacked


def _prep_layer(l, p, stacked):
    pad_lanes = lambda v: jnp.pad(v.reshape(1, -1), ((0, 0), (0, LANES - v.size)))
    return dict(
        stacked,
        norm_g=p["norm_g"][l],
        dt_bias=pad_lanes(p["dt_bias"][l]),
        a_log=pad_lanes(p["a_log"][l]),
        d_skip=jnp.broadcast_to(pad_lanes(p["d_skip"][l]).reshape(LANES, 1), (LANES, LANES)),
        pool_scale=p["pool_scale"][l].reshape(1, POOL_DIM),
        conv_w=p["conv_w"][l], conv_b=p["conv_b"][l].reshape(1, CONV_DIM),
        ssd_norm=p["ssd_norm"][l].reshape(1, D_INNER),
    )


def kernel(x_prompt, x_sample, state_ssd_fwd, state_ssd_bwd, c, c_ctx, ada_w, ada_b, norm_g, ffn1_w13, ffn1_w2, ffn2_w13, ffn2_w2, w_in, pool_w, pool_scale, conv_w, conv_b, a_log, dt_bias, d_skip, ssd_norm, w_branch_pool, w_branch_ssd, w_out, final_norm):
    p = dict(norm_g=norm_g, ffn1_w13=ffn1_w13, ffn1_w2=ffn1_w2, ffn2_w13=ffn2_w13, ffn2_w2=ffn2_w2,
             w_in=w_in, pool_w=pool_w, pool_scale=pool_scale, conv_w=conv_w, conv_b=conv_b, a_log=a_log,
             dt_bias=dt_bias, d_skip=d_skip, ssd_norm=ssd_norm, w_branch_pool=w_branch_pool,
             w_branch_ssd=w_branch_ssd, w_out=w_out)
    nb = c.shape[0]
    rows = x_sample.shape[1] // GRID_W
    cond = jnp.concatenate([c_ctx[None, :], c], axis=0)
    cond = jnp.pad(cond, ((0, -(nb + 1) % SUBLANES), (0, 0)))
    mod = _adaln(cond, ada_w, ada_b).reshape(DEPTH, cond.shape[0], N_MOD, D_MODEL)
    fg = final_norm.reshape(1, D_MODEL)
    sd = state_ssd_fwd.shape
    h0_f = state_ssd_fwd.reshape(sd[0], DEPTH, D_INNER, SSD_STATE)
    h0_b = state_ssd_bwd.reshape(sd[0], DEPTH, D_INNER, SSD_STATE)
    xc, xl = x_prompt, x_sample
    new_f, new_b = [], []
    stacked = _prep_weights(p)
    for l in range(DEPTH):
        lw = _prep_layer(l, p, stacked)
        last = fg if l == DEPTH - 1 else None
        xc, hf, hb = _layer(xc, mod[l, 0:1], None, None, None, l, lw, True, last)
        new_f.append(hf.reshape(-1, SSD_HEADS, SSD_HEAD_DIM, SSD_STATE))
        new_b.append(hb.reshape(-1, SSD_HEADS, SSD_HEAD_DIM, SSD_STATE))
        xl, _, _ = _layer(xl, mod[l, 1:1 + nb], rows, h0_f, h0_b, l, lw, False, last)
    return (xc, xl, jnp.stack(new_f, axis=1), jnp.stack(new_b, axis=1))
```
